```python
import math
import jax
import jax.numpy as jnp
from jax import lax
import numpy as np

D_MODEL = 1024
BATCH = 8
SEQ = 4096
DEPTH = 4

GRID_W = 64
CTX_LEN = 256

ATT_HEADS = 4
ATT_QK = 64
ATT_V = 2 * ATT_QK
ATT_WIDTH = ATT_HEADS * ATT_V
AXIS_DIM = ATT_QK // 2
ROPE_THETA = 10000.0
Q_BLOCK = 128
SUBLN_EPS = 1e-5

RWKV_HEADS = 4
RWKV_N = 64
RWKV_WIDTH = RWKV_HEADS * RWKV_N
DECAY_LORA = 32
ICLR_LORA = 32
GATE_LORA = 64
N_DIR = 2
GN_EPS = 64e-5

CONV_WIDTH = 256
CONV_K = 3

MIX_WIDTH = ATT_WIDTH + RWKV_WIDTH + CONV_WIDTH
IN_SIZES = (ATT_WIDTH, ATT_WIDTH, ATT_WIDTH,
            RWKV_WIDTH, RWKV_WIDTH, RWKV_WIDTH, N_DIR * DECAY_LORA, N_DIR * ICLR_LORA, GATE_LORA,
            CONV_WIDTH, CONV_WIDTH, CONV_WIDTH)
IN_WIDTH = sum(IN_SIZES)
IN_OFFSETS = tuple(int(o) for o in np.cumsum(IN_SIZES)[:-1])

D_FF = 2816
N_EXPERTS = 8
TOP_K = 2
D_FF_EXPERT = 1408
N_DENSE = (DEPTH + 1) // 2
N_MOE = DEPTH // 2

DN_ALPHA = (2 * DEPTH) ** 0.25
DN_BETA = (8 * DEPTH) ** -0.25
LN_EPS = 1e-5

kernel_name = 'hybrid_diffattn_rwkv7_shortconv_moe_dit'


def layer_norm(x, g, b, eps=LN_EPS):
    xf = x.astype(jnp.float32)
    mu = xf.mean(-1, keepdims=True)
    var = jnp.square(xf - mu).mean(-1, keepdims=True)
    return ((xf - mu) * lax.rsqrt(var + eps) * g + b).astype(x.dtype)


def rms_norm(x, g, eps):
    xf = x.astype(jnp.float32)
    return (xf * lax.rsqrt(jnp.mean(xf * xf, -1, keepdims=True) + eps) * g).astype(x.dtype)


def head_norm(y, g, b, eps):
    yf = y.astype(jnp.float32)
    mu = yf.mean(-1, keepdims=True)
    var = jnp.square(yf - mu).mean(-1, keepdims=True)
    yn = (yf - mu) * lax.rsqrt(var + eps)
    return yn.reshape(y.shape[:-2] + (-1,)) * g + b


def short_conv(u, w):
    return lax.conv_general_dilated(
        u, w[:, None, :].astype(u.dtype), window_strides=(1,),
        padding=((CONV_K // 2, CONV_K // 2),), dimension_numbers=('NWC', 'WIO', 'NWC'),
        feature_group_count=u.shape[-1])


def axial_rope(x, row, col):
    b, t, h, m, _ = x.shape
    inv = ROPE_THETA ** (-jnp.arange(0, AXIS_DIM, 2, dtype=jnp.float32) / AXIS_DIM)
    ang = jnp.stack([row, col], -1).astype(jnp.float32)[..., None] * inv
    cos = jnp.cos(ang)[None, :, None, None]
    sin = jnp.sin(ang)[None, :, None, None]
    xf = x.astype(jnp.float32).reshape(b, t, h, m, 2, 2, AXIS_DIM // 2)
    x1, x2 = xf[..., 0, :], xf[..., 1, :]
    out = jnp.stack([x1 * cos - x2 * sin, x2 * cos + x1 * sin], -2)
    return out.reshape(x.shape).astype(x.dtype)


def diff_attend(q, k, v, lam):
    s = jnp.einsum('bqhmd,bkhmd->bhmqk', q, k).astype(jnp.float32) * (ATT_QK ** -0.5)
    p = jax.nn.softmax(s, axis=-1)
    w = p[:, :, 0] - lam * p[:, :, 1]
    return jnp.einsum('bhqk,bkhd->bqhd', w.astype(v.dtype), v)


def blockwise_diff_attend(q, k, v, lam):
    b, t = q.shape[:2]
    nb = t // Q_BLOCK
    qb = jnp.moveaxis(q.reshape((b, nb, Q_BLOCK) + q.shape[2:]), 1, 0)
    ob = lax.map(lambda qi: diff_attend(qi, k, v, lam), qb)
    return jnp.moveaxis(ob, 0, 1).reshape(b, t, ATT_HEADS, ATT_V)


def rwkv_prepare(r, k, v, wd, ad, gd, conv_w, w0, w_up, a0, a_up, g_up, k_k, k_a):
    r, k, v = jnp.split(short_conv(jnp.concatenate([r, k, v], -1), conv_w), 3, axis=-1)
    b, t, _ = r.shape
    wl = w0[:, None, None, :] + jnp.einsum('btdr,drc->dbtc', jnp.tanh(wd.reshape(b, t, N_DIR, DECAY_LORA)), w_up)
    decay = jnp.exp(-jnp.exp(-jax.nn.softplus(-wl.astype(jnp.float32)) - 0.5))
    a = jax.nn.sigmoid(a0[:, None, None, :] + jnp.einsum('btdr,drc->dbtc', ad.reshape(b, t, N_DIR, ICLR_LORA), a_up))
    g = jax.nn.sigmoid(gd) @ g_up
    kk = (k * k_k).reshape(b, t, RWKV_HEADS, RWKV_N).astype(jnp.float32)
    kk = (kk * lax.rsqrt(jnp.maximum(jnp.sum(kk * kk, -1, keepdims=True), 1e-24))).reshape(b, t, RWKV_WIDTH)
    kd = k * (1.0 + (a - 1.0) * k_a)
    return r, v, kk, decay, a, kd, g


def rwkv_scan(state0, r, v, kk, decay, a, kd, emit):
    def to_scan(u):
        u = jnp.stack([u[0], jnp.flip(u[1], 1)]).astype(jnp.float32)
        d, b, t, _ = u.shape
        return jnp.transpose(u, (2, 0, 1, 3)).reshape(t, d, b, RWKV_HEADS, RWKV_N)

    def both(u):
        return jnp.stack([u, u])

    xs = (to_scan(decay), to_scan(kd), to_scan(both(v)), to_scan(both(kk)), to_scan(a))
    if emit:
        xs = xs + (to_scan(both(r)),)

    def step(s, inp):
        w_t, k_t, v_t, kk_t, a_t = inp[:5]
        sa = jnp.einsum('dbhvk,dbhk->dbhv', s, -kk_t)
        s = s * w_t[..., None, :] + sa[..., None] * (kk_t * a_t)[..., None, :] + v_t[..., None] * k_t[..., None, :]
        y = jnp.einsum('dbhvk,dbhk->dbhv', s, inp[5]) if emit else None
        return s, y

    s, y = lax.scan(step, state0, xs)
    if not emit:
        return s, None
    y = y[:, 0] + jnp.flip(y[:, 1], 0)
    return s, jnp.transpose(y, (1, 0, 2, 3))


def rwkv_output(y, r, kd, v, g, r_k, gn_g, gn_b):
    b, t, _ = r.shape
    hs = lambda u: u.reshape(u.shape[:-1] + (RWKV_HEADS, RWKV_N))
    yn = head_norm(y, gn_g, gn_b, GN_EPS)
    bonus = jnp.sum(hs(r) * hs(kd) * r_k, axis=(0, -1)).astype(jnp.float32)
    out = yn + (bonus[..., None] * hs(v)).reshape(b, t, RWKV_WIDTH)
    return (out * g).astype(r.dtype)


def token_mixer(hl, hc, row, col, lam, lam_init, with_ctx, w_in, subln_g, rkv_conv, decay_w0, decay_up,
                iclr_a0, iclr_up, gate_up, k_k, k_a, r_k, gn_g, gn_b, conv_w, w_out):
    (q_l, k_l, v_l, rr_l, rk_l, rv_l, wd_l, ad_l, gd_l, ch_l, cb_l, cc_l) = jnp.split(hl @ w_in, IN_OFFSETS, axis=-1)
    (q_c, k_c, v_c, rr_c, rk_c, rv_c, wd_c, ad_c, gd_c, ch_c, cb_c, cc_c) = jnp.split(hc @ w_in, IN_OFFSETS, axis=-1)
    qk_heads = lambda u: u.reshape(u.shape[:2] + (ATT_HEADS, 2, ATT_QK))
    v_heads = lambda u: u.reshape(u.shape[:2] + (ATT_HEADS, ATT_V))
    att_post = lambda o: (rms_norm(o, subln_g, SUBLN_EPS) * (1.0 - lam_init)).reshape(o.shape[:2] + (ATT_WIDTH,))

    kc, vc = qk_heads(k_c), v_heads(v_c)
    k_all = jnp.concatenate([kc, axial_rope(qk_heads(k_l), row, col)], axis=1)
    v_all = jnp.concatenate([vc, v_heads(v_l)], axis=1)
    att_l = att_post(blockwise_diff_attend(axial_rope(qk_heads(q_l), row, col), k_all, v_all, lam))

    rw_args = (rkv_conv, decay_w0, decay_up, iclr_a0, iclr_up, gate_up, k_k, k_a)
    r_c, vr_c, kk_c, dec_c, a_c, kd_c, g_c = rwkv_prepare(rr_c, rk_c, rv_c, wd_c, ad_c, gd_c, *rw_args)
    r_l, vr_l, kk_l, dec_l, a_l, kd_l, g_l = rwkv_prepare(rr_l, rk_l, rv_l, wd_l, ad_l, gd_l, *rw_args)
    state0 = jnp.zeros((N_DIR, hl.shape[0], RWKV_HEADS, RWKV_N, RWKV_N), jnp.float32)
    s_ctx, y_c = rwkv_scan(state0, r_c, vr_c, kk_c, dec_c, a_c, kd_c, with_ctx)
    _, y_l = rwkv_scan(s_ctx, r_l, vr_l, kk_l, dec_l, a_l, kd_l, True)
    rwkv_l = rwkv_output(y_l, r_l, kd_l, vr_l, g_l, r_k, gn_g, gn_b)

    conv_l = cb_l * short_conv(cc_l * ch_l, conv_w)

    out_l = jnp.concatenate([att_l, rwkv_l, conv_l], axis=-1) @ w_out
    if not with_ctx:
        return out_l, None
    att_c = att_post(diff_attend(qk_heads(q_c), kc, vc, lam))
    rwkv_c = rwkv_output(y_c, r_c, kd_c, vr_c, g_c, r_k, gn_g, gn_b)
    conv_c = cb_c * short_conv(cc_c * ch_c, conv_w)
    out_c = jnp.concatenate([att_c, rwkv_c, conv_c], axis=-1) @ w_out
    return out_l, out_c


def swiglu(h, w1, w3, w2):
    return (jax.nn.silu(h @ w1) * (h @ w3)) @ w2


def moe_swiglu(h, router_w, router_b, w1, w3, w2):
    logits = (h @ router_w).astype(jnp.float32) + router_b
    top_logit, top_idx = lax.top_k(logits, TOP_K)
    top_gate = jax.nn.softmax(top_logit, axis=-1)
    gates = jnp.sum(jax.nn.one_hot(top_idx, N_EXPERTS, dtype=jnp.float32) * top_gate[..., None], axis=-2)
    out = jnp.zeros_like(h)
    for e in range(N_EXPERTS):
        out = out + gates[..., e:e + 1].astype(h.dtype) * swiglu(h, w1[e], w3[e], w2[e])
    return out


def setup_inputs(seed: int = 0) -> dict:
    key = jax.random.key(seed)
    keys = iter(jax.random.split(key, 48))

    def nrm(shape, scale):
        return jax.random.normal(next(keys), shape, jnp.float32) * scale

    L, D = DEPTH, D_MODEL
    centre_tap = jnp.array([0.0, 1.0, 0.0], jnp.float32)[None, :, None]
    return {
        'x': nrm((BATCH, SEQ, D), 1.0),
        'c': nrm((BATCH, D), 1.0),
        'ctx': nrm((BATCH, CTX_LEN, D), 1.0),
        'c_ctx': nrm((D,), 1.0),
        'w_ada': nrm((L, D, 6 * D), 0.5 * D ** -0.5),
        'b_ada': nrm((L, 6 * D), 0.02),
        'w_in': nrm((L, D, IN_WIDTH), D ** -0.5),
        'w_out': nrm((L, MIX_WIDTH, D), DN_BETA * MIX_WIDTH ** -0.5),
        'ln1_g': 1.0 + nrm((L, D), 0.02),
        'ln1_b': nrm((L, D), 0.02),
        'ln2_g': 1.0 + nrm((L, D), 0.02),
        'ln2_b': nrm((L, D), 0.02),
        'lam_q1': nrm((L, ATT_QK), 0.1),
        'lam_k1': nrm((L, ATT_QK), 0.1),
        'lam_q2': nrm((L, ATT_QK), 0.1),
        'lam_k2': nrm((L, ATT_QK), 0.1),
        'subln_g': 1.0 + nrm((L, ATT_V), 0.02),
        'rkv_conv': nrm((L, CONV_K, 3 * RWKV_WIDTH), 0.2) + centre_tap,
        'decay_w0': nrm((L, N_DIR, RWKV_WIDTH), 1.0) - 2.0,
        'decay_up': nrm((L, N_DIR, DECAY_LORA, RWKV_WIDTH), 0.1),
        'iclr_a0': nrm((L, N_DIR, RWKV_WIDTH), 0.5),
        'iclr_up': nrm((L, N_DIR, ICLR_LORA, RWKV_WIDTH), 0.1),
        'gate_up': nrm((L, GATE_LORA, RWKV_WIDTH), GATE_LORA ** -0.5),
        'k_k': 0.85 + nrm((L, RWKV_WIDTH), 0.05),
        'k_a': 1.0 + nrm((L, RWKV_WIDTH), 0.05),
        'r_k': nrm((L, RWKV_HEADS, RWKV_N), 0.1),
        'gn_g': 1.0 + nrm((L, RWKV_WIDTH), 0.02),
        'gn_b': nrm((L, RWKV_WIDTH), 0.02),
        'conv_w': nrm((L, CONV_K, CONV_WIDTH), CONV_K ** -0.5),
        'ffn_w1': nrm((N_DENSE, D, D_FF), D ** -0.5),
        'ffn_w3': nrm((N_DENSE, D, D_FF), D ** -0.5),
        'ffn_w2': nrm((N_DENSE, D_FF, D), DN_BETA * D_FF ** -0.5),
        'router_w': nrm((N_MOE, D, N_EXPERTS), D ** -0.5),
        'router_b': nrm((N_MOE, N_EXPERTS), 0.01),
        'moe_w1': nrm((N_MOE, N_EXPERTS, D, D_FF_EXPERT), D ** -0.5),
        'moe_w3': nrm((N_MOE, N_EXPERTS, D, D_FF_EXPERT), D ** -0.5),
        'moe_w2': nrm((N_MOE, N_EXPERTS, D_FF_EXPERT, D), DN_BETA * D_FF_EXPERT ** -0.5),
    }


def reference(x, c, ctx, c_ctx, w_ada, b_ada, w_in, w_out, ln1_g, ln1_b, ln2_g, ln2_b,
              lam_q1, lam_k1, lam_q2, lam_k2, subln_g, rkv_conv, decay_w0, decay_up, iclr_a0, iclr_up,
              gate_up, k_k, k_a, r_k, gn_g, gn_b, conv_w, ffn_w1, ffn_w3, ffn_w2,
              router_w, router_b, moe_w1, moe_w3, moe_w2):
    t = x.shape[1]
    rows = t // GRID_W
    row = jnp.repeat(jnp.arange(rows), GRID_W)
    col = jnp.tile(jnp.arange(GRID_W), rows)
    xl, xc = x, ctx
    sc, scc = jax.nn.silu(c), jax.nn.silu(c_ctx)
    for l in range(DEPTH):
        with_ctx = l < DEPTH - 1
        mod_l = jnp.split((sc @ w_ada[l] + b_ada[l])[:, None, :], 6, axis=-1)
        mod_c = jnp.split(scc @ w_ada[l] + b_ada[l], 6, axis=-1)

        lam_init = 0.8 - 0.6 * math.exp(-0.3 * l)
        lam = (jnp.exp(jnp.sum(lam_q1[l].astype(jnp.float32) * lam_k1[l].astype(jnp.float32)))
               - jnp.exp(jnp.sum(lam_q2[l].astype(jnp.float32) * lam_k2[l].astype(jnp.float32))) + lam_init)
        hl = xl * (1.0 + mod_l[1]) + mod_l[0]
        hc = xc * (1.0 + mod_c[1]) + mod_c[0]
        mix_l, mix_c = token_mixer(hl, hc, row, col, lam, lam_init, with_ctx, w_in[l], subln_g[l], rkv_conv[l],
                                   decay_w0[l], decay_up[l], iclr_a0[l], iclr_up[l], gate_up[l], k_k[l], k_a[l],
                                   r_k[l], gn_g[l], gn_b[l], conv_w[l], w_out[l])
        xl = layer_norm(DN_ALPHA * xl + mod_l[2] * mix_l, ln1_g[l], ln1_b[l])
        if with_ctx:
            xc = layer_norm(DN_ALPHA * xc + mod_c[2] * mix_c, ln1_g[l], ln1_b[l])

        def channel_mixer(h):
            j = l // 2
            if l % 2 == 0:
                return swiglu(h, ffn_w1[j], ffn_w3[j], ffn_w2[j])
            return moe_swiglu(h, router_w[j], router_b[j], moe_w1[j], moe_w3[j], moe_w2[j])

        hl = xl * (1.0 + mod_l[4]) + mod_l[3]
        xl = layer_norm(DN_ALPHA * xl + mod_l[5] * channel_mixer(hl), ln2_g[l], ln2_b[l])
        if with_ctx:
            hc = xc * (1.0 + mod_c[4]) + mod_c[3]
            xc = layer_norm(DN_ALPHA * xc + mod_c[5] * channel_mixer(hc), ln2_g[l], ln2_b[l])
    return xl
```

```python
import functools
import math

import numpy as np
import jax
import jax.numpy as jnp
from jax import lax
from jax.experimental import pallas as pl
from jax.experimental.pallas import tpu as pltpu

F32 = jnp.float32
BF16 = jnp.bfloat16
HI = lax.Precision.HIGHEST

D_MODEL = 1024
DEPTH = 4
GRID_W = 64
ATT_HEADS = 4
ATT_QK = 64
ATT_V = 128
ATT_WIDTH = 512
AXIS_DIM = 32
ROPE_THETA = 10000.0
SUBLN_EPS = 1e-5
RWKV_HEADS = 4
RWKV_N = 64
RWKV_WIDTH = 256
DECAY_LORA = 32
ICLR_LORA = 32
GATE_LORA = 64
GN_EPS = 64e-5
CONV_WIDTH = 256
D_FF = 2816
N_EXPERTS = 8
D_FF_EXPERT = 1408
DN_ALPHA = (2 * DEPTH) ** 0.25
LN_EPS = 1e-5

LANES = 128
MOD_ROWS = 16
QKV_W = 3 * ATT_WIDTH
RW_W = 1024
CV_W = 3 * CONV_WIDTH
SCAN_CHUNK = 64
VMEM_LIMIT = 56 * 1024 * 1024


def _cparams(sem):
    return pltpu.CompilerParams(dimension_semantics=sem, vmem_limit_bytes=VMEM_LIMIT)


def _dot(a, b):
    return jnp.dot(a, b, preferred_element_type=F32)


def _dot_hi(a, b):
    return jnp.dot(a, b, precision=HI, preferred_element_type=F32)


def _dot_nt_hi(a, b):
    return lax.dot_general(a, b, (((1,), (1,)), ((), ())), precision=HI, preferred_element_type=F32)


def _dot_tn_hi(a, b):
    return lax.dot_general(a, b, (((0,), (0,)), ((), ())), precision=HI, preferred_element_type=F32)


def _layer_norm(z, g, b):
    mu = jnp.mean(z, axis=-1, keepdims=True)
    zc = z - mu
    var = jnp.mean(zc * zc, axis=-1, keepdims=True)
    return zc * lax.rsqrt(var + LN_EPS) * g + b


def _head_ones(width, head):
    r = lax.broadcasted_iota(jnp.int32, (width, width), 0) // head
    c = lax.broadcasted_iota(jnp.int32, (width, width), 1) // head
    return (r == c).astype(F32)


def _ada_kernel(c_ref, w_ref, b_ref, o_ref):
    c = c_ref[...]
    sc = c * jax.nn.sigmoid(c)
    o_ref[...] = _dot(sc.astype(BF16), w_ref[...].astype(BF16)) + b_ref[...]


def _adaln(c_all, w_ada, b_ada):
    depth, d, n = w_ada.shape
    tn = 1536
    return pl.pallas_call(
        _ada_kernel,
        grid=(depth, n // tn),
        in_specs=[
            pl.BlockSpec((MOD_ROWS, d), lambda l, j: (0, 0)),
            pl.BlockSpec((None, d, tn), lambda l, j: (l, 0, j)),
            pl.BlockSpec((None, 1, tn), lambda l, j: (l, 0, j)),
        ],
        out_specs=pl.BlockSpec((None, MOD_ROWS, tn), lambda l, j: (l, 0, j)),
        out_shape=jax.ShapeDtypeStruct((depth, MOD_ROWS, n), F32),
        compiler_params=_cparams(("parallel", "parallel")),
        name="adaln",
    )(c_all, w_ada, b_ada.reshape(depth, 1, n))


class _Layout:
    def __init__(self, batch, ctx_len, seq):
        self.batch, self.ctx_len, self.seq = batch, ctx_len, seq
        self.n_lat = batch * seq
        self.n_tok = self.n_lat + batch * ctx_len

    def mod_spec(self, layer, tm, which):
        n_lat_tiles = self.n_lat // tm
        rows_per = self.seq // tm
        batch = self.batch

        def imap(i, *_):
            row = jnp.where(i < n_lat_tiles, i // rows_per, batch)
            return (layer * MOD_ROWS + row, 0, which)

        return pl.BlockSpec((None, 1, D_MODEL), imap)


def _inproj_kernel(x_ref, sh_ref, sc_ref, wq_ref, wr_ref, wc_ref, cos_ref, sin_ref,
                   oq_ref, or_ref, oc_ref):
    h = (x_ref[...] * (1.0 + sc_ref[...]) + sh_ref[...]).astype(BF16)
    qkv = _dot(h, wq_ref[...])
    cos = cos_ref[...]
    sin = sin_ref[...]
    lane = lax.broadcasted_iota(jnp.int32, cos.shape, 1)
    first_half = (lane % AXIS_DIM) < (AXIS_DIM // 2)
    for s in range(2 * ATT_WIDTH // LANES):
        seg = qkv[:, s * LANES:(s + 1) * LANES]
        swapped = jnp.where(first_half, pltpu.roll(seg, LANES - AXIS_DIM // 2, 1),
                            pltpu.roll(seg, AXIS_DIM // 2, 1))
        rot = seg * cos + swapped * sin
        if s < ATT_WIDTH // LANES:
            rot = rot * (ATT_QK ** -0.5)
        oq_ref[:, s * LANES:(s + 1) * LANES] = rot.astype(BF16)
    oq_ref[:, 2 * ATT_WIDTH:] = qkv[:, 2 * ATT_WIDTH:].astype(BF16)
    or_ref[...] = _dot(h, wr_ref[...])
    oc_ref[...] = _dot(h, wc_ref[...])


def _inproj(lay, layer, x, mods, wq, wr, wc, cos, sin, tm):
    n_lat_tiles = lay.n_lat // tm
    ctx_tiles_per = lay.ctx_len // tm
    lat_tiles_per = lay.seq // tm

    def rope_map(i):
        return (jnp.where(i < n_lat_tiles, i % lat_tiles_per,
                          lat_tiles_per + (i - n_lat_tiles) % ctx_tiles_per), 0)

    row = lambda i: (i, 0)
    wmap = lambda i: (layer, 0, 0)
    return pl.pallas_call(
        _inproj_kernel,
        grid=(lay.n_tok // tm,),
        in_specs=[
            pl.BlockSpec((tm, D_MODEL), row),
            lay.mod_spec(layer, tm, 0),
            lay.mod_spec(layer, tm, 1),
            pl.BlockSpec((None, D_MODEL, QKV_W), wmap),
            pl.BlockSpec((None, D_MODEL, RW_W), wmap),
            pl.BlockSpec((None, D_MODEL, CV_W), wmap),
            pl.BlockSpec((tm, LANES), rope_map),
            pl.BlockSpec((tm, LANES), rope_map),
        ],
        out_specs=[
            pl.BlockSpec((tm, QKV_W), row),
            pl.BlockSpec((tm, RW_W), row),
            pl.BlockSpec((tm, CV_W), row),
        ],
        out_shape=[
            jax.ShapeDtypeStruct((lay.n_tok, QKV_W), BF16),
            jax.ShapeDtypeStruct((lay.n_tok, RW_W), F32),
            jax.ShapeDtypeStruct((lay.n_tok, CV_W), F32),
        ],
        compiler_params=_cparams(("parallel",)),
        name="inproj",
    )(x, mods, mods, wq, wr, wc, cos, sin)


def _attn_kernel(q_ref, kc_ref, vc_ref, kl_ref, vl_ref, lam_ref, g_ref, o_ref, *, lam_init, n_lat_q, tq):
    q = q_ref[...]
    lane = lax.broadcasted_iota(jnp.int32, q.shape, 1)
    zero = jnp.zeros_like(q)
    qq = jnp.concatenate([jnp.where(lane < ATT_QK, q, zero), jnp.where(lane >= ATT_QK, q, zero)], axis=0)
    lam4 = lam_ref[...]
    lam = (jnp.exp(jnp.sum(lam4[0:1] * lam4[1:2], axis=-1, keepdims=True))
           - jnp.exp(jnp.sum(lam4[2:3] * lam4[3:4], axis=-1, keepdims=True)) + lam_init)

    def scores(k_ref):
        return lax.dot_general(qq, k_ref[...], (((1,), (1,)), ((), ())), preferred_element_type=F32)

    def finish(o, l):
        o = o / l
        dlt = o[:tq] - lam * o[tq:]
        ms = jnp.mean(dlt * dlt, axis=-1, keepdims=True)
        o_ref[...] = (dlt * lax.rsqrt(ms + SUBLN_EPS) * g_ref[...] * (1.0 - lam_init)).astype(o_ref.dtype)

    is_ctx = pl.program_id(2) >= n_lat_q

    @pl.when(is_ctx)
    def _():
        s = scores(kc_ref)
        p = jnp.exp(s - jnp.max(s, axis=-1, keepdims=True))
        finish(_dot(p.astype(BF16), vc_ref[...]), jnp.sum(p, axis=-1, keepdims=True))

    @pl.when(jnp.logical_not(is_ctx))
    def _():
        sc = scores(kc_ref)
        sl = scores(kl_ref)
        m = jnp.maximum(jnp.max(sc, axis=-1, keepdims=True), jnp.max(sl, axis=-1, keepdims=True))
        pc = jnp.exp(sc - m)
        pll = jnp.exp(sl - m)
        l = jnp.sum(pc, axis=-1, keepdims=True) + jnp.sum(pll, axis=-1, keepdims=True)
        finish(_dot(pc.astype(BF16), vc_ref[...]) + _dot(pll.astype(BF16), vl_ref[...]), l)


def _attention(lay, layer, qkv, lam4, subln_g, lam_init, tq):
    n_ctx_q = lay.ctx_len // tq
    n_lat_q = lay.seq // tq
    n_lat_tiles = lay.n_lat // tq
    ctx_blk0 = lay.n_lat // lay.ctx_len
    hq = ATT_WIDTH // LANES

    def qmap(b, h, qi):
        return (jnp.where(qi < n_lat_q, b * n_lat_q + qi, n_lat_tiles + b * n_ctx_q + (qi - n_lat_q)), h)

    kern = functools.partial(_attn_kernel, lam_init=lam_init, n_lat_q=n_lat_q, tq=tq)
    return pl.pallas_call(
        kern,
        grid=(lay.batch, ATT_HEADS, n_lat_q + n_ctx_q),
        in_specs=[
            pl.BlockSpec((tq, LANES), qmap),
            pl.BlockSpec((lay.ctx_len, LANES), lambda b, h, qi: (ctx_blk0 + b, hq + h)),
            pl.BlockSpec((lay.ctx_len, LANES), lambda b, h, qi: (ctx_blk0 + b, 2 * hq + h)),
            pl.BlockSpec((lay.seq, LANES), lambda b, h, qi: (b, hq + h)),
            pl.BlockSpec((lay.seq, LANES), lambda b, h, qi: (b, 2 * hq + h)),
            pl.BlockSpec((None, 4, ATT_QK), lambda b, h, qi: (layer, 0, 0)),
            pl.BlockSpec((None, 1, ATT_V), lambda b, h, qi: (layer, 0, 0)),
        ],
        out_specs=pl.BlockSpec((tq, LANES), qmap),
        out_shape=jax.ShapeDtypeStruct((lay.n_tok, ATT_WIDTH), BF16),
        compiler_params=_cparams(("parallel", "parallel", "arbitrary")),
        name="diff_attn",
    )(qkv, qkv, qkv, qkv, qkv, lam4, subln_g)


def _prep_kernel(rw_ref, rwp_ref, rwn_ref, cv_ref, cvp_ref, cvn_ref,
                 rkvw_ref, w0_ref, wd_ref, a0_ref, wa_ref, wg_ref, kk_w_ref, ka_ref, rk_ref, cw_ref,
                 r_ref, v_ref, kk_ref, lw_ref, bb_ref, kd_ref, bv_ref, g_ref, co_ref,
                 *, tm, n_lat, ctx_len, seq):
    i = pl.program_id(0)
    start = i * tm
    in_lat = start < n_lat
    seg_pos = jnp.where(in_lat, start % seq, (start - n_lat) % ctx_len)
    seg_len = jnp.where(in_lat, seq, ctx_len)
    has_prev = seg_pos != 0
    has_next = seg_pos + tm != seg_len

    def shifted(u, prev_row, next_row):
        rows = lax.broadcasted_iota(jnp.int32, u.shape, 0)
        prev_row = jnp.where(has_prev, prev_row, jnp.zeros_like(prev_row))
        next_row = jnp.where(has_next, next_row, jnp.zeros_like(next_row))
        up = jnp.where(rows == 0, prev_row, pltpu.roll(u, 1, 0))
        un = jnp.where(rows == tm - 1, next_row, pltpu.roll(u, tm - 1, 0))
        return up, un

    def conv3(u, prev_row, next_row, w):
        up, un = shifted(u, prev_row, next_row)
        return up * w[0:1] + u * w[1:2] + un * w[2:3]

    rw = rw_ref[...]
    nrkv = 3 * RWKV_WIDTH
    rkv = conv3(rw[:, :nrkv], rwp_ref[7:8, :nrkv], rwn_ref[0:1, :nrkv], rkvw_ref[...])
    r = rkv[:, :RWKV_WIDTH]
    k = rkv[:, RWKV_WIDTH:2 * RWKV_WIDTH]
    v = rkv[:, 2 * RWKV_WIDTH:]

    lora = rw[:, nrkv:nrkv + LANES]
    gate = rw[:, nrkv + LANES:]
    wl = _dot_hi(jnp.tanh(lora), wd_ref[...]) + w0_ref[...]
    lw = (-math.exp(-0.5)) * jax.nn.sigmoid(wl)
    a = jax.nn.sigmoid(_dot_hi(lora, wa_ref[...]) + a0_ref[...])
    g = _dot_hi(jax.nn.sigmoid(gate), wg_ref[...])

    ones = _head_ones(RWKV_WIDTH, RWKV_N)
    kraw = k * kk_w_ref[...]
    ss = _dot_hi(kraw * kraw, ones)
    kk = kraw * lax.rsqrt(jnp.maximum(ss, 1e-24))
    k2 = jnp.concatenate([k, k], axis=1)
    ka2 = jnp.concatenate([ka_ref[...], ka_ref[...]], axis=1)
    kd = k2 * (1.0 + (a - 1.0) * ka2)
    bb = jnp.concatenate([kk, kk], axis=1) * a
    bonus = _dot_hi(r * (kd[:, :RWKV_WIDTH] + kd[:, RWKV_WIDTH:]) * rk_ref[...], ones)

    r_ref[...] = r
    v_ref[...] = v
    kk_ref[...] = kk
    lw_ref[...] = lw
    bb_ref[...] = bb
    kd_ref[...] = kd
    bv_ref[...] = bonus * v
    g_ref[...] = g

    def gated(ref):
        return ref[:, 2 * CONV_WIDTH:] * ref[:, :CONV_WIDTH]

    cv_u = gated(cv_ref)
    conv = conv3(cv_u, gated(cvp_ref)[7:8], gated(cvn_ref)[0:1], cw_ref[...])
    co_ref[...] = (cv_ref[:, CONV_WIDTH:2 * CONV_WIDTH] * conv).astype(co_ref.dtype)


def _prep(lay, layer, rw, cv, rkv_conv, w0, wd, a0, wa, wg, k_k, k_a, r_k, conv_w, tm):
    n8 = lay.n_tok // 8
    t8 = tm // 8
    row = lambda i: (i, 0)
    prev = lambda i: (jnp.maximum(i * t8 - 1, 0), 0)
    nxt = lambda i: (jnp.minimum((i + 1) * t8, n8 - 1), 0)
    lmap = lambda i: (layer, 0, 0)
    w2 = 2 * RWKV_WIDTH
    kern = functools.partial(_prep_kernel, tm=tm, n_lat=lay.n_lat, ctx_len=lay.ctx_len, seq=lay.seq)
    f32 = lambda w: jax.ShapeDtypeStruct((lay.n_tok, w), F32)
    return pl.pallas_call(
        kern,
        grid=(lay.n_tok // tm,),
        in_specs=[
            pl.BlockSpec((tm, RW_W), row), pl.BlockSpec((8, RW_W), prev), pl.BlockSpec((8, RW_W), nxt),
            pl.BlockSpec((tm, CV_W), row), pl.BlockSpec((8, CV_W), prev), pl.BlockSpec((8, CV_W), nxt),
            pl.BlockSpec((None, 3, 3 * RWKV_WIDTH), lmap),
            pl.BlockSpec((None, 1, w2), lmap),
            pl.BlockSpec((None, LANES, w2), lmap),
            pl.BlockSpec((None, 1, w2), lmap),
            pl.BlockSpec((None, LANES, w2), lmap),
            pl.BlockSpec((None, LANES, RWKV_WIDTH), lmap),
            pl.BlockSpec((None, 1, RWKV_WIDTH), lmap),
            pl.BlockSpec((None, 1, RWKV_WIDTH), lmap),
            pl.BlockSpec((None, 1, RWKV_WIDTH), lmap),
            pl.BlockSpec((None, 3, CONV_WIDTH), lmap),
        ],
        out_specs=[
            pl.BlockSpec((tm, RWKV_WIDTH), row), pl.BlockSpec((tm, RWKV_WIDTH), row),
            pl.BlockSpec((tm, RWKV_WIDTH), row),
            pl.BlockSpec((tm, w2), row), pl.BlockSpec((tm, w2), row), pl.BlockSpec((tm, w2), row),
            pl.BlockSpec((tm, RWKV_WIDTH), row), pl.BlockSpec((tm, RWKV_WIDTH), row),
            pl.BlockSpec((tm, CONV_WIDTH), row),
        ],
        out_shape=[f32(RWKV_WIDTH), f32(RWKV_WIDTH), f32(RWKV_WIDTH), f32(w2), f32(w2), f32(w2),
                   f32(RWKV_WIDTH), f32(RWKV_WIDTH), jax.ShapeDtypeStruct((lay.n_tok, CONV_WIDTH), BF16)],
        compiler_params=_cparams(("parallel",)),
        name="rwkv_prep",
    )(rw, rw, rw, cv, cv, cv, rkv_conv, w0, wd, a0, wa, wg, k_k, k_a, r_k, conv_w)


def _scan_kernel(r_ref, v_ref, kk_ref, lw_ref, bb_ref, kd_ref, y_ref, st_ref):
    c = SCAN_CHUNK
    fwd = pl.program_id(1) == 0

    @pl.when(pl.program_id(2) == 0)
    def _():
        st_ref[...] = jnp.zeros_like(st_ref)

    row = lax.broadcasted_iota(jnp.int32, (c, c), 0)
    col = lax.broadcasted_iota(jnp.int32, (c, c), 1)
    order = (col - row) * jnp.where(fwd, 1, -1)
    incl = order <= 0
    strict = order < 0

    lw = lw_ref[...]
    lp = _dot_hi(incl.astype(F32), lw)
    lt = jnp.sum(lw, axis=0, keepdims=True)
    p_in = jnp.exp(lp)
    p_inv = jnp.exp(-lp)
    p_end = jnp.exp(lt - lp)
    kk = kk_ref[...]
    bb = bb_ref[...]
    kd = kd_ref[...]
    v_all = v_ref[...]
    a_t = -kk * jnp.exp(lp - lw)
    b_t = bb * p_inv
    k_t = kd * p_inv
    r_t = r_ref[...] * p_in
    k_e = kd * p_end
    b_e = bb * p_end
    p_tot = jnp.exp(lt)

    ys = []
    for h in range(RWKV_HEADS):
        sl = slice(h * RWKV_N, (h + 1) * RWKV_N)
        ah, bh, kh, rh, vh = a_t[:, sl], b_t[:, sl], k_t[:, sl], r_t[:, sl], v_all[:, sl]
        s0 = st_ref[h]
        n = jnp.where(strict, _dot_nt_hi(ah, bh), 0.0)
        mk = jnp.where(strict, _dot_nt_hi(ah, kh), 0.0)
        u = _dot_nt_hi(ah, s0) + _dot_hi(mk, vh)
        u = u + _dot_hi(n, u)
        npow = n
        for _ in range(5):
            npow = _dot_hi(npow, npow)
            u = u + _dot_hi(npow, u)
        y = (_dot_nt_hi(rh, s0)
             + _dot_hi(jnp.where(incl, _dot_nt_hi(rh, kh), 0.0), vh)
             + _dot_hi(jnp.where(incl, _dot_nt_hi(rh, bh), 0.0), u))
        ys.append(y)
        st_ref[h] = s0 * p_tot[:, sl] + _dot_tn_hi(vh, k_e[:, sl]) + _dot_tn_hi(u, b_e[:, sl])
    y_ref[...] = jnp.concatenate(ys, axis=1)


def _scan(lay, r, v, kk, lw, bb, kd):
    c = SCAN_CHUNK
    nc_ctx = lay.ctx_len // c
    nc_lat = lay.seq // c
    lat_blocks = lay.n_lat // c

    def blk(b, d, ci):
        fwd_blk = jnp.where(ci < nc_ctx, lat_blocks + b * nc_ctx + ci, b * nc_lat + (ci - nc_ctx))
        bwd_blk = jnp.where(ci < nc_ctx, lat_blocks + b * nc_ctx + (nc_ctx - 1 - ci),
                            b * nc_lat + (nc_lat - 1 - (ci - nc_ctx)))
        return jnp.where(d == 0, fwd_blk, bwd_blk)

    shared = pl.BlockSpec((c, RWKV_WIDTH), lambda b, d, ci: (blk(b, d, ci), 0))
    per_dir = pl.BlockSpec((c, RWKV_WIDTH), lambda b, d, ci: (blk(b, d, ci), d))
    return pl.pallas_call(
        _scan_kernel,
        grid=(lay.batch, 2, nc_ctx + nc_lat),
        in_specs=[shared, shared, shared, per_dir, per_dir, per_dir],
        out_specs=pl.BlockSpec((None, c, RWKV_WIDTH), lambda b, d, ci: (d, blk(b, d, ci), 0)),
        out_shape=jax.ShapeDtypeStruct((2, lay.n_tok, RWKV_WIDTH), F32),
        scratch_shapes=[pltpu.VMEM((RWKV_HEADS, RWKV_N, RWKV_N), F32)],
        compiler_params=_cparams(("parallel", "parallel", "arbitrary")),
        name="rwkv_scan",
    )(r, v, kk, lw, bb, kd)


def _outproj_kernel(att_ref, yf_ref, yb_ref, bv_ref, g_ref, co_ref, x_ref, gate_ref, w_ref,
                    gng_ref, gnb_ref, lng_ref, lnb_ref, o_ref):
    ones = _head_ones(RWKV_WIDTH, RWKV_N)
    y = yf_ref[...] + yb_ref[...]
    mu = _dot_hi(y, ones) * (1.0 / RWKV_N)
    yc = y - mu
    var = _dot_hi(yc * yc, ones) * (1.0 / RWKV_N)
    yn = yc * lax.rsqrt(var + GN_EPS) * gng_ref[...] + gnb_ref[...]
    rwkv = ((yn + bv_ref[...]) * g_ref[...]).astype(BF16)
    mix = (_dot(att_ref[...], w_ref[:ATT_WIDTH, :])
           + _dot(rwkv, w_ref[ATT_WIDTH:ATT_WIDTH + RWKV_WIDTH, :])
           + _dot(co_ref[...], w_ref[ATT_WIDTH + RWKV_WIDTH:, :]))
    z = DN_ALPHA * x_ref[...] + gate_ref[...] * mix
    o_ref[...] = _layer_norm(z, lng_ref[...], lnb_ref[...])


def _outproj(lay, layer, att, y, bv, g, co, x, mods, w_out, gn_g, gn_b, ln_g, ln_b, tm):
    row = lambda i: (i, 0)
    lmap = lambda i: (layer, 0, 0)
    return pl.pallas_call(
        _outproj_kernel,
        grid=(lay.n_tok // tm,),
        in_specs=[
            pl.BlockSpec((tm, ATT_WIDTH), row),
            pl.BlockSpec((None, tm, RWKV_WIDTH), lambda i: (0, i, 0)),
            pl.BlockSpec((None, tm, RWKV_WIDTH), lambda i: (1, i, 0)),
            pl.BlockSpec((tm, RWKV_WIDTH), row),
            pl.BlockSpec((tm, RWKV_WIDTH), row),
            pl.BlockSpec((tm, CONV_WIDTH), row),
            pl.BlockSpec((tm, D_MODEL), row),
            lay.mod_spec(layer, tm, 2),
            pl.BlockSpec((None, D_MODEL, D_MODEL), lmap),
            pl.BlockSpec((None, 1, RWKV_WIDTH), lmap),
            pl.BlockSpec((None, 1, RWKV_WIDTH), lmap),
            pl.BlockSpec((None, 1, D_MODEL), lmap),
            pl.BlockSpec((None, 1, D_MODEL), lmap),
        ],
        out_specs=pl.BlockSpec((tm, D_MODEL), row),
        out_shape=jax.ShapeDtypeStruct((lay.n_tok, D_MODEL), F32),
        compiler_params=_cparams(("parallel",)),
        name="outproj_ln1",
    )(att, y, y, bv, g, co, x, mods, w_out, gn_g, gn_b, ln_g, ln_b)


def _ffn_kernel(x_ref, sh_ref, sc_ref, gate_ref, w1_ref, w3_ref, w2_ref, lng_ref, lnb_ref, o_ref, *, n_split):
    x = x_ref[...]
    h = (x * (1.0 + sc_ref[...]) + sh_ref[...]).astype(BF16)
    step = D_FF // n_split
    f = jnp.zeros(x.shape, F32)
    for s in range(n_split):
        cols = slice(s * step, (s + 1) * step)
        a = _dot(h, w1_ref[:, cols])
        act = (a * jax.nn.sigmoid(a) * _dot(h, w3_ref[:, cols])).astype(BF16)
        f = f + _dot(act, w2_ref[cols, :])
    z = DN_ALPHA * x + gate_ref[...] * f
    o_ref[...] = _layer_norm(z, lng_ref[...], lnb_ref[...])


def _ffn(lay, layer, j, x, mods, w1, w3, w2, ln_g, ln_b, tm):
    row = lambda i: (i, 0)
    once = pl.Buffered(1)
    return pl.pallas_call(
        functools.partial(_ffn_kernel, n_split=2),
        grid=(lay.n_tok // tm,),
        in_specs=[
            pl.BlockSpec((tm, D_MODEL), row),
            lay.mod_spec(layer, tm, 3), lay.mod_spec(layer, tm, 4), lay.mod_spec(layer, tm, 5),
            pl.BlockSpec((None, D_MODEL, D_FF), lambda i: (j, 0, 0), pipeline_mode=once),
            pl.BlockSpec((None, D_MODEL, D_FF), lambda i: (j, 0, 0), pipeline_mode=once),
            pl.BlockSpec((None, D_FF, D_MODEL), lambda i: (j, 0, 0), pipeline_mode=once),
            pl.BlockSpec((None, 1, D_MODEL), lambda i: (layer, 0, 0)),
            pl.BlockSpec((None, 1, D_MODEL), lambda i: (layer, 0, 0)),
        ],
        out_specs=pl.BlockSpec((tm, D_MODEL), row),
        out_shape=jax.ShapeDtypeStruct((lay.n_tok, D_MODEL), F32),
        compiler_params=_cparams(("parallel",)),
        name="ffn_ln2",
    )(x, mods, mods, mods, w1, w3, w2, ln_g, ln_b)


def _moe_kernel(x_ref, sh_ref, sc_ref, gate_ref, rw_ref, rb_ref, w1_ref, w3_ref, w2_ref, lng_ref, lnb_ref,
                o_ref, h_ref, gates_ref, acc_ref):
    e = pl.program_id(1)
    lane = lax.broadcasted_iota(jnp.int32, gates_ref.shape, 1)

    @pl.when(e == 0)
    def _():
        h = x_ref[...] * (1.0 + sc_ref[...]) + sh_ref[...]
        h_ref[...] = h.astype(BF16)
        logits = _dot_hi(h, rw_ref[...]) + rb_ref[...]
        m1 = jnp.max(logits, axis=-1, keepdims=True)
        i1 = jnp.min(jnp.where(logits == m1, lane, LANES), axis=-1, keepdims=True)
        rest = jnp.where(lane == i1, -jnp.inf, logits)
        m2 = jnp.max(rest, axis=-1, keepdims=True)
        i2 = jnp.min(jnp.where(rest == m2, lane, LANES), axis=-1, keepdims=True)
        e2 = jnp.exp(m2 - m1)
        den = 1.0 + e2
        gates_ref[...] = jnp.where(lane == i1, 1.0 / den, 0.0) + jnp.where(lane == i2, e2 / den, 0.0)
        acc_ref[...] = jnp.zeros_like(acc_ref)

    h = h_ref[...]
    a = _dot(h, w1_ref[...])
    act = (a * jax.nn.sigmoid(a) * _dot(h, w3_ref[...])).astype(BF16)
    ge = jnp.sum(jnp.where(lane == e, gates_ref[...], 0.0), axis=-1, keepdims=True)
    acc_ref[...] += ge * _dot(act, w2_ref[...])

    @pl.when(e == N_EXPERTS - 1)
    def _():
        z = DN_ALPHA * x_ref[...] + gate_ref[...] * acc_ref[...]
        o_ref[...] = _layer_norm(z, lng_ref[...], lnb_ref[...])


def _moe(lay, layer, j, x, mods, router_w, router_b, w1, w3, w2, ln_g, ln_b, tm):
    row = lambda i, e: (i, 0)
    return pl.pallas_call(
        _moe_kernel,
        grid=(lay.n_tok // tm, N_EXPERTS),
        in_specs=[
            pl.BlockSpec((tm, D_MODEL), row),
            lay.mod_spec(layer, tm, 3), lay.mod_spec(layer, tm, 4), lay.mod_spec(layer, tm, 5),
            pl.BlockSpec((None, D_MODEL, LANES), lambda i, e: (j, 0, 0)),
            pl.BlockSpec((None, 1, LANES), lambda i, e: (j, 0, 0)),
            pl.BlockSpec((None, None, D_MODEL, D_FF_EXPERT), lambda i, e: (j, e, 0, 0)),
            pl.BlockSpec((None, None, D_MODEL, D_FF_EXPERT), lambda i, e: (j, e, 0, 0)),
            pl.BlockSpec((None, None, D_FF_EXPERT, D_MODEL), lambda i, e: (j, e, 0, 0)),
            pl.BlockSpec((None, 1, D_MODEL), lambda i, e: (layer, 0, 0)),
            pl.BlockSpec((None, 1, D_MODEL), lambda i, e: (layer, 0, 0)),
        ],
        out_specs=pl.BlockSpec((tm, D_MODEL), row),
        out_shape=jax.ShapeDtypeStruct((lay.n_tok, D_MODEL), F32),
        scratch_shapes=[pltpu.VMEM((tm, D_MODEL), BF16), pltpu.VMEM((tm, LANES), F32),
                        pltpu.VMEM((tm, D_MODEL), F32)],
        compiler_params=_cparams(("parallel", "arbitrary")),
        name="moe_ln2",
    )(x, mods, mods, mods, router_w, router_b, w1, w3, w2, ln_g, ln_b)


def _rope_tables(ctx_len, seq):
    t = np.arange(seq)
    inv = ROPE_THETA ** (-np.arange(0, AXIS_DIM, 2, dtype=np.float64) / AXIS_DIM)
    ang_r = (t // GRID_W)[:, None].astype(np.float64) * inv
    ang_c = (t % GRID_W)[:, None].astype(np.float64) * inv
    cos = np.concatenate([np.cos(ang_r), np.cos(ang_r), np.cos(ang_c), np.cos(ang_c)], axis=1)
    sin = np.concatenate([-np.sin(ang_r), np.sin(ang_r), -np.sin(ang_c), np.sin(ang_c)], axis=1)
    cos = np.concatenate([cos, np.ones((ctx_len, ATT_QK))], axis=0)
    sin = np.concatenate([sin, np.zeros((ctx_len, ATT_QK))], axis=0)
    reps = LANES // ATT_QK
    return (jnp.asarray(np.tile(cos, (1, reps)), F32), jnp.asarray(np.tile(sin, (1, reps)), F32))


def kernel(x, c, ctx, c_ctx, w_ada, b_ada, w_in, w_out, ln1_g, ln1_b, ln2_g, ln2_b,
           lam_q1, lam_k1, lam_q2, lam_k2, subln_g, rkv_conv, decay_w0, decay_up, iclr_a0, iclr_up,
           gate_up, k_k, k_a, r_k, gn_g, gn_b, conv_w, ffn_w1, ffn_w3, ffn_w2,
           router_w, router_b, moe_w1, moe_w3, moe_w2):
    batch, seq, d = x.shape
    ctx_len = ctx.shape[1]
    depth = w_in.shape[0]
    assert d == D_MODEL and depth == DEPTH and batch < MOD_ROWS
    tm = 256
    tq = 128
    tm_moe = 512
    assert ctx_len % tm == 0 and seq % tm == 0 and (batch * ctx_len) % tm_moe == 0 and seq % tm_moe == 0
    assert seq % ctx_len == 0 and ctx_len % SCAN_CHUNK == 0 and seq % GRID_W == 0
    lay = _Layout(batch, ctx_len, seq)

    tokens = jnp.concatenate([x.reshape(batch * seq, d), ctx.reshape(batch * ctx_len, d)], axis=0)

    c_all = jnp.concatenate([c, c_ctx[None, :], jnp.zeros((MOD_ROWS - batch - 1, d), F32)], axis=0)
    mods = _adaln(c_all, w_ada, b_ada).reshape(depth * MOD_ROWS, 1, 6 * d)

    cos, sin = _rope_tables(ctx_len, seq)

    o = np.cumsum([0, 512, 512, 512, 256, 256, 256, 64, 64, 64, 256, 256, 256])
    wq = w_in[:, :, :o[3]].astype(BF16)
    wr = jnp.concatenate([w_in[:, :, o[3]:o[9]], jnp.zeros((depth, d, RW_W - (o[9] - o[3])), F32)],
                         axis=-1).astype(BF16)
    wc = w_in[:, :, o[9]:].astype(BF16)
    w_out_b = w_out.astype(BF16)

    z = jnp.zeros((depth, DECAY_LORA, RWKV_WIDTH), F32)
    wd = jnp.concatenate([
        jnp.concatenate([decay_up[:, 0], z], axis=-1),
        jnp.concatenate([z, decay_up[:, 1]], axis=-1),
        jnp.zeros((depth, 2 * ICLR_LORA, 2 * RWKV_WIDTH), F32)], axis=1)
    wa = jnp.concatenate([
        jnp.zeros((depth, 2 * DECAY_LORA, 2 * RWKV_WIDTH), F32),
        jnp.concatenate([iclr_up[:, 0], z], axis=-1),
        jnp.concatenate([z, iclr_up[:, 1]], axis=-1)], axis=1)
    wg = jnp.concatenate([gate_up, jnp.zeros((depth, LANES - GATE_LORA, RWKV_WIDTH), F32)], axis=1)
    w0 = decay_w0.reshape(depth, 1, 2 * RWKV_WIDTH)
    a0 = iclr_a0.reshape(depth, 1, 2 * RWKV_WIDTH)
    vec = lambda w: w.reshape(depth, 1, -1)

    lam4 = jnp.stack([lam_q1, lam_k1, lam_q2, lam_k2], axis=1)

    ffn_w1_b, ffn_w3_b, ffn_w2_b = ffn_w1.astype(BF16), ffn_w3.astype(BF16), ffn_w2.astype(BF16)
    moe_w1_b, moe_w3_b, moe_w2_b = moe_w1.astype(BF16), moe_w3.astype(BF16), moe_w2.astype(BF16)
    n_moe = router_w.shape[0]
    router_w_p = jnp.concatenate([router_w, jnp.zeros((n_moe, d, LANES - N_EXPERTS), F32)], axis=-1)
    router_b_p = jnp.concatenate([router_b, jnp.full((n_moe, LANES - N_EXPERTS), -1e30, F32)],
                                 axis=-1).reshape(n_moe, 1, LANES)

    xs = tokens
    for l in range(depth):
        lam_init = 0.8 - 0.6 * math.exp(-0.3 * l)
        qkv, rw, cv = _inproj(lay, l, xs, mods, wq, wr, wc, cos, sin, tm)
        att = _attention(lay, l, qkv, lam4, vec(subln_g), lam_init, tq)
        r, v, kk, lw, bb, kd, bv, g, co = _prep(lay, l, rw, cv, rkv_conv, w0, wd, a0, wa, wg,
                                                vec(k_k), vec(k_a), vec(r_k), conv_w, tm)
        y = _scan(lay, r, v, kk, lw, bb, kd)
        xs = _outproj(lay, l, att, y, bv, g, co, xs, mods, w_out_b, vec(gn_g), vec(gn_b),
                      vec(ln1_g), vec(ln1_b), tm)
        if l % 2 == 0:
            xs = _ffn(lay, l, l // 2, xs, mods, ffn_w1_b, ffn_w3_b, ffn_w2_b, vec(ln2_g), vec(ln2_b), tm)
        else:
            xs = _moe(lay, l, l // 2, xs, mods, router_w_p, router_b_p, moe_w1_b, moe_w3_b, moe_w2_b,
                      vec(ln2_g), vec(ln2_b), tm_moe)
    return xs[:lay.n_lat].reshape(batch, seq, d)
```

```python
import functools
import math

import numpy as np
import jax
import jax.numpy as jnp
from jax import lax
from jax.experimental import pallas as pl
from jax.experimental.pallas import tpu as pltpu

F32 = jnp.float32
BF16 = jnp.bfloat16
HI = lax.Precision.HIGHEST

D_MODEL = 1024
DEPTH = 4
GRID_W = 64
ATT_HEADS = 4
ATT_QK = 64
ATT_V = 128
ATT_WIDTH = 512
AXIS_DIM = 32
ROPE_THETA = 10000.0
SUBLN_EPS = 1e-5
RWKV_HEADS = 4
RWKV_N = 64
RWKV_WIDTH = 256
DECAY_LORA = 32
ICLR_LORA = 32
GATE_LORA = 64
GN_EPS = 64e-5
CONV_WIDTH = 256
D_FF = 2816
N_EXPERTS = 8
D_FF_EXPERT = 1408
DN_ALPHA = (2 * DEPTH) ** 0.25
LN_EPS = 1e-5

LANES = 128
MOD_ROWS = 16
QKV_W = 3 * ATT_WIDTH
RW_W = 1024
CV_W = 3 * CONV_WIDTH
SCAN_CHUNK = 64
VMEM_LIMIT = 56 * 1024 * 1024


def _cparams(sem):
    return pltpu.CompilerParams(dimension_semantics=sem, vmem_limit_bytes=VMEM_LIMIT)


def _dot(a, b):
    return jnp.dot(a, b, preferred_element_type=F32)


def _dot_hi(a, b):
    return jnp.dot(a, b, precision=HI, preferred_element_type=F32)


def _dot_nt_hi(a, b):
    return lax.dot_general(a, b, (((1,), (1,)), ((), ())), precision=HI, preferred_element_type=F32)


def _dot_tn_hi(a, b):
    return lax.dot_general(a, b, (((0,), (0,)), ((), ())), precision=HI, preferred_element_type=F32)


def _layer_norm(z, g, b):
    mu = jnp.mean(z, axis=-1, keepdims=True)
    zc = z - mu
    var = jnp.mean(zc * zc, axis=-1, keepdims=True)
    return zc * lax.rsqrt(var + LN_EPS) * g + b


def _head_ones(width, head):
    r = lax.broadcasted_iota(jnp.int32, (width, width), 0) // head
    c = lax.broadcasted_iota(jnp.int32, (width, width), 1) // head
    return (r == c).astype(F32)


def _ada_kernel(c_ref, w_ref, b_ref, o_ref):
    c = c_ref[...]
    sc = c * jax.nn.sigmoid(c)
    o_ref[...] = _dot(sc.astype(BF16), w_ref[...].astype(BF16)) + b_ref[...]


def _adaln(c_all, w_ada, b_ada):
    depth, d, n = w_ada.shape
    tn = 1536
    return pl.pallas_call(
        _ada_kernel,
        grid=(depth, n // tn),
        in_specs=[
            pl.BlockSpec((MOD_ROWS, d), lambda l, j: (0, 0)),
            pl.BlockSpec((None, d, tn), lambda l, j: (l, 0, j)),
            pl.BlockSpec((None, 1, tn), lambda l, j: (l, 0, j)),
        ],
        out_specs=pl.BlockSpec((None, MOD_ROWS, tn), lambda l, j: (l, 0, j)),
        out_shape=jax.ShapeDtypeStruct((depth, MOD_ROWS, n), F32),
        compiler_params=_cparams(("parallel", "parallel")),
        name="adaln",
    )(c_all, w_ada, b_ada.reshape(depth, 1, n))


class _Layout:
    def __init__(self, batch, ctx_len, seq):
        self.batch, self.ctx_len, self.seq = batch, ctx_len, seq
        self.n_lat = batch * seq
        self.n_tok = self.n_lat + batch * ctx_len

    def mod_spec(self, layer, tm, which):
        n_lat_tiles = self.n_lat // tm
        rows_per = self.seq // tm
        batch = self.batch

        def imap(i, *_):
            row = jnp.where(i < n_lat_tiles, i // rows_per, batch)
            return (layer * MOD_ROWS + row, 0, which)

        return pl.BlockSpec((None, 1, D_MODEL), imap)


def _inproj_kernel(x_ref, sh_ref, sc_ref, wq_ref, wr_ref, wc_ref, cos_ref, sin_ref,
                   oq_ref, or_ref, oc_ref):
    h = (x_ref[...] * (1.0 + sc_ref[...]) + sh_ref[...]).astype(BF16)
    qkv = _dot(h, wq_ref[...])
    cos = cos_ref[...]
    sin = sin_ref[...]
    lane = lax.broadcasted_iota(jnp.int32, cos.shape, 1)
    first_half = (lane % AXIS_DIM) < (AXIS_DIM // 2)
    for s in range(2 * ATT_WIDTH // LANES):
        seg = qkv[:, s * LANES:(s + 1) * LANES]
        swapped = jnp.where(first_half, pltpu.roll(seg, LANES - AXIS_DIM // 2, 1),
                            pltpu.roll(seg, AXIS_DIM // 2, 1))
        rot = seg * cos + swapped * sin
        if s < ATT_WIDTH // LANES:
            rot = rot * (ATT_QK ** -0.5)
        oq_ref[:, s * LANES:(s + 1) * LANES] = rot.astype(BF16)
    oq_ref[:, 2 * ATT_WIDTH:] = qkv[:, 2 * ATT_WIDTH:].astype(BF16)
    or_ref[...] = _dot(h, wr_ref[...])
    oc_ref[...] = _dot(h, wc_ref[...])


def _inproj(lay, layer, x, mods, wq, wr, wc, cos, sin, tm):
    n_lat_tiles = lay.n_lat // tm
    ctx_tiles_per = lay.ctx_len // tm
    lat_tiles_per = lay.seq // tm

    def rope_map(i):
        return (jnp.where(i < n_lat_tiles, i % lat_tiles_per,
                          lat_tiles_per + (i - n_lat_tiles) % ctx_tiles_per), 0)

    row = lambda i: (i, 0)
    wmap = lambda i: (layer, 0, 0)
    return pl.pallas_call(
        _inproj_kernel,
        grid=(lay.n_tok // tm,),
        in_specs=[
            pl.BlockSpec((tm, D_MODEL), row),
            lay.mod_spec(layer, tm, 0),
            lay.mod_spec(layer, tm, 1),
            pl.BlockSpec((None, D_MODEL, QKV_W), wmap),
            pl.BlockSpec((None, D_MODEL, RW_W), wmap),
            pl.BlockSpec((None, D_MODEL, CV_W), wmap),
            pl.BlockSpec((tm, LANES), rope_map),
            pl.BlockSpec((tm, LANES), rope_map),
        ],
        out_specs=[
            pl.BlockSpec((tm, QKV_W), row),
            pl.BlockSpec((tm, RW_W), row),
            pl.BlockSpec((tm, CV_W), row),
        ],
        out_shape=[
            jax.ShapeDtypeStruct((lay.n_tok, QKV_W), BF16),
            jax.ShapeDtypeStruct((lay.n_tok, RW_W), F32),
            jax.ShapeDtypeStruct((lay.n_tok, CV_W), F32),
        ],
        compiler_params=_cparams(("parallel",)),
        name="inproj",
    )(x, mods, mods, wq, wr, wc, cos, sin)


def _attn_kernel(q_ref, kc_ref, vc_ref, kl_ref, vl_ref, lam_ref, g_ref, o_ref, *, lam_init, n_lat_q, tq):
    q = q_ref[...]
    lane = lax.broadcasted_iota(jnp.int32, q.shape, 1)
    zero = jnp.zeros_like(q)
    qq = jnp.concatenate([jnp.where(lane < ATT_QK, q, zero), jnp.where(lane >= ATT_QK, q, zero)], axis=0)
    lam4 = lam_ref[...]
    lam = (jnp.exp(jnp.sum(lam4[0:1] * lam4[1:2], axis=-1, keepdims=True))
           - jnp.exp(jnp.sum(lam4[2:3] * lam4[3:4], axis=-1, keepdims=True)) + lam_init)

    def scores(k_ref):
        return lax.dot_general(qq, k_ref[...], (((1,), (1,)), ((), ())), preferred_element_type=F32)

    def finish(o, l):
        o = o / l
        dlt = o[:tq] - lam * o[tq:]
        ms = jnp.mean(dlt * dlt, axis=-1, keepdims=True)
        o_ref[...] = (dlt * lax.rsqrt(ms + SUBLN_EPS) * g_ref[...] * (1.0 - lam_init)).astype(o_ref.dtype)

    is_ctx = pl.program_id(2) >= n_lat_q

    @pl.when(is_ctx)
    def _():
        s = scores(kc_ref)
        p = jnp.exp(s - jnp.max(s, axis=-1, keepdims=True))
        finish(_dot(p.astype(BF16), vc_ref[...]), jnp.sum(p, axis=-1, keepdims=True))

    @pl.when(jnp.logical_not(is_ctx))
    def _():
        sc = scores(kc_ref)
        sl = scores(kl_ref)
        m = jnp.maximum(jnp.max(sc, axis=-1, keepdims=True), jnp.max(sl, axis=-1, keepdims=True))
        pc = jnp.exp(sc - m)
        pll = jnp.exp(sl - m)
        l = jnp.sum(pc, axis=-1, keepdims=True) + jnp.sum(pll, axis=-1, keepdims=True)
        finish(_dot(pc.astype(BF16), vc_ref[...]) + _dot(pll.astype(BF16), vl_ref[...]), l)


def _attention(lay, layer, qkv, lam4, subln_g, lam_init, tq):
    n_ctx_q = lay.ctx_len // tq
    n_lat_q = lay.seq // tq
    n_lat_tiles = lay.n_lat // tq
    ctx_blk0 = lay.n_lat // lay.ctx_len
    hq = ATT_WIDTH // LANES

    def qmap(b, h, qi):
        return (jnp.where(qi < n_lat_q, b * n_lat_q + qi, n_lat_tiles + b * n_ctx_q + (qi - n_lat_q)), h)

    kern = functools.partial(_attn_kernel, lam_init=lam_init, n_lat_q=n_lat_q, tq=tq)
    return pl.pallas_call(
        kern,
        grid=(lay.batch, ATT_HEADS, n_lat_q + n_ctx_q),
        in_specs=[
            pl.BlockSpec((tq, LANES), qmap),
            pl.BlockSpec((lay.ctx_len, LANES), lambda b, h, qi: (ctx_blk0 + b, hq + h)),
            pl.BlockSpec((lay.ctx_len, LANES), lambda b, h, qi: (ctx_blk0 + b, 2 * hq + h)),
            pl.BlockSpec((lay.seq, LANES), lambda b, h, qi: (b, hq + h)),
            pl.BlockSpec((lay.seq, LANES), lambda b, h, qi: (b, 2 * hq + h)),
            pl.BlockSpec((None, 4, ATT_QK), lambda b, h, qi: (layer, 0, 0)),
            pl.BlockSpec((None, 1, ATT_V), lambda b, h, qi: (layer, 0, 0)),
        ],
        out_specs=pl.BlockSpec((tq, LANES), qmap),
        out_shape=jax.ShapeDtypeStruct((lay.n_tok, ATT_WIDTH), BF16),
        compiler_params=_cparams(("parallel", "parallel", "arbitrary")),
        name="diff_attn",
    )(qkv, qkv, qkv, qkv, qkv, lam4, subln_g)


def _prep_kernel(rw_ref, rwp_ref, rwn_ref, cv_ref, cvp_ref, cvn_ref,
                 rkvw_ref, w0_ref, wd_ref, a0_ref, wa_ref, wg_ref, kk_w_ref, ka_ref, rk_ref, cw_ref,
                 r_ref, v_ref, kk_ref, lw_ref, bb_ref, kd_ref, bv_ref, g_ref, co_ref,
                 *, tm, n_lat, ctx_len, seq):
    i = pl.program_id(0)
    start = i * tm
    in_lat = start < n_lat
    seg_pos = jnp.where(in_lat, start % seq, (start - n_lat) % ctx_len)
    seg_len = jnp.where(in_lat, seq, ctx_len)
    has_prev = seg_pos != 0
    has_next = seg_pos + tm != seg_len

    def shifted(u, prev_row, next_row):
        rows = lax.broadcasted_iota(jnp.int32, u.shape, 0)
        prev_row = jnp.where(has_prev, prev_row, jnp.zeros_like(prev_row))
        next_row = jnp.where(has_next, next_row, jnp.zeros_like(next_row))
        up = jnp.where(rows == 0, prev_row, pltpu.roll(u, 1, 0))
        un = jnp.where(rows == tm - 1, next_row, pltpu.roll(u, tm - 1, 0))
        return up, un

    def conv3(u, prev_row, next_row, w):
        up, un = shifted(u, prev_row, next_row)
        return up * w[0:1] + u * w[1:2] + un * w[2:3]

    rw = rw_ref[...]
    nrkv = 3 * RWKV_WIDTH
    rkv = conv3(rw[:, :nrkv], rwp_ref[7:8, :nrkv], rwn_ref[0:1, :nrkv], rkvw_ref[...])
    r = rkv[:, :RWKV_WIDTH]
    k = rkv[:, RWKV_WIDTH:2 * RWKV_WIDTH]
    v = rkv[:, 2 * RWKV_WIDTH:]

    lora = rw[:, nrkv:nrkv + LANES]
    gate = rw[:, nrkv + LANES:]
    wl = _dot_hi(jnp.tanh(lora), wd_ref[...]) + w0_ref[...]
    lw = (-math.exp(-0.5)) * jax.nn.sigmoid(wl)
    a = jax.nn.sigmoid(_dot_hi(lora, wa_ref[...]) + a0_ref[...])
    g = _dot_hi(jax.nn.sigmoid(gate), wg_ref[...])

    ones = _head_ones(RWKV_WIDTH, RWKV_N)
    kraw = k * kk_w_ref[...]
    ss = _dot_hi(kraw * kraw, ones)
    kk = kraw * lax.rsqrt(jnp.maximum(ss, 1e-24))
    k2 = jnp.concatenate([k, k], axis=1)
    ka2 = jnp.concatenate([ka_ref[...], ka_ref[...]], axis=1)
    kd = k2 * (1.0 + (a - 1.0) * ka2)
    bb = jnp.concatenate([kk, kk], axis=1) * a
    bonus = _dot_hi(r * (kd[:, :RWKV_WIDTH] + kd[:, RWKV_WIDTH:]) * rk_ref[...], ones)

    r_ref[...] = r
    v_ref[...] = v
    kk_ref[...] = kk
    lw_ref[...] = lw
    bb_ref[...] = bb
    kd_ref[...] = kd
    bv_ref[...] = bonus * v
    g_ref[...] = g

    def gated(ref):
        return ref[:, 2 * CONV_WIDTH:] * ref[:, :CONV_WIDTH]

    cv_u = gated(cv_ref)
    conv = conv3(cv_u, gated(cvp_ref)[7:8], gated(cvn_ref)[0:1], cw_ref[...])
    co_ref[...] = (cv_ref[:, CONV_WIDTH:2 * CONV_WIDTH] * conv).astype(co_ref.dtype)


def _prep(lay, layer, rw, cv, rkv_conv, w0, wd, a0, wa, wg, k_k, k_a, r_k, conv_w, tm):
    n8 = lay.n_tok // 8
    t8 = tm // 8
    row = lambda i: (i, 0)
    prev = lambda i: (jnp.maximum(i * t8 - 1, 0), 0)
    nxt = lambda i: (jnp.minimum((i + 1) * t8, n8 - 1), 0)
    lmap = lambda i: (layer, 0, 0)
    w2 = 2 * RWKV_WIDTH
    kern = functools.partial(_prep_kernel, tm=tm, n_lat=lay.n_lat, ctx_len=lay.ctx_len, seq=lay.seq)
    f32 = lambda w: jax.ShapeDtypeStruct((lay.n_tok, w), F32)
    return pl.pallas_call(
        kern,
        grid=(lay.n_tok // tm,),
        in_specs=[
            pl.BlockSpec((tm, RW_W), row), pl.BlockSpec((8, RW_W), prev), pl.BlockSpec((8, RW_W), nxt),
            pl.BlockSpec((tm, CV_W), row), pl.BlockSpec((8, CV_W), prev), pl.BlockSpec((8, CV_W), nxt),
            pl.BlockSpec((None, 3, 3 * RWKV_WIDTH), lmap),
            pl.BlockSpec((None, 1, w2), lmap),
            pl.BlockSpec((None, LANES, w2), lmap),
            pl.BlockSpec((None, 1, w2), lmap),
            pl.BlockSpec((None, LANES, w2), lmap),
            pl.BlockSpec((None, LANES, RWKV_WIDTH), lmap),
            pl.BlockSpec((None, 1, RWKV_WIDTH), lmap),
            pl.BlockSpec((None, 1, RWKV_WIDTH), lmap),
            pl.BlockSpec((None, 1, RWKV_WIDTH), lmap),
            pl.BlockSpec((None, 3, CONV_WIDTH), lmap),
        ],
        out_specs=[
            pl.BlockSpec((tm, RWKV_WIDTH), row), pl.BlockSpec((tm, RWKV_WIDTH), row),
            pl.BlockSpec((tm, RWKV_WIDTH), row),
            pl.BlockSpec((tm, w2), row), pl.BlockSpec((tm, w2), row), pl.BlockSpec((tm, w2), row),
            pl.BlockSpec((tm, RWKV_WIDTH), row), pl.BlockSpec((tm, RWKV_WIDTH), row),
            pl.BlockSpec((tm, CONV_WIDTH), row),
        ],
        out_shape=[f32(RWKV_WIDTH), f32(RWKV_WIDTH), f32(RWKV_WIDTH), f32(w2), f32(w2), f32(w2),
                   f32(RWKV_WIDTH), f32(RWKV_WIDTH), jax.ShapeDtypeStruct((lay.n_tok, CONV_WIDTH), BF16)],
        compiler_params=_cparams(("parallel",)),
        name="rwkv_prep",
    )(rw, rw, rw, cv, cv, cv, rkv_conv, w0, wd, a0, wa, wg, k_k, k_a, r_k, conv_w)


def _split(a):
    hi = a.astype(BF16)
    return hi, (a - hi.astype(F32)).astype(BF16)


_NN = (((1,), (0,)), ((), ()))
_NT = (((1,), (1,)), ((), ()))
_TN = (((0,), (0,)), ((), ()))


def _dot3(a, b, dims=_NN):
    (ah, al), (bh, bl) = a, b
    dg = lambda p, q: lax.dot_general(p, q, dims, preferred_element_type=F32)
    return dg(ah, bh) + dg(ah, bl) + dg(al, bh)


def _chunk_terms_kernel(r_ref, v_ref, kk_ref, lw_ref, bb_ref, kd_ref, g_ref, h_ref, rq_ref, y0_ref):
    c = SCAN_CHUNK
    n = RWKV_N
    row = lax.broadcasted_iota(jnp.int32, (c, c), 0)
    col = lax.broadcasted_iota(jnp.int32, (c, c), 1)
    row2 = lax.broadcasted_iota(jnp.int32, (c, 2 * c), 0)
    col2 = lax.broadcasted_iota(jnp.int32, (c, 2 * c), 1) % c
    eye = row == col
    r_all = r_ref[...]
    v_all = v_ref[...]
    kk = kk_ref[...]

    chains = []
    for d in range(2):
        sgn = 1 if d == 0 else -1
        dsl = slice(d * RWKV_WIDTH, (d + 1) * RWKV_WIDTH)
        incl = (col - row) * sgn <= 0
        lw = lw_ref[:, dsl]
        bb = bb_ref[:, dsl]
        kd = kd_ref[:, dsl]
        lp = _dot_hi(incl.astype(F32), lw)
        lt = jnp.sum(lw, axis=0, keepdims=True)
        p_inv = jnp.exp(-lp)
        p_end = jnp.exp(lt - lp)
        a_t = -kk * jnp.exp(lp - lw)
        b_t = bb * p_inv
        k_t = kd * p_inv
        r_t = r_all * jnp.exp(lp)
        k_e = kd * p_end
        b_e = bb * p_end
        p_tot = jnp.exp(lt)
        for h in range(RWKV_HEADS):
            sl = slice(h * n, (h + 1) * n)
            chains.append(dict(
                strict=(col - row) * sgn < 0, incl2=(col2 - row2) * sgn <= 0,
                ah=a_t[:, sl], rh=r_t[:, sl], vh=v_all[:, sl], bt=b_t[:, sl], kt=k_t[:, sl],
                be=b_e[:, sl], ke=k_e[:, sl], ptot=p_tot[:, sl]))

    for ch in chains:
        ch["sc"] = _dot3(_split(jnp.concatenate([ch["ah"], ch["rh"]], axis=0)),
                         _split(jnp.concatenate([ch["bt"], ch["kt"]], axis=0)), _NT)
    for ch in chains:
        sc = ch["sc"]
        ch["nmat"] = _split(jnp.where(ch["strict"], sc[:c, :c], 0.0))
        ch["mr"] = _split(jnp.where(ch["incl2"], sc[c:, :], 0.0))
        ch["mkv"] = _dot3(_split(jnp.where(ch["strict"], sc[:c, c:], 0.0)), _split(ch["vh"]))
    for ch in chains:
        ch["z"] = jnp.concatenate([ch["ah"], ch["mkv"]], axis=1)
    for p in range(6):
        for ch in chains:
            ch["z"] = ch["z"] + _dot3(ch["nmat"], _split(ch["z"]))
        if p < 5:
            for ch in chains:
                ch["nmat"] = _split(_dot3(ch["nmat"], ch["nmat"]))
    for ch in chains:
        z = ch["z"]
        ch["ws"] = _split(z[:, :n])
        ch["uv"] = _split(jnp.concatenate([z[:, n:], ch["vh"]], axis=0))
    for ch in chains:
        mr = ch["mr"]
        ch["rq"] = ch["rh"] + _dot3((mr[0][:, :c], mr[1][:, :c]), ch["ws"])
        ch["y0"] = _dot3(mr, ch["uv"])
        ch["g"] = jnp.where(eye, ch["ptot"], 0.0) + _dot3(ch["ws"], _split(ch["be"]), _TN)
        ch["h"] = _dot3(ch["uv"], _split(jnp.concatenate([ch["be"], ch["ke"]], axis=0)), _TN)
    for d in range(2):
        part = chains[d * RWKV_HEADS:(d + 1) * RWKV_HEADS]
        g_ref[d] = jnp.concatenate([ch["g"] for ch in part], axis=1)
        h_ref[d] = jnp.concatenate([ch["h"] for ch in part], axis=1)
        rq_ref[d] = jnp.concatenate([ch["rq"] for ch in part], axis=1)
        y0_ref[d] = jnp.concatenate([ch["y0"] for ch in part], axis=1)


def _chunk_terms(lay, r, v, kk, lw, bb, kd):
    c = SCAN_CHUNK
    row = lambda i: (i, 0)
    out = pl.BlockSpec((2, c, RWKV_WIDTH), lambda i: (0, i, 0))
    shp = jax.ShapeDtypeStruct((2, lay.n_tok, RWKV_WIDTH), F32)
    return pl.pallas_call(
        _chunk_terms_kernel,
        grid=(lay.n_tok // c,),
        in_specs=[pl.BlockSpec((c, RWKV_WIDTH), row)] * 3 + [pl.BlockSpec((c, 2 * RWKV_WIDTH), row)] * 3,
        out_specs=[out] * 4,
        out_shape=[shp] * 4,
        compiler_params=_cparams(("parallel",)),
        name="rwkv_chunk_terms",
    )(r, v, kk, lw, bb, kd)


def _scan_kernel(gf_ref, hf_ref, rqf_ref, y0f_ref, gb_ref, hb_ref, rqb_ref, y0b_ref, yf_ref, yb_ref, st_ref):
    @pl.when(pl.program_id(1) == 0)
    def _():
        st_ref[...] = jnp.zeros_like(st_ref)

    dirs = ((gf_ref, hf_ref, rqf_ref, y0f_ref, yf_ref), (gb_ref, hb_ref, rqb_ref, y0b_ref, yb_ref))
    for d, (g_ref, h_ref, rq_ref, y0_ref, y_ref) in enumerate(dirs):
        g, hh, rq, y0 = g_ref[...], h_ref[...], rq_ref[...], y0_ref[...]
        ys = []
        for h in range(RWKV_HEADS):
            sl = slice(h * RWKV_N, (h + 1) * RWKV_N)
            s0 = st_ref[d, h]
            ys.append(_dot_nt_hi(rq[:, sl], s0) + y0[:, sl])
            st_ref[d, h] = _dot_hi(s0, g[:, sl]) + hh[:, sl]
        y_ref[...] = jnp.concatenate(ys, axis=1)


def _scan(lay, g, h, rq, y0):
    c = SCAN_CHUNK
    nc_ctx = lay.ctx_len // c
    nc_lat = lay.seq // c
    lat_blocks = lay.n_lat // c

    def blk(b, d, ci):
        if d == 0:
            return jnp.where(ci < nc_ctx, lat_blocks + b * nc_ctx + ci, b * nc_lat + (ci - nc_ctx))
        return jnp.where(ci < nc_ctx, lat_blocks + b * nc_ctx + (nc_ctx - 1 - ci),
                         b * nc_lat + (nc_lat - 1 - (ci - nc_ctx)))

    def term(d):
        return pl.BlockSpec((None, c, RWKV_WIDTH), lambda b, ci: (d, blk(b, d, ci), 0))

    def yspec(d):
        return pl.BlockSpec((c, RWKV_WIDTH), lambda b, ci: (blk(b, d, ci), 0))

    shp = jax.ShapeDtypeStruct((lay.n_tok, RWKV_WIDTH), F32)
    return pl.pallas_call(
        _scan_kernel,
        grid=(lay.batch, nc_ctx + nc_lat),
        in_specs=[term(0)] * 4 + [term(1)] * 4,
        out_specs=[yspec(0), yspec(1)],
        out_shape=[shp, shp],
        scratch_shapes=[pltpu.VMEM((2, RWKV_HEADS, RWKV_N, RWKV_N), F32)],
        compiler_params=_cparams(("parallel", "arbitrary")),
        name="rwkv_scan",
    )(g, h, rq, y0, g, h, rq, y0)


def _outproj_kernel(att_ref, yf_ref, yb_ref, bv_ref, g_ref, co_ref, x_ref, gate_ref, w_ref,
                    gng_ref, gnb_ref, lng_ref, lnb_ref, o_ref):
    ones = _head_ones(RWKV_WIDTH, RWKV_N)
    y = yf_ref[...] + yb_ref[...]
    mu = _dot_hi(y, ones) * (1.0 / RWKV_N)
    yc = y - mu
    var = _dot_hi(yc * yc, ones) * (1.0 / RWKV_N)
    yn = yc * lax.rsqrt(var + GN_EPS) * gng_ref[...] + gnb_ref[...]
    rwkv = ((yn + bv_ref[...]) * g_ref[...]).astype(BF16)
    mix = (_dot(att_ref[...], w_ref[:ATT_WIDTH, :])
           + _dot(rwkv, w_ref[ATT_WIDTH:ATT_WIDTH + RWKV_WIDTH, :])
           + _dot(co_ref[...], w_ref[ATT_WIDTH + RWKV_WIDTH:, :]))
    z = DN_ALPHA * x_ref[...] + gate_ref[...] * mix
    o_ref[...] = _layer_norm(z, lng_ref[...], lnb_ref[...])


def _outproj(lay, layer, att, yf, yb, bv, g, co, x, mods, w_out, gn_g, gn_b, ln_g, ln_b, tm):
    row = lambda i: (i, 0)
    lmap = lambda i: (layer, 0, 0)
    return pl.pallas_call(
        _outproj_kernel,
        grid=(lay.n_tok // tm,),
        in_specs=[
            pl.BlockSpec((tm, ATT_WIDTH), row),
            pl.BlockSpec((tm, RWKV_WIDTH), row),
            pl.BlockSpec((tm, RWKV_WIDTH), row),
            pl.BlockSpec((tm, RWKV_WIDTH), row),
            pl.BlockSpec((tm, RWKV_WIDTH), row),
            pl.BlockSpec((tm, CONV_WIDTH), row),
            pl.BlockSpec((tm, D_MODEL), row),
            lay.mod_spec(layer, tm, 2),
            pl.BlockSpec((None, D_MODEL, D_MODEL), lmap),
            pl.BlockSpec((None, 1, RWKV_WIDTH), lmap),
            pl.BlockSpec((None, 1, RWKV_WIDTH), lmap),
            pl.BlockSpec((None, 1, D_MODEL), lmap),
            pl.BlockSpec((None, 1, D_MODEL), lmap),
        ],
        out_specs=pl.BlockSpec((tm, D_MODEL), row),
        out_shape=jax.ShapeDtypeStruct((lay.n_tok, D_MODEL), F32),
        compiler_params=_cparams(("parallel",)),
        name="outproj_ln1",
    )(att, yf, yb, bv, g, co, x, mods, w_out, gn_g, gn_b, ln_g, ln_b)


def _ffn_kernel(x_ref, sh_ref, sc_ref, gate_ref, w1_ref, w3_ref, w2_ref, lng_ref, lnb_ref, o_ref, *, n_split):
    x = x_ref[...]
    h = (x * (1.0 + sc_ref[...]) + sh_ref[...]).astype(BF16)
    step = D_FF // n_split
    f = jnp.zeros(x.shape, F32)
    for s in range(n_split):
        cols = slice(s * step, (s + 1) * step)
        a = _dot(h, w1_ref[:, cols])
        act = (a * jax.nn.sigmoid(a) * _dot(h, w3_ref[:, cols])).astype(BF16)
        f = f + _dot(act, w2_ref[cols, :])
    z = DN_ALPHA * x + gate_ref[...] * f
    o_ref[...] = _layer_norm(z, lng_ref[...], lnb_ref[...])


def _ffn(lay, layer, j, x, mods, w1, w3, w2, ln_g, ln_b, tm):
    row = lambda i: (i, 0)
    once = pl.Buffered(1)
    return pl.pallas_call(
        functools.partial(_ffn_kernel, n_split=2),
        grid=(lay.n_tok // tm,),
        in_specs=[
            pl.BlockSpec((tm, D_MODEL), row),
            lay.mod_spec(layer, tm, 3), lay.mod_spec(layer, tm, 4), lay.mod_spec(layer, tm, 5),
            pl.BlockSpec((None, D_MODEL, D_FF), lambda i: (j, 0, 0), pipeline_mode=once),
            pl.BlockSpec((None, D_MODEL, D_FF), lambda i: (j, 0, 0), pipeline_mode=once),
            pl.BlockSpec((None, D_FF, D_MODEL), lambda i: (j, 0, 0), pipeline_mode=once),
            pl.BlockSpec((None, 1, D_MODEL), lambda i: (layer, 0, 0)),
            pl.BlockSpec((None, 1, D_MODEL), lambda i: (layer, 0, 0)),
        ],
        out_specs=pl.BlockSpec((tm, D_MODEL), row),
        out_shape=jax.ShapeDtypeStruct((lay.n_tok, D_MODEL), F32),
        compiler_params=_cparams(("parallel",)),
        name="ffn_ln2",
    )(x, mods, mods, mods, w1, w3, w2, ln_g, ln_b)


def _moe_kernel(x_ref, sh_ref, sc_ref, gate_ref, rw_ref, rb_ref, w1_ref, w3_ref, w2_ref, lng_ref, lnb_ref,
                o_ref, h_ref, gates_ref, acc_ref):
    e = pl.program_id(1)
    lane = lax.broadcasted_iota(jnp.int32, gates_ref.shape, 1)

    @pl.when(e == 0)
    def _():
        h = x_ref[...] * (1.0 + sc_ref[...]) + sh_ref[...]
        h_ref[...] = h.astype(BF16)
        logits = _dot_hi(h, rw_ref[...]) + rb_ref[...]
        m1 = jnp.max(logits, axis=-1, keepdims=True)
        i1 = jnp.min(jnp.where(logits == m1, lane, LANES), axis=-1, keepdims=True)
        rest = jnp.where(lane == i1, -jnp.inf, logits)
        m2 = jnp.max(rest, axis=-1, keepdims=True)
        i2 = jnp.min(jnp.where(rest == m2, lane, LANES), axis=-1, keepdims=True)
        e2 = jnp.exp(m2 - m1)
        den = 1.0 + e2
        gates_ref[...] = jnp.where(lane == i1, 1.0 / den, 0.0) + jnp.where(lane == i2, e2 / den, 0.0)
        acc_ref[...] = jnp.zeros_like(acc_ref)

    h = h_ref[...]
    a = _dot(h, w1_ref[...])
    act = (a * jax.nn.sigmoid(a) * _dot(h, w3_ref[...])).astype(BF16)
    ge = jnp.sum(jnp.where(lane == e, gates_ref[...], 0.0), axis=-1, keepdims=True)
    acc_ref[...] += ge * _dot(act, w2_ref[...])

    @pl.when(e == N_EXPERTS - 1)
    def _():
        z = DN_ALPHA * x_ref[...] + gate_ref[...] * acc_ref[...]
        o_ref[...] = _layer_norm(z, lng_ref[...], lnb_ref[...])


def _moe(lay, layer, j, x, mods, router_w, router_b, w1, w3, w2, ln_g, ln_b, tm):
    row = lambda i, e: (i, 0)
    return pl.pallas_call(
        _moe_kernel,
        grid=(lay.n_tok // tm, N_EXPERTS),
        in_specs=[
            pl.BlockSpec((tm, D_MODEL), row),
            lay.mod_spec(layer, tm, 3), lay.mod_spec(layer, tm, 4), lay.mod_spec(layer, tm, 5),
            pl.BlockSpec((None, D_MODEL, LANES), lambda i, e: (j, 0, 0)),
            pl.BlockSpec((None, 1, LANES), lambda i, e: (j, 0, 0)),
            pl.BlockSpec((None, None, D_MODEL, D_FF_EXPERT), lambda i, e: (j, e, 0, 0)),
            pl.BlockSpec((None, None, D_MODEL, D_FF_EXPERT), lambda i, e: (j, e, 0, 0)),
            pl.BlockSpec((None, None, D_FF_EXPERT, D_MODEL), lambda i, e: (j, e, 0, 0)),
            pl.BlockSpec((None, 1, D_MODEL), lambda i, e: (layer, 0, 0)),
            pl.BlockSpec((None, 1, D_MODEL), lambda i, e: (layer, 0, 0)),
        ],
        out_specs=pl.BlockSpec((tm, D_MODEL), row),
        out_shape=jax.ShapeDtypeStruct((lay.n_tok, D_MODEL), F32),
        scratch_shapes=[pltpu.VMEM((tm, D_MODEL), BF16), pltpu.VMEM((tm, LANES), F32),
                        pltpu.VMEM((tm, D_MODEL), F32)],
        compiler_params=_cparams(("parallel", "arbitrary")),
        name="moe_ln2",
    )(x, mods, mods, mods, router_w, router_b, w1, w3, w2, ln_g, ln_b)


def _rope_tables(ctx_len, seq):
    t = np.arange(seq)
    inv = ROPE_THETA ** (-np.arange(0, AXIS_DIM, 2, dtype=np.float64) / AXIS_DIM)
    ang_r = (t // GRID_W)[:, None].astype(np.float64) * inv
    ang_c = (t % GRID_W)[:, None].astype(np.float64) * inv
    cos = np.concatenate([np.cos(ang_r), np.cos(ang_r), np.cos(ang_c), np.cos(ang_c)], axis=1)
    sin = np.concatenate([-np.sin(ang_r), np.sin(ang_r), -np.sin(ang_c), np.sin(ang_c)], axis=1)
    cos = np.concatenate([cos, np.ones((ctx_len, ATT_QK))], axis=0)
    sin = np.concatenate([sin, np.zeros((ctx_len, ATT_QK))], axis=0)
    reps = LANES // ATT_QK
    return (jnp.asarray(np.tile(cos, (1, reps)), F32), jnp.asarray(np.tile(sin, (1, reps)), F32))


def kernel(x, c, ctx, c_ctx, w_ada, b_ada, w_in, w_out, ln1_g, ln1_b, ln2_g, ln2_b,
           lam_q1, lam_k1, lam_q2, lam_k2, subln_g, rkv_conv, decay_w0, decay_up, iclr_a0, iclr_up,
           gate_up, k_k, k_a, r_k, gn_g, gn_b, conv_w, ffn_w1, ffn_w3, ffn_w2,
           router_w, router_b, moe_w1, moe_w3, moe_w2):
    batch, seq, d = x.shape
    ctx_len = ctx.shape[1]
    depth = w_in.shape[0]
    assert d == D_MODEL and depth == DEPTH and batch < MOD_ROWS
    tm = 256
    tq = 128
    tm_moe = 512
    assert ctx_len % tm == 0 and seq % tm == 0 and (batch * ctx_len) % tm_moe == 0 and seq % tm_moe == 0
    assert seq % ctx_len == 0 and ctx_len % SCAN_CHUNK == 0 and seq % GRID_W == 0
    lay = _Layout(batch, ctx_len, seq)

    tokens = jnp.concatenate([x.reshape(batch * seq, d), ctx.reshape(batch * ctx_len, d)], axis=0)

    c_all = jnp.concatenate([c, c_ctx[None, :], jnp.zeros((MOD_ROWS - batch - 1, d), F32)], axis=0)
    mods = _adaln(c_all, w_ada, b_ada).reshape(depth * MOD_ROWS, 1, 6 * d)

    cos, sin = _rope_tables(ctx_len, seq)

    o = np.cumsum([0, 512, 512, 512, 256, 256, 256, 64, 64, 64, 256, 256, 256])
    wq = w_in[:, :, :o[3]].astype(BF16)
    wr = jnp.concatenate([w_in[:, :, o[3]:o[9]], jnp.zeros((depth, d, RW_W - (o[9] - o[3])), F32)],
                         axis=-1).astype(BF16)
    wc = w_in[:, :, o[9]:].astype(BF16)
    w_out_b = w_out.astype(BF16)

    z = jnp.zeros((depth, DECAY_LORA, RWKV_WIDTH), F32)
    wd = jnp.concatenate([
        jnp.concatenate([decay_up[:, 0], z], axis=-1),
        jnp.concatenate([z, decay_up[:, 1]], axis=-1),
        jnp.zeros((depth, 2 * ICLR_LORA, 2 * RWKV_WIDTH), F32)], axis=1)
    wa = jnp.concatenate([
        jnp.zeros((depth, 2 * DECAY_LORA, 2 * RWKV_WIDTH), F32),
        jnp.concatenate([iclr_up[:, 0], z], axis=-1),
        jnp.concatenate([z, iclr_up[:, 1]], axis=-1)], axis=1)
    wg = jnp.concatenate([gate_up, jnp.zeros((depth, LANES - GATE_LORA, RWKV_WIDTH), F32)], axis=1)
    w0 = decay_w0.reshape(depth, 1, 2 * RWKV_WIDTH)
    a0 = iclr_a0.reshape(depth, 1, 2 * RWKV_WIDTH)
    vec = lambda w: w.reshape(depth, 1, -1)

    lam4 = jnp.stack([lam_q1, lam_k1, lam_q2, lam_k2], axis=1)

    ffn_w1_b, ffn_w3_b, ffn_w2_b = ffn_w1.astype(BF16), ffn_w3.astype(BF16), ffn_w2.astype(BF16)
    moe_w1_b, moe_w3_b, moe_w2_b = moe_w1.astype(BF16), moe_w3.astype(BF16), moe_w2.astype(BF16)
    n_moe = router_w.shape[0]
    router_w_p = jnp.concatenate([router_w, jnp.zeros((n_moe, d, LANES - N_EXPERTS), F32)], axis=-1)
    router_b_p = jnp.concatenate([router_b, jnp.full((n_moe, LANES - N_EXPERTS), -1e30, F32)],
                                 axis=-1).reshape(n_moe, 1, LANES)

    xs = tokens
    for l in range(depth):
        lam_init = 0.8 - 0.6 * math.exp(-0.3 * l)
        qkv, rw, cv = _inproj(lay, l, xs, mods, wq, wr, wc, cos, sin, tm)
        att = _attention(lay, l, qkv, lam4, vec(subln_g), lam_init, tq)
        r, v, kk, lw, bb, kd, bv, g, co = _prep(lay, l, rw, cv, rkv_conv, w0, wd, a0, wa, wg,
                                                vec(k_k), vec(k_a), vec(r_k), conv_w, tm)
        yf, yb = _scan(lay, *_chunk_terms(lay, r, v, kk, lw, bb, kd))
        xs = _outproj(lay, l, att, yf, yb, bv, g, co, xs, mods, w_out_b, vec(gn_g), vec(gn_b),
                      vec(ln1_g), vec(ln1_b), tm)
        if l % 2 == 0:
            xs = _ffn(lay, l, l // 2, xs, mods, ffn_w1_b, ffn_w3_b, ffn_w2_b, vec(ln2_g), vec(ln2_b), tm)
        else:
            xs = _moe(lay, l, l // 2, xs, mods, router_w_p, router_b_p, moe_w1_b, moe_w3_b, moe_w2_b,
                      vec(ln2_g), vec(ln2_b), tm_moe)
    return xs[:lay.n_lat].reshape(batch, seq, d)
```

```python
import functools
import math

import numpy as np
import jax
import jax.numpy as jnp
from jax import lax
from jax.experimental import pallas as pl
from jax.experimental.pallas import tpu as pltpu

F32 = jnp.float32
BF16 = jnp.bfloat16
HI = lax.Precision.HIGHEST

D_MODEL = 1024
DEPTH = 4
GRID_W = 64
ATT_HEADS = 4
ATT_QK = 64
ATT_V = 128
ATT_WIDTH = 512
AXIS_DIM = 32
ROPE_THETA = 10000.0
SUBLN_EPS = 1e-5
RWKV_HEADS = 4
RWKV_N = 64
RWKV_WIDTH = 256
DECAY_LORA = 32
ICLR_LORA = 32
GATE_LORA = 64
GN_EPS = 64e-5
CONV_WIDTH = 256
D_FF = 2816
N_EXPERTS = 8
D_FF_EXPERT = 1408
DN_ALPHA = (2 * DEPTH) ** 0.25
LN_EPS = 1e-5

LANES = 128
MOD_ROWS = 16
QKV_W = 3 * ATT_WIDTH
RW_W = 1024
CV_W = 3 * CONV_WIDTH
SCAN_CHUNK = 64
VMEM_LIMIT = 56 * 1024 * 1024


def _cparams(sem):
    return pltpu.CompilerParams(dimension_semantics=sem, vmem_limit_bytes=VMEM_LIMIT)


def _dot(a, b):
    return jnp.dot(a, b, preferred_element_type=F32)


def _dot_hi(a, b):
    return jnp.dot(a, b, precision=HI, preferred_element_type=F32)


def _dot_nt_hi(a, b):
    return lax.dot_general(a, b, (((1,), (1,)), ((), ())), precision=HI, preferred_element_type=F32)


def _dot_tn_hi(a, b):
    return lax.dot_general(a, b, (((0,), (0,)), ((), ())), precision=HI, preferred_element_type=F32)


def _split3(a):
    hi = a.astype(BF16)
    rest = a - hi.astype(F32)
    mid = rest.astype(BF16)
    return hi, mid, (rest - mid.astype(F32)).astype(BF16)


def _dot_mask_lhs(mask, a):
    m = jnp.where(mask, 1.0, 0.0).astype(BF16)
    return sum(jnp.dot(m, part, preferred_element_type=F32) for part in _split3(a))


def _dot_mask_rhs(a, mask):
    m = mask.astype(BF16)
    return sum(jnp.dot(part, m, preferred_element_type=F32) for part in _split3(a))


def _layer_norm(z, g, b):
    mu = jnp.mean(z, axis=-1, keepdims=True)
    zc = z - mu
    var = jnp.mean(zc * zc, axis=-1, keepdims=True)
    return zc * lax.rsqrt(var + LN_EPS) * g + b


def _head_ones(width, head):
    r = lax.broadcasted_iota(jnp.int32, (width, width), 0) // head
    c = lax.broadcasted_iota(jnp.int32, (width, width), 1) // head
    return (r == c).astype(F32)


def _ada_kernel(c_ref, w_ref, b_ref, o_ref):
    c = c_ref[...]
    sc = c * jax.nn.sigmoid(c)
    o_ref[...] = _dot(sc.astype(BF16), w_ref[...].astype(BF16)) + b_ref[...]


def _adaln(c_all, w_ada, b_ada):
    depth, d, n = w_ada.shape
    tn = 1536
    return pl.pallas_call(
        _ada_kernel,
        grid=(depth, n // tn),
        in_specs=[
            pl.BlockSpec((MOD_ROWS, d), lambda l, j: (0, 0)),
            pl.BlockSpec((None, d, tn), lambda l, j: (l, 0, j)),
            pl.BlockSpec((None, 1, tn), lambda l, j: (l, 0, j)),
        ],
        out_specs=pl.BlockSpec((None, MOD_ROWS, tn), lambda l, j: (l, 0, j)),
        out_shape=jax.ShapeDtypeStruct((depth, MOD_ROWS, n), F32),
        compiler_params=_cparams(("parallel", "parallel")),
        name="adaln",
    )(c_all, w_ada, b_ada.reshape(depth, 1, n))


class _Layout:
    def __init__(self, batch, ctx_len, seq):
        self.batch, self.ctx_len, self.seq = batch, ctx_len, seq
        self.n_lat = batch * seq
        self.n_tok = self.n_lat + batch * ctx_len

    def mod_spec(self, layer, tm, which):
        n_lat_tiles = self.n_lat // tm
        rows_per = self.seq // tm
        batch = self.batch

        def imap(i, *_):
            row = jnp.where(i < n_lat_tiles, i // rows_per, batch)
            return (layer * MOD_ROWS + row, 0, which)

        return pl.BlockSpec((None, 1, D_MODEL), imap)


def _inproj_kernel(x_ref, sh_ref, sc_ref, wq_ref, wr_ref, wc_ref, cos_ref, sin_ref,
                   oq_ref, or_ref, oc_ref):
    h = (x_ref[...] * (1.0 + sc_ref[...]) + sh_ref[...]).astype(BF16)
    qkv = _dot(h, wq_ref[...])
    cos = cos_ref[...]
    sin = sin_ref[...]
    lane = lax.broadcasted_iota(jnp.int32, cos.shape, 1)
    first_half = (lane % AXIS_DIM) < (AXIS_DIM // 2)
    for s in range(2 * ATT_WIDTH // LANES):
        seg = qkv[:, s * LANES:(s + 1) * LANES]
        swapped = jnp.where(first_half, pltpu.roll(seg, LANES - AXIS_DIM // 2, 1),
                            pltpu.roll(seg, AXIS_DIM // 2, 1))
        rot = seg * cos + swapped * sin
        if s < ATT_WIDTH // LANES:
            rot = rot * (ATT_QK ** -0.5 * math.log2(math.e))
        oq_ref[:, s * LANES:(s + 1) * LANES] = rot.astype(BF16)
    oq_ref[:, 2 * ATT_WIDTH:] = qkv[:, 2 * ATT_WIDTH:].astype(BF16)
    or_ref[...] = _dot(h, wr_ref[...])
    oc_ref[...] = _dot(h, wc_ref[...])


def _inproj(lay, layer, x, mods, wq, wr, wc, cos, sin, tm):
    n_lat_tiles = lay.n_lat // tm
    ctx_tiles_per = lay.ctx_len // tm
    lat_tiles_per = lay.seq // tm

    def rope_map(i):
        return (jnp.where(i < n_lat_tiles, i % lat_tiles_per,
                          lat_tiles_per + (i - n_lat_tiles) % ctx_tiles_per), 0)

    row = lambda i: (i, 0)
    wmap = lambda i: (layer, 0, 0)
    return pl.pallas_call(
        _inproj_kernel,
        grid=(lay.n_tok // tm,),
        in_specs=[
            pl.BlockSpec((tm, D_MODEL), row),
            lay.mod_spec(layer, tm, 0),
            lay.mod_spec(layer, tm, 1),
            pl.BlockSpec((None, D_MODEL, QKV_W), wmap),
            pl.BlockSpec((None, D_MODEL, RW_W), wmap),
            pl.BlockSpec((None, D_MODEL, CV_W), wmap),
            pl.BlockSpec((tm, LANES), rope_map),
            pl.BlockSpec((tm, LANES), rope_map),
        ],
        out_specs=[
            pl.BlockSpec((tm, QKV_W), row),
            pl.BlockSpec((tm, RW_W), row),
            pl.BlockSpec((tm, CV_W), row),
        ],
        out_shape=[
            jax.ShapeDtypeStruct((lay.n_tok, QKV_W), BF16),
            jax.ShapeDtypeStruct((lay.n_tok, RW_W), F32),
            jax.ShapeDtypeStruct((lay.n_tok, CV_W), F32),
        ],
        compiler_params=_cparams(("parallel",)),
        name="inproj",
    )(x, mods, mods, wq, wr, wc, cos, sin)


def _attn_kernel(q_ref, kc_ref, vc_ref, kl_ref, vl_ref, lam_ref, g_ref, o_ref, vxc_ref, vxl_ref,
                 *, lam_init, n_lat_q, tq, tk):
    @pl.when(pl.program_id(2) == 0)
    def _():
        vxc_ref[:, :ATT_V] = vc_ref[...]
        vxc_ref[:, ATT_V:] = jnp.ones((vxc_ref.shape[0], LANES), BF16)
        vxl_ref[:, :ATT_V] = vl_ref[...]
        vxl_ref[:, ATT_V:] = jnp.ones((vxl_ref.shape[0], LANES), BF16)

    q = q_ref[...]
    lane = lax.broadcasted_iota(jnp.int32, q.shape, 1)
    zero = jnp.zeros_like(q)
    qq = jnp.concatenate([jnp.where(lane < ATT_QK, q, zero), jnp.where(lane >= ATT_QK, q, zero)], axis=0)
    lam4 = lam_ref[...]
    lam = (jnp.exp(jnp.sum(lam4[0:1] * lam4[1:2], axis=-1, keepdims=True))
           - jnp.exp(jnp.sum(lam4[2:3] * lam4[3:4], axis=-1, keepdims=True)) + lam_init)

    def scores(blk):
        k_ref, _, start, size = blk
        return lax.dot_general(qq, k_ref[start:start + size, :], _NT, preferred_element_type=F32)

    def attend(blocks):
        s = scores(blocks[0])
        m = acc = None
        for j, (_, vx_ref, start, size) in enumerate(blocks):
            s_next = scores(blocks[j + 1]) if j + 1 < len(blocks) else None
            bm = jnp.max(s, axis=-1, keepdims=True)
            m_new = bm if m is None else jnp.maximum(m, bm)
            p = jnp.exp2(s - m_new).astype(BF16)
            pv = _dot(p, vx_ref[start:start + size, :])
            acc = pv if m is None else acc * jnp.exp2(m - m_new) + pv
            m, s = m_new, s_next
        o = acc[:, :ATT_V] / acc[:, ATT_V:]
        dlt = o[:tq] - lam * o[tq:]
        ms = jnp.mean(dlt * dlt, axis=-1, keepdims=True)
        o_ref[...] = (dlt * lax.rsqrt(ms + SUBLN_EPS) * g_ref[...] * (1.0 - lam_init)).astype(o_ref.dtype)

    ctx_blocks = [(kc_ref, vxc_ref, 0, kc_ref.shape[0])]
    lat_blocks = [(kl_ref, vxl_ref, s0, tk) for s0 in range(0, kl_ref.shape[0], tk)]
    is_ctx = pl.program_id(2) >= n_lat_q

    @pl.when(is_ctx)
    def _():
        attend(ctx_blocks)

    @pl.when(jnp.logical_not(is_ctx))
    def _():
        attend(ctx_blocks + lat_blocks)


def _attention(lay, layer, qkv, lam4, subln_g, lam_init, tq, tk):
    n_ctx_q = lay.ctx_len // tq
    n_lat_q = lay.seq // tq
    n_lat_tiles = lay.n_lat // tq
    ctx_blk0 = lay.n_lat // lay.ctx_len
    hq = ATT_WIDTH // LANES

    def qmap(b, h, qi):
        return (jnp.where(qi < n_lat_q, b * n_lat_q + qi, n_lat_tiles + b * n_ctx_q + (qi - n_lat_q)), h)

    kern = functools.partial(_attn_kernel, lam_init=lam_init, n_lat_q=n_lat_q, tq=tq, tk=tk)
    return pl.pallas_call(
        kern,
        grid=(lay.batch, ATT_HEADS, n_lat_q + n_ctx_q),
        in_specs=[
            pl.BlockSpec((tq, LANES), qmap),
            pl.BlockSpec((lay.ctx_len, LANES), lambda b, h, qi: (ctx_blk0 + b, hq + h)),
            pl.BlockSpec((lay.ctx_len, LANES), lambda b, h, qi: (ctx_blk0 + b, 2 * hq + h)),
            pl.BlockSpec((lay.seq, LANES), lambda b, h, qi: (b, hq + h)),
            pl.BlockSpec((lay.seq, LANES), lambda b, h, qi: (b, 2 * hq + h)),
            pl.BlockSpec((None, 4, ATT_QK), lambda b, h, qi: (layer, 0, 0)),
            pl.BlockSpec((None, 1, ATT_V), lambda b, h, qi: (layer, 0, 0)),
        ],
        out_specs=pl.BlockSpec((tq, LANES), qmap),
        out_shape=jax.ShapeDtypeStruct((lay.n_tok, ATT_WIDTH), BF16),
        scratch_shapes=[pltpu.VMEM((lay.ctx_len, ATT_V + LANES), BF16),
                        pltpu.VMEM((lay.seq, ATT_V + LANES), BF16)],
        compiler_params=_cparams(("parallel", "parallel", "arbitrary")),
        name="diff_attn",
    )(qkv, qkv, qkv, qkv, qkv, lam4, subln_g)


def _prep_kernel(rw_ref, rwp_ref, rwn_ref, cv_ref, cvp_ref, cvn_ref,
                 rkvw_ref, w0_ref, wd_ref, a0_ref, wa_ref, wg_ref, kk_w_ref, ka_ref, rk_ref, cw_ref,
                 r_ref, v_ref, kk_ref, lw_ref, bb_ref, kd_ref, bv_ref, g_ref, co_ref,
                 *, tm, n_lat, ctx_len, seq):
    i = pl.program_id(0)
    start = i * tm
    in_lat = start < n_lat
    seg_pos = jnp.where(in_lat, start % seq, (start - n_lat) % ctx_len)
    seg_len = jnp.where(in_lat, seq, ctx_len)
    has_prev = seg_pos != 0
    has_next = seg_pos + tm != seg_len

    def shifted(u, prev_row, next_row):
        rows = lax.broadcasted_iota(jnp.int32, u.shape, 0)
        prev_row = jnp.where(has_prev, prev_row, jnp.zeros_like(prev_row))
        next_row = jnp.where(has_next, next_row, jnp.zeros_like(next_row))
        up = jnp.where(rows == 0, prev_row, pltpu.roll(u, 1, 0))
        un = jnp.where(rows == tm - 1, next_row, pltpu.roll(u, tm - 1, 0))
        return up, un

    def conv3(u, prev_row, next_row, w):
        up, un = shifted(u, prev_row, next_row)
        return up * w[0:1] + u * w[1:2] + un * w[2:3]

    rw = rw_ref[...]
    nrkv = 3 * RWKV_WIDTH
    rkv = conv3(rw[:, :nrkv], rwp_ref[7:8, :nrkv], rwn_ref[0:1, :nrkv], rkvw_ref[...])
    r = rkv[:, :RWKV_WIDTH]
    k = rkv[:, RWKV_WIDTH:2 * RWKV_WIDTH]
    v = rkv[:, 2 * RWKV_WIDTH:]

    lora = rw[:, nrkv:nrkv + LANES]
    gate = rw[:, nrkv + LANES:]
    wl = _dot_hi(jnp.tanh(lora), wd_ref[...]) + w0_ref[...]
    lw = (-math.exp(-0.5)) * jax.nn.sigmoid(wl)
    a = jax.nn.sigmoid(_dot_hi(lora, wa_ref[...]) + a0_ref[...])
    g = _dot_hi(jax.nn.sigmoid(gate), wg_ref[...])

    ones = _head_ones(RWKV_WIDTH, RWKV_N)
    kraw = k * kk_w_ref[...]
    ss = _dot_mask_rhs(kraw * kraw, ones)
    kk = kraw * lax.rsqrt(jnp.maximum(ss, 1e-24))
    k2 = jnp.concatenate([k, k], axis=1)
    ka2 = jnp.concatenate([ka_ref[...], ka_ref[...]], axis=1)
    kd = k2 * (1.0 + (a - 1.0) * ka2)
    bb = jnp.concatenate([kk, kk], axis=1) * a
    bonus = _dot_mask_rhs(r * (kd[:, :RWKV_WIDTH] + kd[:, RWKV_WIDTH:]) * rk_ref[...], ones)

    r_ref[...] = r
    v_ref[...] = v
    kk_ref[...] = kk
    lw_ref[...] = lw
    bb_ref[...] = bb
    kd_ref[...] = kd
    bv_ref[...] = bonus * v
    g_ref[...] = g

    def gated(ref):
        return ref[:, 2 * CONV_WIDTH:] * ref[:, :CONV_WIDTH]

    cv_u = gated(cv_ref)
    conv = conv3(cv_u, gated(cvp_ref)[7:8], gated(cvn_ref)[0:1], cw_ref[...])
    co_ref[...] = (cv_ref[:, CONV_WIDTH:2 * CONV_WIDTH] * conv).astype(co_ref.dtype)


def _prep(lay, layer, rw, cv, rkv_conv, w0, wd, a0, wa, wg, k_k, k_a, r_k, conv_w, tm):
    n8 = lay.n_tok // 8
    t8 = tm // 8
    row = lambda i: (i, 0)
    prev = lambda i: (jnp.maximum(i * t8 - 1, 0), 0)
    nxt = lambda i: (jnp.minimum((i + 1) * t8, n8 - 1), 0)
    lmap = lambda i: (layer, 0, 0)
    w2 = 2 * RWKV_WIDTH
    kern = functools.partial(_prep_kernel, tm=tm, n_lat=lay.n_lat, ctx_len=lay.ctx_len, seq=lay.seq)
    f32 = lambda w: jax.ShapeDtypeStruct((lay.n_tok, w), F32)
    return pl.pallas_call(
        kern,
        grid=(lay.n_tok // tm,),
        in_specs=[
            pl.BlockSpec((tm, RW_W), row), pl.BlockSpec((8, RW_W), prev), pl.BlockSpec((8, RW_W), nxt),
            pl.BlockSpec((tm, CV_W), row), pl.BlockSpec((8, CV_W), prev), pl.BlockSpec((8, CV_W), nxt),
            pl.BlockSpec((None, 3, 3 * RWKV_WIDTH), lmap),
            pl.BlockSpec((None, 1, w2), lmap),
            pl.BlockSpec((None, LANES, w2), lmap),
            pl.BlockSpec((None, 1, w2), lmap),
            pl.BlockSpec((None, LANES, w2), lmap),
            pl.BlockSpec((None, LANES, RWKV_WIDTH), lmap),
            pl.BlockSpec((None, 1, RWKV_WIDTH), lmap),
            pl.BlockSpec((None, 1, RWKV_WIDTH), lmap),
            pl.BlockSpec((None, 1, RWKV_WIDTH), lmap),
            pl.BlockSpec((None, 3, CONV_WIDTH), lmap),
        ],
        out_specs=[
            pl.BlockSpec((tm, RWKV_WIDTH), row), pl.BlockSpec((tm, RWKV_WIDTH), row),
            pl.BlockSpec((tm, RWKV_WIDTH), row),
            pl.BlockSpec((tm, w2), row), pl.BlockSpec((tm, w2), row), pl.BlockSpec((tm, w2), row),
            pl.BlockSpec((tm, RWKV_WIDTH), row), pl.BlockSpec((tm, RWKV_WIDTH), row),
            pl.BlockSpec((tm, CONV_WIDTH), row),
        ],
        out_shape=[f32(RWKV_WIDTH), f32(RWKV_WIDTH), f32(RWKV_WIDTH), f32(w2), f32(w2), f32(w2),
                   f32(RWKV_WIDTH), f32(RWKV_WIDTH), jax.ShapeDtypeStruct((lay.n_tok, CONV_WIDTH), BF16)],
        compiler_params=_cparams(("parallel",)),
        name="rwkv_prep",
    )(rw, rw, rw, cv, cv, cv, rkv_conv, w0, wd, a0, wa, wg, k_k, k_a, r_k, conv_w)


def _split(a):
    hi = a.astype(BF16)
    return hi, (a - hi.astype(F32)).astype(BF16)


_NN = (((1,), (0,)), ((), ()))
_NT = (((1,), (1,)), ((), ()))
_TN = (((0,), (0,)), ((), ()))


def _dot3(a, b, dims=_NN):
    (ah, al), (bh, bl) = a, b
    dg = lambda p, q: lax.dot_general(p, q, dims, preferred_element_type=F32)
    return dg(ah, bh) + dg(ah, bl) + dg(al, bh)


def _dot1(a, b, dims=_NN):
    return lax.dot_general(a.astype(BF16), b.astype(BF16), dims, preferred_element_type=F32)


def _chunk_terms_kernel(r_ref, v_ref, kk_ref, lw_ref, bb_ref, kd_ref, g_ref, h_ref, rq_ref, y0_ref):
    c = SCAN_CHUNK
    n = RWKV_N
    row = lax.broadcasted_iota(jnp.int32, (c, c), 0)
    col = lax.broadcasted_iota(jnp.int32, (c, c), 1)
    row2 = lax.broadcasted_iota(jnp.int32, (c, 2 * c), 0)
    col2 = lax.broadcasted_iota(jnp.int32, (c, 2 * c), 1) % c
    eye = row == col
    n_chunks = r_ref.shape[0] // c

    chains = []
    for ck in range(n_chunks):
        rows = slice(ck * c, (ck + 1) * c)
        r_all = r_ref[rows, :]
        v_all = v_ref[rows, :]
        kk = kk_ref[rows, :]
        for d in range(2):
            sgn = 1 if d == 0 else -1
            dsl = slice(d * RWKV_WIDTH, (d + 1) * RWKV_WIDTH)
            incl = (col - row) * sgn <= 0
            lw = lw_ref[rows, dsl]
            bb = bb_ref[rows, dsl]
            kd = kd_ref[rows, dsl]
            lp = _dot_mask_lhs(incl, lw)
            lt = jnp.sum(lw, axis=0, keepdims=True)
            p_inv = jnp.exp(-lp)
            p_end = jnp.exp(lt - lp)
            a_t = -kk * jnp.exp(lp - lw)
            b_t = bb * p_inv
            k_t = kd * p_inv
            r_t = r_all * jnp.exp(lp)
            k_e = kd * p_end
            b_e = bb * p_end
            p_tot = jnp.exp(lt)
            for h in range(RWKV_HEADS):
                sl = slice(h * n, (h + 1) * n)
                chains.append(dict(
                    strict=(col - row) * sgn < 0, incl2=(col2 - row2) * sgn <= 0,
                    ah=a_t[:, sl], rh=r_t[:, sl], vh=v_all[:, sl], bt=b_t[:, sl], kt=k_t[:, sl],
                    be=b_e[:, sl], ke=k_e[:, sl], ptot=p_tot[:, sl]))

    for ch in chains:
        ch["sc"] = _dot3(_split(jnp.concatenate([ch["ah"], ch["rh"]], axis=0)),
                         _split(jnp.concatenate([ch["bt"], ch["kt"]], axis=0)), _NT)
    for ch in chains:
        sc = ch["sc"]
        ch["nmat"] = jnp.where(ch["strict"], sc[:c, :c], 0.0).astype(BF16)
        ch["mr"] = jnp.where(ch["incl2"], sc[c:, :], 0.0).astype(BF16)
        ch["mkv"] = _dot1(jnp.where(ch["strict"], sc[:c, c:], 0.0), ch["vh"])
    for ch in chains:
        ch["z"] = jnp.concatenate([ch["ah"], ch["mkv"]], axis=1)
    for p in range(6):
        for ch in chains:
            ch["z"] = ch["z"] + _dot1(ch["nmat"], ch["z"])
        if p < 5:
            for ch in chains:
                ch["nmat"] = _dot1(ch["nmat"], ch["nmat"]).astype(BF16)
    for ch in chains:
        z = ch["z"]
        ch["ws"] = z[:, :n].astype(BF16)
        ch["uv"] = jnp.concatenate([z[:, n:], ch["vh"]], axis=0).astype(BF16)
    for ch in chains:
        mr = ch["mr"]
        ch["rq"] = ch["rh"] + _dot1(mr[:, :c], ch["ws"])
        ch["y0"] = _dot1(mr, ch["uv"])
        ch["g"] = jnp.where(eye, ch["ptot"], 0.0) + _dot1(ch["ws"], ch["be"], _TN)
        ch["h"] = _dot1(ch["uv"], jnp.concatenate([ch["be"], ch["ke"]], axis=0), _TN)
    for ck in range(n_chunks):
        rows = slice(ck * c, (ck + 1) * c)
        for d in range(2):
            first = (ck * 2 + d) * RWKV_HEADS
            part = chains[first:first + RWKV_HEADS]
            g_ref[d, rows, :] = jnp.concatenate([ch["g"] for ch in part], axis=1)
            h_ref[d, rows, :] = jnp.concatenate([ch["h"] for ch in part], axis=1)
            rq_ref[d, rows, :] = jnp.concatenate([ch["rq"] for ch in part], axis=1)
            y0_ref[d, rows, :] = jnp.concatenate([ch["y0"] for ch in part], axis=1)


def _chunk_terms(lay, r, v, kk, lw, bb, kd, chunks_per_step):
    c = SCAN_CHUNK * chunks_per_step
    row = lambda i: (i, 0)
    out = pl.BlockSpec((2, c, RWKV_WIDTH), lambda i: (0, i, 0))
    shp = jax.ShapeDtypeStruct((2, lay.n_tok, RWKV_WIDTH), F32)
    return pl.pallas_call(
        _chunk_terms_kernel,
        grid=(lay.n_tok // c,),
        in_specs=[pl.BlockSpec((c, RWKV_WIDTH), row)] * 3 + [pl.BlockSpec((c, 2 * RWKV_WIDTH), row)] * 3,
        out_specs=[out] * 4,
        out_shape=[shp] * 4,
        compiler_params=_cparams(("parallel",)),
        name="rwkv_chunk_terms",
    )(r, v, kk, lw, bb, kd)


def _scan_kernel(gf_ref, hf_ref, rqf_ref, y0f_ref, gb_ref, hb_ref, rqb_ref, y0b_ref, yf_ref, yb_ref, st_ref):
    @pl.when(pl.program_id(1) == 0)
    def _():
        st_ref[...] = jnp.zeros_like(st_ref)

    dirs = ((gf_ref, hf_ref, rqf_ref, y0f_ref, yf_ref), (gb_ref, hb_ref, rqb_ref, y0b_ref, yb_ref))
    heads = [slice(h * RWKV_N, (h + 1) * RWKV_N) for h in range(RWKV_HEADS)]
    s0 = [[_split(st_ref[d, h]) for h in range(RWKV_HEADS)] for d in range(2)]
    new = [[_dot3(s0[d][h], _split(dirs[d][0][:, sl])) + dirs[d][1][:, sl] for h, sl in enumerate(heads)]
           for d in range(2)]
    for d in range(2):
        rq_ref, y0_ref, y_ref = dirs[d][2:]
        y_ref[...] = jnp.concatenate(
            [_dot1(rq_ref[:, sl], s0[d][h][0], _NT) + y0_ref[:, sl] for h, sl in enumerate(heads)], axis=1)
    for d in range(2):
        for h in range(RWKV_HEADS):
            st_ref[d, h] = new[d][h]


def _scan(lay, g, h, rq, y0):
    c = SCAN_CHUNK
    nc_ctx = lay.ctx_len // c
    nc_lat = lay.seq // c
    lat_blocks = lay.n_lat // c

    def blk(b, d, ci):
        if d == 0:
            return jnp.where(ci < nc_ctx, lat_blocks + b * nc_ctx + ci, b * nc_lat + (ci - nc_ctx))
        return jnp.where(ci < nc_ctx, lat_blocks + b * nc_ctx + (nc_ctx - 1 - ci),
                         b * nc_lat + (nc_lat - 1 - (ci - nc_ctx)))

    def term(d):
        return pl.BlockSpec((None, c, RWKV_WIDTH), lambda b, ci: (d, blk(b, d, ci), 0))

    def yspec(d):
        return pl.BlockSpec((c, RWKV_WIDTH), lambda b, ci: (blk(b, d, ci), 0))

    shp = jax.ShapeDtypeStruct((lay.n_tok, RWKV_WIDTH), F32)
    return pl.pallas_call(
        _scan_kernel,
        grid=(lay.batch, nc_ctx + nc_lat),
        in_specs=[term(0)] * 4 + [term(1)] * 4,
        out_specs=[yspec(0), yspec(1)],
        out_shape=[shp, shp],
        scratch_shapes=[pltpu.VMEM((2, RWKV_HEADS, RWKV_N, RWKV_N), F32)],
        compiler_params=_cparams(("parallel", "arbitrary")),
        name="rwkv_scan",
    )(g, h, rq, y0, g, h, rq, y0)


def _outproj_kernel(att_ref, yf_ref, yb_ref, bv_ref, g_ref, co_ref, x_ref, gate_ref, w_ref,
                    gng_ref, gnb_ref, lng_ref, lnb_ref, o_ref):
    ones = _head_ones(RWKV_WIDTH, RWKV_N)
    y = yf_ref[...] + yb_ref[...]
    mu = _dot_mask_rhs(y, ones) * (1.0 / RWKV_N)
    yc = y - mu
    var = _dot_mask_rhs(yc * yc, ones) * (1.0 / RWKV_N)
    yn = yc * lax.rsqrt(var + GN_EPS) * gng_ref[...] + gnb_ref[...]
    rwkv = ((yn + bv_ref[...]) * g_ref[...]).astype(BF16)
    mix = (_dot(att_ref[...], w_ref[:ATT_WIDTH, :])
           + _dot(rwkv, w_ref[ATT_WIDTH:ATT_WIDTH + RWKV_WIDTH, :])
           + _dot(co_ref[...], w_ref[ATT_WIDTH + RWKV_WIDTH:, :]))
    z = DN_ALPHA * x_ref[...] + gate_ref[...] * mix
    o_ref[...] = _layer_norm(z, lng_ref[...], lnb_ref[...])


def _outproj(lay, layer, att, yf, yb, bv, g, co, x, mods, w_out, gn_g, gn_b, ln_g, ln_b, tm):
    row = lambda i: (i, 0)
    lmap = lambda i: (layer, 0, 0)
    return pl.pallas_call(
        _outproj_kernel,
        grid=(lay.n_tok // tm,),
        in_specs=[
            pl.BlockSpec((tm, ATT_WIDTH), row),
            pl.BlockSpec((tm, RWKV_WIDTH), row),
            pl.BlockSpec((tm, RWKV_WIDTH), row),
            pl.BlockSpec((tm, RWKV_WIDTH), row),
            pl.BlockSpec((tm, RWKV_WIDTH), row),
            pl.BlockSpec((tm, CONV_WIDTH), row),
            pl.BlockSpec((tm, D_MODEL), row),
            lay.mod_spec(layer, tm, 2),
            pl.BlockSpec((None, D_MODEL, D_MODEL), lmap),
            pl.BlockSpec((None, 1, RWKV_WIDTH), lmap),
            pl.BlockSpec((None, 1, RWKV_WIDTH), lmap),
            pl.BlockSpec((None, 1, D_MODEL), lmap),
            pl.BlockSpec((None, 1, D_MODEL), lmap),
        ],
        out_specs=pl.BlockSpec((tm, D_MODEL), row),
        out_shape=jax.ShapeDtypeStruct((lay.n_tok, D_MODEL), F32),
        compiler_params=_cparams(("parallel",)),
        name="outproj_ln1",
    )(att, yf, yb, bv, g, co, x, mods, w_out, gn_g, gn_b, ln_g, ln_b)


def _ffn_kernel(x_ref, sh_ref, sc_ref, gate_ref, w1_ref, w3_ref, w2_ref, lng_ref, lnb_ref, o_ref, *, n_split):
    x = x_ref[...]
    h = (x * (1.0 + sc_ref[...]) + sh_ref[...]).astype(BF16)
    step = D_FF // n_split
    f = jnp.zeros(x.shape, F32)
    for s in range(n_split):
        cols = slice(s * step, (s + 1) * step)
        a = _dot(h, w1_ref[:, cols])
        act = (a * jax.nn.sigmoid(a) * _dot(h, w3_ref[:, cols])).astype(BF16)
        f = f + _dot(act, w2_ref[cols, :])
    z = DN_ALPHA * x + gate_ref[...] * f
    o_ref[...] = _layer_norm(z, lng_ref[...], lnb_ref[...])


def _ffn(lay, layer, j, x, mods, w1, w3, w2, ln_g, ln_b, tm):
    row = lambda i: (i, 0)
    once = pl.Buffered(1)
    return pl.pallas_call(
        functools.partial(_ffn_kernel, n_split=2),
        grid=(lay.n_tok // tm,),
        in_specs=[
            pl.BlockSpec((tm, D_MODEL), row),
            lay.mod_spec(layer, tm, 3), lay.mod_spec(layer, tm, 4), lay.mod_spec(layer, tm, 5),
            pl.BlockSpec((None, D_MODEL, D_FF), lambda i: (j, 0, 0), pipeline_mode=once),
            pl.BlockSpec((None, D_MODEL, D_FF), lambda i: (j, 0, 0), pipeline_mode=once),
            pl.BlockSpec((None, D_FF, D_MODEL), lambda i: (j, 0, 0), pipeline_mode=once),
            pl.BlockSpec((None, 1, D_MODEL), lambda i: (layer, 0, 0)),
            pl.BlockSpec((None, 1, D_MODEL), lambda i: (layer, 0, 0)),
        ],
        out_specs=pl.BlockSpec((tm, D_MODEL), row),
        out_shape=jax.ShapeDtypeStruct((lay.n_tok, D_MODEL), F32),
        compiler_params=_cparams(("parallel",)),
        name="ffn_ln2",
    )(x, mods, mods, mods, w1, w3, w2, ln_g, ln_b)


def _moe_kernel(x_ref, sh_ref, sc_ref, gate_ref, rw_ref, rb_ref, w1_ref, w3_ref, w2_ref, lng_ref, lnb_ref,
                o_ref, h_ref, gates_ref, acc_ref):
    e = pl.program_id(1)
    lane = lax.broadcasted_iota(jnp.int32, gates_ref.shape, 1)

    @pl.when(e == 0)
    def _():
        h = x_ref[...] * (1.0 + sc_ref[...]) + sh_ref[...]
        h_ref[...] = h.astype(BF16)
        logits = _dot_hi(h, rw_ref[...]) + rb_ref[...]
        m1 = jnp.max(logits, axis=-1, keepdims=True)
        i1 = jnp.min(jnp.where(logits == m1, lane, LANES), axis=-1, keepdims=True)
        rest = jnp.where(lane == i1, -jnp.inf, logits)
        m2 = jnp.max(rest, axis=-1, keepdims=True)
        i2 = jnp.min(jnp.where(rest == m2, lane, LANES), axis=-1, keepdims=True)
        e2 = jnp.exp(m2 - m1)
        den = 1.0 + e2
        gates_ref[...] = jnp.where(lane == i1, 1.0 / den, 0.0) + jnp.where(lane == i2, e2 / den, 0.0)
        acc_ref[...] = jnp.zeros_like(acc_ref)

    h = h_ref[...]
    a = _dot(h, w1_ref[...])
    act = (a * jax.nn.sigmoid(a) * _dot(h, w3_ref[...])).astype(BF16)
    ge = jnp.sum(jnp.where(lane == e, gates_ref[...], 0.0), axis=-1, keepdims=True)
    acc_ref[...] += ge * _dot(act, w2_ref[...])

    @pl.when(e == N_EXPERTS - 1)
    def _():
        z = DN_ALPHA * x_ref[...] + gate_ref[...] * acc_ref[...]
        o_ref[...] = _layer_norm(z, lng_ref[...], lnb_ref[...])


def _moe(lay, layer, j, x, mods, router_w, router_b, w1, w3, w2, ln_g, ln_b, tm):
    row = lambda i, e: (i, 0)
    return pl.pallas_call(
        _moe_kernel,
        grid=(lay.n_tok // tm, N_EXPERTS),
        in_specs=[
            pl.BlockSpec((tm, D_MODEL), row),
            lay.mod_spec(layer, tm, 3), lay.mod_spec(layer, tm, 4), lay.mod_spec(layer, tm, 5),
            pl.BlockSpec((None, D_MODEL, LANES), lambda i, e: (j, 0, 0)),
            pl.BlockSpec((None, 1, LANES), lambda i, e: (j, 0, 0)),
            pl.BlockSpec((None, None, D_MODEL, D_FF_EXPERT), lambda i, e: (j, e, 0, 0)),
            pl.BlockSpec((None, None, D_MODEL, D_FF_EXPERT), lambda i, e: (j, e, 0, 0)),
            pl.BlockSpec((None, None, D_FF_EXPERT, D_MODEL), lambda i, e: (j, e, 0, 0)),
            pl.BlockSpec((None, 1, D_MODEL), lambda i, e: (layer, 0, 0)),
            pl.BlockSpec((None, 1, D_MODEL), lambda i, e: (layer, 0, 0)),
        ],
        out_specs=pl.BlockSpec((tm, D_MODEL), row),
        out_shape=jax.ShapeDtypeStruct((lay.n_tok, D_MODEL), F32),
        scratch_shapes=[pltpu.VMEM((tm, D_MODEL), BF16), pltpu.VMEM((tm, LANES), F32),
                        pltpu.VMEM((tm, D_MODEL), F32)],
        compiler_params=_cparams(("parallel", "arbitrary")),
        name="moe_ln2",
    )(x, mods, mods, mods, router_w, router_b, w1, w3, w2, ln_g, ln_b)


def _rope_tables(ctx_len, seq):
    t = np.arange(seq)
    inv = ROPE_THETA ** (-np.arange(0, AXIS_DIM, 2, dtype=np.float64) / AXIS_DIM)
    ang_r = (t // GRID_W)[:, None].astype(np.float64) * inv
    ang_c = (t % GRID_W)[:, None].astype(np.float64) * inv
    cos = np.concatenate([np.cos(ang_r), np.cos(ang_r), np.cos(ang_c), np.cos(ang_c)], axis=1)
    sin = np.concatenate([-np.sin(ang_r), np.sin(ang_r), -np.sin(ang_c), np.sin(ang_c)], axis=1)
    cos = np.concatenate([cos, np.ones((ctx_len, ATT_QK))], axis=0)
    sin = np.concatenate([sin, np.zeros((ctx_len, ATT_QK))], axis=0)
    reps = LANES // ATT_QK
    return (jnp.asarray(np.tile(cos, (1, reps)), F32), jnp.asarray(np.tile(sin, (1, reps)), F32))


def kernel(x, c, ctx, c_ctx, w_ada, b_ada, w_in, w_out, ln1_g, ln1_b, ln2_g, ln2_b,
           lam_q1, lam_k1, lam_q2, lam_k2, subln_g, rkv_conv, decay_w0, decay_up, iclr_a0, iclr_up,
           gate_up, k_k, k_a, r_k, gn_g, gn_b, conv_w, ffn_w1, ffn_w3, ffn_w2,
           router_w, router_b, moe_w1, moe_w3, moe_w2):
    batch, seq, d = x.shape
    ctx_len = ctx.shape[1]
    depth = w_in.shape[0]
    assert d == D_MODEL and depth == DEPTH and batch < MOD_ROWS
    tm = 256
    tq = 256
    tk = 512
    tm_moe = 512
    scan_chunks_per_step = 4
    assert ctx_len % tm == 0 and seq % tm == 0 and (batch * ctx_len) % tm_moe == 0 and seq % tm_moe == 0
    assert seq % ctx_len == 0 and ctx_len % SCAN_CHUNK == 0 and seq % GRID_W == 0
    lay = _Layout(batch, ctx_len, seq)

    tokens = jnp.concatenate([x.reshape(batch * seq, d), ctx.reshape(batch * ctx_len, d)], axis=0)

    c_all = jnp.concatenate([c, c_ctx[None, :], jnp.zeros((MOD_ROWS - batch - 1, d), F32)], axis=0)
    mods = _adaln(c_all, w_ada, b_ada).reshape(depth * MOD_ROWS, 1, 6 * d)

    cos, sin = _rope_tables(ctx_len, seq)

    o = np.cumsum([0, 512, 512, 512, 256, 256, 256, 64, 64, 64, 256, 256, 256])
    wq = w_in[:, :, :o[3]].astype(BF16)
    wr = jnp.concatenate([w_in[:, :, o[3]:o[9]], jnp.zeros((depth, d, RW_W - (o[9] - o[3])), F32)],
                         axis=-1).astype(BF16)
    wc = w_in[:, :, o[9]:].astype(BF16)
    w_out_b = w_out.astype(BF16)

    z = jnp.zeros((depth, DECAY_LORA, RWKV_WIDTH), F32)
    wd = jnp.concatenate([
        jnp.concatenate([decay_up[:, 0], z], axis=-1),
        jnp.concatenate([z, decay_up[:, 1]], axis=-1),
        jnp.zeros((depth, 2 * ICLR_LORA, 2 * RWKV_WIDTH), F32)], axis=1)
    wa = jnp.concatenate([
        jnp.zeros((depth, 2 * DECAY_LORA, 2 * RWKV_WIDTH), F32),
        jnp.concatenate([iclr_up[:, 0], z], axis=-1),
        jnp.concatenate([z, iclr_up[:, 1]], axis=-1)], axis=1)
    wg = jnp.concatenate([gate_up, jnp.zeros((depth, LANES - GATE_LORA, RWKV_WIDTH), F32)], axis=1)
    w0 = decay_w0.reshape(depth, 1, 2 * RWKV_WIDTH)
    a0 = iclr_a0.reshape(depth, 1, 2 * RWKV_WIDTH)
    vec = lambda w: w.reshape(depth, 1, -1)

    lam4 = jnp.stack([lam_q1, lam_k1, lam_q2, lam_k2], axis=1)

    ffn_w1_b, ffn_w3_b, ffn_w2_b = ffn_w1.astype(BF16), ffn_w3.astype(BF16), ffn_w2.astype(BF16)
    moe_w1_b, moe_w3_b, moe_w2_b = moe_w1.astype(BF16), moe_w3.astype(BF16), moe_w2.astype(BF16)
    n_moe = router_w.shape[0]
    router_w_p = jnp.concatenate([router_w, jnp.zeros((n_moe, d, LANES - N_EXPERTS), F32)], axis=-1)
    router_b_p = jnp.concatenate([router_b, jnp.full((n_moe, LANES - N_EXPERTS), -1e30, F32)],
                                 axis=-1).reshape(n_moe, 1, LANES)

    xs = tokens
    for l in range(depth):
        lam_init = 0.8 - 0.6 * math.exp(-0.3 * l)
        qkv, rw, cv = _inproj(lay, l, xs, mods, wq, wr, wc, cos, sin, tm)
        att = _attention(lay, l, qkv, lam4, vec(subln_g), lam_init, tq, tk)
        r, v, kk, lw, bb, kd, bv, g, co = _prep(lay, l, rw, cv, rkv_conv, w0, wd, a0, wa, wg,
                                                vec(k_k), vec(k_a), vec(r_k), conv_w, tm)
        yf, yb = _scan(lay, *_chunk_terms(lay, r, v, kk, lw, bb, kd, scan_chunks_per_step))
        xs = _outproj(lay, l, att, yf, yb, bv, g, co, xs, mods, w_out_b, vec(gn_g), vec(gn_b),
                      vec(ln1_g), vec(ln1_b), tm)
        if l % 2 == 0:
            xs = _ffn(lay, l, l // 2, xs, mods, ffn_w1_b, ffn_w3_b, ffn_w2_b, vec(ln2_g), vec(ln2_b), tm)
        else:
            xs = _moe(lay, l, l // 2, xs, mods, router_w_p, router_b_p, moe_w1_b, moe_w3_b, moe_w2_b,
                      vec(ln2_g), vec(ln2_b), tm_moe)
    return xs[:lay.n_lat].reshape(batch, seq, d)
```

```python
import functools
import math

import numpy as np
import jax
import jax.numpy as jnp
from jax import lax
from jax.experimental import pallas as pl
from jax.experimental.pallas import tpu as pltpu

F32 = jnp.float32
BF16 = jnp.bfloat16
HI = lax.Precision.HIGHEST

D_MODEL = 1024
DEPTH = 4
GRID_W = 64
ATT_HEADS = 4
ATT_QK = 64
ATT_V = 128
ATT_WIDTH = 512
AXIS_DIM = 32
ROPE_THETA = 10000.0
SUBLN_EPS = 1e-5
RWKV_HEADS = 4
RWKV_N = 64
RWKV_WIDTH = 256
DECAY_LORA = 32
ICLR_LORA = 32
GATE_LORA = 64
GN_EPS = 64e-5
CONV_WIDTH = 256
D_FF = 2816
N_EXPERTS = 8
D_FF_EXPERT = 1408
DN_ALPHA = (2 * DEPTH) ** 0.25
LN_EPS = 1e-5

LANES = 128
MOD_ROWS = 16
QKV_W = 3 * ATT_WIDTH
RW_W = 1024
CV_W = 3 * CONV_WIDTH
SCAN_CHUNK = 64
VMEM_LIMIT = 56 * 1024 * 1024


def _cparams(sem):
    return pltpu.CompilerParams(dimension_semantics=sem, vmem_limit_bytes=VMEM_LIMIT)


def _dot(a, b):
    return jnp.dot(a, b, preferred_element_type=F32)


def _dot_hi(a, b):
    return jnp.dot(a, b, precision=HI, preferred_element_type=F32)


def _dot_nt_hi(a, b):
    return lax.dot_general(a, b, (((1,), (1,)), ((), ())), precision=HI, preferred_element_type=F32)


def _dot_tn_hi(a, b):
    return lax.dot_general(a, b, (((0,), (0,)), ((), ())), precision=HI, preferred_element_type=F32)


def _split3(a):
    hi = a.astype(BF16)
    rest = a - hi.astype(F32)
    mid = rest.astype(BF16)
    return hi, mid, (rest - mid.astype(F32)).astype(BF16)


def _dot_mask_lhs(mask, a):
    m = jnp.where(mask, 1.0, 0.0).astype(BF16)
    return sum(jnp.dot(m, part, preferred_element_type=F32) for part in _split3(a))


def _dot_mask_rhs(a, mask):
    m = mask.astype(BF16)
    return sum(jnp.dot(part, m, preferred_element_type=F32) for part in _split3(a))


def _layer_norm(z, g, b):
    mu = jnp.mean(z, axis=-1, keepdims=True)
    zc = z - mu
    var = jnp.mean(zc * zc, axis=-1, keepdims=True)
    return zc * lax.rsqrt(var + LN_EPS) * g + b


def _head_ones(width, head):
    r = lax.broadcasted_iota(jnp.int32, (width, width), 0) // head
    c = lax.broadcasted_iota(jnp.int32, (width, width), 1) // head
    return (r == c).astype(F32)


def _ada_kernel(c_ref, w_ref, b_ref, o_ref):
    c = c_ref[...]
    sc = c * jax.nn.sigmoid(c)
    o_ref[...] = _dot(sc.astype(BF16), w_ref[...].astype(BF16)) + b_ref[...]


def _adaln(c_all, w_ada, b_ada):
    depth, d, n = w_ada.shape
    tn = 1536
    return pl.pallas_call(
        _ada_kernel,
        grid=(depth, n // tn),
        in_specs=[
            pl.BlockSpec((MOD_ROWS, d), lambda l, j: (0, 0)),
            pl.BlockSpec((None, d, tn), lambda l, j: (l, 0, j)),
            pl.BlockSpec((None, 1, tn), lambda l, j: (l, 0, j)),
        ],
        out_specs=pl.BlockSpec((None, MOD_ROWS, tn), lambda l, j: (l, 0, j)),
        out_shape=jax.ShapeDtypeStruct((depth, MOD_ROWS, n), F32),
        compiler_params=_cparams(("parallel", "parallel")),
        name="adaln",
    )(c_all, w_ada, b_ada.reshape(depth, 1, n))


class _Layout:
    def __init__(self, batch, ctx_len, seq):
        self.batch, self.ctx_len, self.seq = batch, ctx_len, seq
        self.n_lat = batch * seq
        self.n_tok = self.n_lat + batch * ctx_len

    def mod_spec(self, layer, tm, which):
        n_lat_tiles = self.n_lat // tm
        rows_per = self.seq // tm
        batch = self.batch

        def imap(i, *_):
            row = jnp.where(i < n_lat_tiles, i // rows_per, batch)
            return (layer * MOD_ROWS + row, 0, which)

        return pl.BlockSpec((None, 1, D_MODEL), imap)


def _inproj_kernel(x_ref, sh_ref, sc_ref, wq_ref, wr_ref, wc_ref, cos_ref, sin_ref,
                   oq_ref, or_ref, oc_ref):
    h = (x_ref[...] * (1.0 + sc_ref[...]) + sh_ref[...]).astype(BF16)
    qkv = _dot(h, wq_ref[...])
    cos = cos_ref[...]
    sin = sin_ref[...]
    lane = lax.broadcasted_iota(jnp.int32, cos.shape, 1)
    first_half = (lane % AXIS_DIM) < (AXIS_DIM // 2)
    for s in range(2 * ATT_WIDTH // LANES):
        seg = qkv[:, s * LANES:(s + 1) * LANES]
        swapped = jnp.where(first_half, pltpu.roll(seg, LANES - AXIS_DIM // 2, 1),
                            pltpu.roll(seg, AXIS_DIM // 2, 1))
        rot = seg * cos + swapped * sin
        if s < ATT_WIDTH // LANES:
            rot = rot * (ATT_QK ** -0.5 * math.log2(math.e))
        oq_ref[:, s * LANES:(s + 1) * LANES] = rot.astype(BF16)
    oq_ref[:, 2 * ATT_WIDTH:] = qkv[:, 2 * ATT_WIDTH:].astype(BF16)
    or_ref[...] = _dot(h, wr_ref[...])
    oc_ref[...] = _dot(h, wc_ref[...])


def _inproj(lay, layer, x, mods, wq, wr, wc, cos, sin, tm):
    n_lat_tiles = lay.n_lat // tm
    ctx_tiles_per = lay.ctx_len // tm
    lat_tiles_per = lay.seq // tm

    def rope_map(i):
        return (jnp.where(i < n_lat_tiles, i % lat_tiles_per,
                          lat_tiles_per + (i - n_lat_tiles) % ctx_tiles_per), 0)

    row = lambda i: (i, 0)
    wmap = lambda i: (layer, 0, 0)
    return pl.pallas_call(
        _inproj_kernel,
        grid=(lay.n_tok // tm,),
        in_specs=[
            pl.BlockSpec((tm, D_MODEL), row),
            lay.mod_spec(layer, tm, 0),
            lay.mod_spec(layer, tm, 1),
            pl.BlockSpec((None, D_MODEL, QKV_W), wmap),
            pl.BlockSpec((None, D_MODEL, RW_W), wmap),
            pl.BlockSpec((None, D_MODEL, CV_W), wmap),
            pl.BlockSpec((tm, LANES), rope_map),
            pl.BlockSpec((tm, LANES), rope_map),
        ],
        out_specs=[
            pl.BlockSpec((tm, QKV_W), row),
            pl.BlockSpec((tm, RW_W), row),
            pl.BlockSpec((tm, CV_W), row),
        ],
        out_shape=[
            jax.ShapeDtypeStruct((lay.n_tok, QKV_W), BF16),
            jax.ShapeDtypeStruct((lay.n_tok, RW_W), F32),
            jax.ShapeDtypeStruct((lay.n_tok, CV_W), F32),
        ],
        compiler_params=_cparams(("parallel",)),
        name="inproj",
    )(x, mods, mods, wq, wr, wc, cos, sin)


def _attn_kernel(q_ref, kc_ref, vc_ref, kl_ref, vl_ref, lam_ref, g_ref, o_ref, vxc_ref, vxl_ref,
                 *, lam_init, n_lat_q, tq, tk):
    @pl.when(pl.program_id(2) == 0)
    def _():
        vxc_ref[:, :ATT_V] = vc_ref[...]
        vxc_ref[:, ATT_V:] = jnp.ones((vxc_ref.shape[0], LANES), BF16)
        vxl_ref[:, :ATT_V] = vl_ref[...]
        vxl_ref[:, ATT_V:] = jnp.ones((vxl_ref.shape[0], LANES), BF16)

    q = q_ref[...]
    lane = lax.broadcasted_iota(jnp.int32, q.shape, 1)
    zero = jnp.zeros_like(q)
    qq = jnp.concatenate([jnp.where(lane < ATT_QK, q, zero), jnp.where(lane >= ATT_QK, q, zero)], axis=0)
    lam4 = lam_ref[...]
    lam = (jnp.exp(jnp.sum(lam4[0:1] * lam4[1:2], axis=-1, keepdims=True))
           - jnp.exp(jnp.sum(lam4[2:3] * lam4[3:4], axis=-1, keepdims=True)) + lam_init)

    def scores(blk):
        k_ref, _, start, size = blk
        return lax.dot_general(qq, k_ref[start:start + size, :], _NT, preferred_element_type=F32)

    def attend(blocks):
        s = scores(blocks[0])
        m = acc = None
        for j, (_, vx_ref, start, size) in enumerate(blocks):
            s_next = scores(blocks[j + 1]) if j + 1 < len(blocks) else None
            bm = jnp.max(s, axis=-1, keepdims=True)
            m_new = bm if m is None else jnp.maximum(m, bm)
            p = jnp.exp2(s - m_new).astype(BF16)
            pv = _dot(p, vx_ref[start:start + size, :])
            acc = pv if m is None else acc * jnp.exp2(m - m_new) + pv
            m, s = m_new, s_next
        o = acc[:, :ATT_V] / acc[:, ATT_V:]
        dlt = o[:tq] - lam * o[tq:]
        ms = jnp.mean(dlt * dlt, axis=-1, keepdims=True)
        o_ref[...] = (dlt * lax.rsqrt(ms + SUBLN_EPS) * g_ref[...] * (1.0 - lam_init)).astype(o_ref.dtype)

    ctx_blocks = [(kc_ref, vxc_ref, 0, kc_ref.shape[0])]
    lat_blocks = [(kl_ref, vxl_ref, s0, tk) for s0 in range(0, kl_ref.shape[0], tk)]
    is_ctx = pl.program_id(2) >= n_lat_q

    @pl.when(is_ctx)
    def _():
        attend(ctx_blocks)

    @pl.when(jnp.logical_not(is_ctx))
    def _():
        attend(ctx_blocks + lat_blocks)


def _attention(lay, layer, qkv, lam4, subln_g, lam_init, tq, tk):
    n_ctx_q = lay.ctx_len // tq
    n_lat_q = lay.seq // tq
    n_lat_tiles = lay.n_lat // tq
    ctx_blk0 = lay.n_lat // lay.ctx_len
    hq = ATT_WIDTH // LANES

    def qmap(b, h, qi):
        return (jnp.where(qi < n_lat_q, b * n_lat_q + qi, n_lat_tiles + b * n_ctx_q + (qi - n_lat_q)), h)

    kern = functools.partial(_attn_kernel, lam_init=lam_init, n_lat_q=n_lat_q, tq=tq, tk=tk)
    return pl.pallas_call(
        kern,
        grid=(lay.batch, ATT_HEADS, n_lat_q + n_ctx_q),
        in_specs=[
            pl.BlockSpec((tq, LANES), qmap),
            pl.BlockSpec((lay.ctx_len, LANES), lambda b, h, qi: (ctx_blk0 + b, hq + h)),
            pl.BlockSpec((lay.ctx_len, LANES), lambda b, h, qi: (ctx_blk0 + b, 2 * hq + h)),
            pl.BlockSpec((lay.seq, LANES), lambda b, h, qi: (b, hq + h)),
            pl.BlockSpec((lay.seq, LANES), lambda b, h, qi: (b, 2 * hq + h)),
            pl.BlockSpec((None, 4, ATT_QK), lambda b, h, qi: (layer, 0, 0)),
            pl.BlockSpec((None, 1, ATT_V), lambda b, h, qi: (layer, 0, 0)),
        ],
        out_specs=pl.BlockSpec((tq, LANES), qmap),
        out_shape=jax.ShapeDtypeStruct((lay.n_tok, ATT_WIDTH), BF16),
        scratch_shapes=[pltpu.VMEM((lay.ctx_len, ATT_V + LANES), BF16),
                        pltpu.VMEM((lay.seq, ATT_V + LANES), BF16)],
        compiler_params=_cparams(("parallel", "parallel", "arbitrary")),
        name="diff_attn",
    )(qkv, qkv, qkv, qkv, qkv, lam4, subln_g)


def _prep_kernel(rw_ref, rwp_ref, rwn_ref, cv_ref, cvp_ref, cvn_ref,
                 rkvw_ref, w0_ref, wd_ref, a0_ref, wa_ref, wg_ref, kk_w_ref, ka_ref, rk_ref, cw_ref,
                 r_ref, v_ref, kk_ref, lw_ref, bb_ref, kd_ref, bv_ref, g_ref, co_ref,
                 *, tm, n_lat, ctx_len, seq):
    i = pl.program_id(0)
    start = i * tm
    in_lat = start < n_lat
    seg_pos = jnp.where(in_lat, start % seq, (start - n_lat) % ctx_len)
    seg_len = jnp.where(in_lat, seq, ctx_len)
    has_prev = seg_pos != 0
    has_next = seg_pos + tm != seg_len

    def shifted(u, prev_row, next_row):
        rows = lax.broadcasted_iota(jnp.int32, u.shape, 0)
        prev_row = jnp.where(has_prev, prev_row, jnp.zeros_like(prev_row))
        next_row = jnp.where(has_next, next_row, jnp.zeros_like(next_row))
        up = jnp.where(rows == 0, prev_row, pltpu.roll(u, 1, 0))
        un = jnp.where(rows == tm - 1, next_row, pltpu.roll(u, tm - 1, 0))
        return up, un

    def conv3(u, prev_row, next_row, w):
        up, un = shifted(u, prev_row, next_row)
        return up * w[0:1] + u * w[1:2] + un * w[2:3]

    rw = rw_ref[...]
    nrkv = 3 * RWKV_WIDTH
    rkv = conv3(rw[:, :nrkv], rwp_ref[7:8, :nrkv], rwn_ref[0:1, :nrkv], rkvw_ref[...])
    r = rkv[:, :RWKV_WIDTH]
    k = rkv[:, RWKV_WIDTH:2 * RWKV_WIDTH]
    v = rkv[:, 2 * RWKV_WIDTH:]

    lora = rw[:, nrkv:nrkv + LANES]
    gate = rw[:, nrkv + LANES:]
    wl = _dot_hi(jnp.tanh(lora), wd_ref[...]) + w0_ref[...]
    lw = (-math.exp(-0.5)) * jax.nn.sigmoid(wl)
    a = jax.nn.sigmoid(_dot_hi(lora, wa_ref[...]) + a0_ref[...])
    g = _dot_hi(jax.nn.sigmoid(gate), wg_ref[...])

    ones = _head_ones(RWKV_WIDTH, RWKV_N)
    kraw = k * kk_w_ref[...]
    ss = _dot_mask_rhs(kraw * kraw, ones)
    kk = kraw * lax.rsqrt(jnp.maximum(ss, 1e-24))
    k2 = jnp.concatenate([k, k], axis=1)
    ka2 = jnp.concatenate([ka_ref[...], ka_ref[...]], axis=1)
    kd = k2 * (1.0 + (a - 1.0) * ka2)
    bb = jnp.concatenate([kk, kk], axis=1) * a
    bonus = _dot_mask_rhs(r * (kd[:, :RWKV_WIDTH] + kd[:, RWKV_WIDTH:]) * rk_ref[...], ones)

    r_ref[...] = r
    v_ref[...] = v
    kk_ref[...] = kk
    lw_ref[...] = lw
    bb_ref[...] = bb
    kd_ref[...] = kd
    bv_ref[...] = bonus * v
    g_ref[...] = g

    def gated(ref):
        return ref[:, 2 * CONV_WIDTH:] * ref[:, :CONV_WIDTH]

    cv_u = gated(cv_ref)
    conv = conv3(cv_u, gated(cvp_ref)[7:8], gated(cvn_ref)[0:1], cw_ref[...])
    co_ref[...] = (cv_ref[:, CONV_WIDTH:2 * CONV_WIDTH] * conv).astype(co_ref.dtype)


def _prep(lay, layer, rw, cv, rkv_conv, w0, wd, a0, wa, wg, k_k, k_a, r_k, conv_w, tm):
    n8 = lay.n_tok // 8
    t8 = tm // 8
    row = lambda i: (i, 0)
    prev = lambda i: (jnp.maximum(i * t8 - 1, 0), 0)
    nxt = lambda i: (jnp.minimum((i + 1) * t8, n8 - 1), 0)
    lmap = lambda i: (layer, 0, 0)
    w2 = 2 * RWKV_WIDTH
    kern = functools.partial(_prep_kernel, tm=tm, n_lat=lay.n_lat, ctx_len=lay.ctx_len, seq=lay.seq)
    f32 = lambda w: jax.ShapeDtypeStruct((lay.n_tok, w), F32)
    return pl.pallas_call(
        kern,
        grid=(lay.n_tok // tm,),
        in_specs=[
            pl.BlockSpec((tm, RW_W), row), pl.BlockSpec((8, RW_W), prev), pl.BlockSpec((8, RW_W), nxt),
            pl.BlockSpec((tm, CV_W), row), pl.BlockSpec((8, CV_W), prev), pl.BlockSpec((8, CV_W), nxt),
            pl.BlockSpec((None, 3, 3 * RWKV_WIDTH), lmap),
            pl.BlockSpec((None, 1, w2), lmap),
            pl.BlockSpec((None, LANES, w2), lmap),
            pl.BlockSpec((None, 1, w2), lmap),
            pl.BlockSpec((None, LANES, w2), lmap),
            pl.BlockSpec((None, LANES, RWKV_WIDTH), lmap),
            pl.BlockSpec((None, 1, RWKV_WIDTH), lmap),
            pl.BlockSpec((None, 1, RWKV_WIDTH), lmap),
            pl.BlockSpec((None, 1, RWKV_WIDTH), lmap),
            pl.BlockSpec((None, 3, CONV_WIDTH), lmap),
        ],
        out_specs=[
            pl.BlockSpec((tm, RWKV_WIDTH), row), pl.BlockSpec((tm, RWKV_WIDTH), row),
            pl.BlockSpec((tm, RWKV_WIDTH), row),
            pl.BlockSpec((tm, w2), row), pl.BlockSpec((tm, w2), row), pl.BlockSpec((tm, w2), row),
            pl.BlockSpec((tm, RWKV_WIDTH), row), pl.BlockSpec((tm, RWKV_WIDTH), row),
            pl.BlockSpec((tm, CONV_WIDTH), row),
        ],
        out_shape=[f32(RWKV_WIDTH), f32(RWKV_WIDTH), f32(RWKV_WIDTH), f32(w2), f32(w2), f32(w2),
                   f32(RWKV_WIDTH), f32(RWKV_WIDTH), jax.ShapeDtypeStruct((lay.n_tok, CONV_WIDTH), BF16)],
        compiler_params=_cparams(("parallel",)),
        name="rwkv_prep",
    )(rw, rw, rw, cv, cv, cv, rkv_conv, w0, wd, a0, wa, wg, k_k, k_a, r_k, conv_w)


def _split(a):
    hi = a.astype(BF16)
    return hi, (a - hi.astype(F32)).astype(BF16)


_NN = (((1,), (0,)), ((), ()))
_NT = (((1,), (1,)), ((), ()))
_TN = (((0,), (0,)), ((), ()))


def _dot3(a, b, dims=_NN):
    (ah, al), (bh, bl) = a, b
    dg = lambda p, q: lax.dot_general(p, q, dims, preferred_element_type=F32)
    return dg(ah, bh) + dg(ah, bl) + dg(al, bh)


def _dot1(a, b, dims=_NN):
    return lax.dot_general(a.astype(BF16), b.astype(BF16), dims, preferred_element_type=F32)


def _chunk_terms_kernel(r_ref, v_ref, kk_ref, lw_ref, bb_ref, kd_ref, g_ref, h_ref, rq_ref, y0_ref):
    c = SCAN_CHUNK
    n = RWKV_N
    row = lax.broadcasted_iota(jnp.int32, (c, c), 0)
    col = lax.broadcasted_iota(jnp.int32, (c, c), 1)
    row2 = lax.broadcasted_iota(jnp.int32, (c, 2 * c), 0)
    col2 = lax.broadcasted_iota(jnp.int32, (c, 2 * c), 1) % c
    eye = row == col
    n_chunks = r_ref.shape[0] // c

    chains = []
    for ck in range(n_chunks):
        rows = slice(ck * c, (ck + 1) * c)
        r_all = r_ref[rows, :]
        v_all = v_ref[rows, :]
        kk = kk_ref[rows, :]
        for d in range(2):
            sgn = 1 if d == 0 else -1
            dsl = slice(d * RWKV_WIDTH, (d + 1) * RWKV_WIDTH)
            incl = (col - row) * sgn <= 0
            lw = lw_ref[rows, dsl]
            bb = bb_ref[rows, dsl]
            kd = kd_ref[rows, dsl]
            lp = _dot_mask_lhs(incl, lw)
            lt = jnp.sum(lw, axis=0, keepdims=True)
            p_inv = jnp.exp(-lp)
            p_end = jnp.exp(lt - lp)
            a_t = -kk * jnp.exp(lp - lw)
            b_t = bb * p_inv
            k_t = kd * p_inv
            r_t = r_all * jnp.exp(lp)
            k_e = kd * p_end
            b_e = bb * p_end
            p_tot = jnp.exp(lt)
            for h in range(RWKV_HEADS):
                sl = slice(h * n, (h + 1) * n)
                chains.append(dict(
                    strict=(col - row) * sgn < 0, incl2=(col2 - row2) * sgn <= 0,
                    ah=a_t[:, sl], rh=r_t[:, sl], vh=v_all[:, sl], bt=b_t[:, sl], kt=k_t[:, sl],
                    be=b_e[:, sl], ke=k_e[:, sl], ptot=p_tot[:, sl]))

    for ch in chains:
        ch["sc"] = _dot3(_split(jnp.concatenate([ch["ah"], ch["rh"]], axis=0)),
                         _split(jnp.concatenate([ch["bt"], ch["kt"]], axis=0)), _NT)
    for ch in chains:
        sc = ch["sc"]
        ch["nmat"] = jnp.where(ch["strict"], sc[:c, :c], 0.0).astype(BF16)
        ch["mr"] = jnp.where(ch["incl2"], sc[c:, :], 0.0).astype(BF16)
        ch["mkv"] = _dot1(jnp.where(ch["strict"], sc[:c, c:], 0.0), ch["vh"])
    for ch in chains:
        ch["z"] = jnp.concatenate([ch["ah"], ch["mkv"]], axis=1)
    for p in range(6):
        for ch in chains:
            ch["z"] = ch["z"] + _dot1(ch["nmat"], ch["z"])
        if p < 5:
            for ch in chains:
                ch["nmat"] = _dot1(ch["nmat"], ch["nmat"]).astype(BF16)
    for ch in chains:
        z = ch["z"]
        ch["ws"] = z[:, :n].astype(BF16)
        ch["uv"] = jnp.concatenate([z[:, n:], ch["vh"]], axis=0).astype(BF16)
    for ch in chains:
        mr = ch["mr"]
        ch["rq"] = ch["rh"] + _dot1(mr[:, :c], ch["ws"])
        ch["y0"] = _dot1(mr, ch["uv"])
        ch["g"] = jnp.where(eye, ch["ptot"], 0.0) + _dot1(ch["ws"], ch["be"], _TN)
        ch["h"] = _dot1(ch["uv"], jnp.concatenate([ch["be"], ch["ke"]], axis=0), _TN)
    for ck in range(n_chunks):
        rows = slice(ck * c, (ck + 1) * c)
        for d in range(2):
            first = (ck * 2 + d) * RWKV_HEADS
            part = chains[first:first + RWKV_HEADS]
            g_ref[d, rows, :] = jnp.concatenate([ch["g"] for ch in part], axis=1)
            h_ref[d, rows, :] = jnp.concatenate([ch["h"] for ch in part], axis=1)
            rq_ref[d, rows, :] = jnp.concatenate([ch["rq"] for ch in part], axis=1)
            y0_ref[d, rows, :] = jnp.concatenate([ch["y0"] for ch in part], axis=1)


def _chunk_terms(lay, r, v, kk, lw, bb, kd, chunks_per_step):
    c = SCAN_CHUNK * chunks_per_step
    row = lambda i: (i, 0)
    out = pl.BlockSpec((2, c, RWKV_WIDTH), lambda i: (0, i, 0))
    shp = jax.ShapeDtypeStruct((2, lay.n_tok, RWKV_WIDTH), F32)
    return pl.pallas_call(
        _chunk_terms_kernel,
        grid=(lay.n_tok // c,),
        in_specs=[pl.BlockSpec((c, RWKV_WIDTH), row)] * 3 + [pl.BlockSpec((c, 2 * RWKV_WIDTH), row)] * 3,
        out_specs=[out] * 4,
        out_shape=[shp] * 4,
        compiler_params=_cparams(("parallel",)),
        name="rwkv_chunk_terms",
    )(r, v, kk, lw, bb, kd)


def _scan_kernel(gf_ref, hf_ref, rqf_ref, y0f_ref, gb_ref, hb_ref, rqb_ref, y0b_ref, yf_ref, yb_ref, st_ref):
    @pl.when(pl.program_id(1) == 0)
    def _():
        st_ref[...] = jnp.zeros_like(st_ref)

    dirs = ((gf_ref, hf_ref, rqf_ref, y0f_ref, yf_ref), (gb_ref, hb_ref, rqb_ref, y0b_ref, yb_ref))
    heads = [slice(h * RWKV_N, (h + 1) * RWKV_N) for h in range(RWKV_HEADS)]
    n_sub = gf_ref.shape[0] // SCAN_CHUNK
    state = [[st_ref[d, h] for h in range(RWKV_HEADS)] for d in range(2)]
    for step in range(n_sub):
        rows = [slice(k * SCAN_CHUNK, (k + 1) * SCAN_CHUNK) for k in (step, n_sub - 1 - step)]
        s0 = [[_split(state[d][h]) for h in range(RWKV_HEADS)] for d in range(2)]
        state = [[_dot1(s0[d][h][0], dirs[d][0][rows[d], sl]) + _dot1(s0[d][h][1], dirs[d][0][rows[d], sl])
                  + dirs[d][1][rows[d], sl] for h, sl in enumerate(heads)] for d in range(2)]
        for d in range(2):
            rq_ref, y0_ref, y_ref = dirs[d][2:]
            y_ref[rows[d], :] = jnp.concatenate(
                [_dot1(rq_ref[rows[d], sl], s0[d][h][0], _NT) + y0_ref[rows[d], sl]
                 for h, sl in enumerate(heads)], axis=1)
    for d in range(2):
        for h in range(RWKV_HEADS):
            st_ref[d, h] = state[d][h]


def _scan(lay, g, h, rq, y0, chunks_per_step):
    c = SCAN_CHUNK * chunks_per_step
    nc_ctx = lay.ctx_len // c
    nc_lat = lay.seq // c
    lat_blocks = lay.n_lat // c

    def blk(b, d, ci):
        if d == 0:
            return jnp.where(ci < nc_ctx, lat_blocks + b * nc_ctx + ci, b * nc_lat + (ci - nc_ctx))
        return jnp.where(ci < nc_ctx, lat_blocks + b * nc_ctx + (nc_ctx - 1 - ci),
                         b * nc_lat + (nc_lat - 1 - (ci - nc_ctx)))

    def term(d):
        return pl.BlockSpec((None, c, RWKV_WIDTH), lambda b, ci: (d, blk(b, d, ci), 0))

    def yspec(d):
        return pl.BlockSpec((c, RWKV_WIDTH), lambda b, ci: (blk(b, d, ci), 0))

    shp = jax.ShapeDtypeStruct((lay.n_tok, RWKV_WIDTH), F32)
    return pl.pallas_call(
        _scan_kernel,
        grid=(lay.batch, nc_ctx + nc_lat),
        in_specs=[term(0)] * 4 + [term(1)] * 4,
        out_specs=[yspec(0), yspec(1)],
        out_shape=[shp, shp],
        scratch_shapes=[pltpu.VMEM((2, RWKV_HEADS, RWKV_N, RWKV_N), F32)],
        compiler_params=_cparams(("parallel", "arbitrary")),
        name="rwkv_scan",
    )(g, h, rq, y0, g, h, rq, y0)


def _outproj_kernel(att_ref, yf_ref, yb_ref, bv_ref, g_ref, co_ref, x_ref, gate_ref, w_ref,
                    gng_ref, gnb_ref, lng_ref, lnb_ref, o_ref):
    ones = _head_ones(RWKV_WIDTH, RWKV_N)
    y = yf_ref[...] + yb_ref[...]
    mu = _dot_mask_rhs(y, ones) * (1.0 / RWKV_N)
    yc = y - mu
    var = _dot_mask_rhs(yc * yc, ones) * (1.0 / RWKV_N)
    yn = yc * lax.rsqrt(var + GN_EPS) * gng_ref[...] + gnb_ref[...]
    rwkv = ((yn + bv_ref[...]) * g_ref[...]).astype(BF16)
    mix = (_dot(att_ref[...], w_ref[:ATT_WIDTH, :])
           + _dot(rwkv, w_ref[ATT_WIDTH:ATT_WIDTH + RWKV_WIDTH, :])
           + _dot(co_ref[...], w_ref[ATT_WIDTH + RWKV_WIDTH:, :]))
    z = DN_ALPHA * x_ref[...] + gate_ref[...] * mix
    o_ref[...] = _layer_norm(z, lng_ref[...], lnb_ref[...])


def _outproj(lay, layer, att, yf, yb, bv, g, co, x, mods, w_out, gn_g, gn_b, ln_g, ln_b, tm):
    row = lambda i: (i, 0)
    lmap = lambda i: (layer, 0, 0)
    return pl.pallas_call(
        _outproj_kernel,
        grid=(lay.n_tok // tm,),
        in_specs=[
            pl.BlockSpec((tm, ATT_WIDTH), row),
            pl.BlockSpec((tm, RWKV_WIDTH), row),
            pl.BlockSpec((tm, RWKV_WIDTH), row),
            pl.BlockSpec((tm, RWKV_WIDTH), row),
            pl.BlockSpec((tm, RWKV_WIDTH), row),
            pl.BlockSpec((tm, CONV_WIDTH), row),
            pl.BlockSpec((tm, D_MODEL), row),
            lay.mod_spec(layer, tm, 2),
            pl.BlockSpec((None, D_MODEL, D_MODEL), lmap),
            pl.BlockSpec((None, 1, RWKV_WIDTH), lmap),
            pl.BlockSpec((None, 1, RWKV_WIDTH), lmap),
            pl.BlockSpec((None, 1, D_MODEL), lmap),
            pl.BlockSpec((None, 1, D_MODEL), lmap),
        ],
        out_specs=pl.BlockSpec((tm, D_MODEL), row),
        out_shape=jax.ShapeDtypeStruct((lay.n_tok, D_MODEL), F32),
        compiler_params=_cparams(("parallel",)),
        name="outproj_ln1",
    )(att, yf, yb, bv, g, co, x, mods, w_out, gn_g, gn_b, ln_g, ln_b)


def _ffn_kernel(x_ref, sh_ref, sc_ref, gate_ref, w1_ref, w3_ref, w2_ref, lng_ref, lnb_ref, o_ref, *, n_split):
    x = x_ref[...]
    h = (x * (1.0 + sc_ref[...]) + sh_ref[...]).astype(BF16)
    step = D_FF // n_split
    f = jnp.zeros(x.shape, F32)
    for s in range(n_split):
        cols = slice(s * step, (s + 1) * step)
        a = _dot(h, w1_ref[:, cols])
        act = (a * jax.nn.sigmoid(a) * _dot(h, w3_ref[:, cols])).astype(BF16)
        f = f + _dot(act, w2_ref[cols, :])
    z = DN_ALPHA * x + gate_ref[...] * f
    o_ref[...] = _layer_norm(z, lng_ref[...], lnb_ref[...])


def _ffn(lay, layer, j, x, mods, w1, w3, w2, ln_g, ln_b, tm):
    row = lambda i: (i, 0)
    once = pl.Buffered(1)
    return pl.pallas_call(
        functools.partial(_ffn_kernel, n_split=2),
        grid=(lay.n_tok // tm,),
        in_specs=[
            pl.BlockSpec((tm, D_MODEL), row),
            lay.mod_spec(layer, tm, 3), lay.mod_spec(layer, tm, 4), lay.mod_spec(layer, tm, 5),
            pl.BlockSpec((None, D_MODEL, D_FF), lambda i: (j, 0, 0), pipeline_mode=once),
            pl.BlockSpec((None, D_MODEL, D_FF), lambda i: (j, 0, 0), pipeline_mode=once),
            pl.BlockSpec((None, D_FF, D_MODEL), lambda i: (j, 0, 0), pipeline_mode=once),
            pl.BlockSpec((None, 1, D_MODEL), lambda i: (layer, 0, 0)),
            pl.BlockSpec((None, 1, D_MODEL), lambda i: (layer, 0, 0)),
        ],
        out_specs=pl.BlockSpec((tm, D_MODEL), row),
        out_shape=jax.ShapeDtypeStruct((lay.n_tok, D_MODEL), F32),
        compiler_params=_cparams(("parallel",)),
        name="ffn_ln2",
    )(x, mods, mods, mods, w1, w3, w2, ln_g, ln_b)


def _router_kernel(x_ref, sh_ref, sc_ref, rw_ref, rb_ref, tri_ref, g_ref, rk_ref, rkt_ref, cnt_ref):
    h = x_ref[...] * (1.0 + sc_ref[...]) + sh_ref[...]
    logits = _dot3(_split(h), _split(rw_ref[...])) + rb_ref[...]
    lane = lax.broadcasted_iota(jnp.int32, logits.shape, 1)
    m1 = jnp.max(logits, axis=-1, keepdims=True)
    i1 = jnp.min(jnp.where(logits == m1, lane, LANES), axis=-1, keepdims=True)
    rest = jnp.where(lane == i1, -jnp.inf, logits)
    m2 = jnp.max(rest, axis=-1, keepdims=True)
    i2 = jnp.min(jnp.where(rest == m2, lane, LANES), axis=-1, keepdims=True)
    e2 = jnp.exp(m2 - m1)
    den = 1.0 + e2
    g_ref[...] = jnp.where(lane == i1, 1.0 / den, 0.0) + jnp.where(lane == i2, e2 / den, 0.0)
    sel = jnp.where(lane == i1, 1.0, 0.0) + jnp.where(lane == i2, 1.0, 0.0)
    rank = _dot(tri_ref[...], sel.astype(BF16))
    rk = jnp.where(sel > 0.0, rank, -1.0)
    rk_ref[...] = rk
    rkt_ref[...] = jnp.transpose(rk)[:N_EXPERTS, :]
    cnt = jnp.sum(sel, axis=0, keepdims=True).astype(jnp.int32)
    cnt_ref[...] = jnp.broadcast_to(cnt, cnt_ref.shape)


def _router(lay, layer, j, x, mods, router_w, router_b, tri, tm):
    n_tiles = lay.n_tok // tm
    row = lambda i: (i, 0)
    return pl.pallas_call(
        _router_kernel,
        grid=(n_tiles,),
        in_specs=[
            pl.BlockSpec((tm, D_MODEL), row),
            lay.mod_spec(layer, tm, 3), lay.mod_spec(layer, tm, 4),
            pl.BlockSpec((None, D_MODEL, LANES), lambda i: (j, 0, 0)),
            pl.BlockSpec((None, 1, LANES), lambda i: (j, 0, 0)),
            pl.BlockSpec((tm, tm), lambda i: (0, 0)),
        ],
        out_specs=[
            pl.BlockSpec((tm, LANES), row),
            pl.BlockSpec((tm, LANES), row),
            pl.BlockSpec((None, N_EXPERTS, tm), lambda i: (i, 0, 0)),
            pl.BlockSpec((None, 8, LANES), lambda i: (i, 0, 0)),
        ],
        out_shape=[
            jax.ShapeDtypeStruct((lay.n_tok, LANES), F32),
            jax.ShapeDtypeStruct((lay.n_tok, LANES), F32),
            jax.ShapeDtypeStruct((n_tiles, N_EXPERTS, tm), F32),
            jax.ShapeDtypeStruct((n_tiles, 8, LANES), jnp.int32),
        ],
        compiler_params=_cparams(("parallel",)),
        name="moe_router",
    )(x, mods, mods, router_w, router_b, tri)


def _moe_kernel(cnt_ref, x_ref, sh_ref, sc_ref, gate_ref, g_ref, rk_ref, rkt_ref, w1_ref, w3_ref, w2_ref,
                lng_ref, lnb_ref, o_ref, h_ref, acc_ref, *, blk0, blk):
    i = pl.program_id(0)
    e = pl.program_id(1)
    tm = x_ref.shape[0]

    @pl.when(e == 0)
    def _():
        h_ref[...] = (x_ref[...] * (1.0 + sc_ref[...]) + sh_ref[...]).astype(BF16)
        acc_ref[...] = jnp.zeros_like(acc_ref)

    lane = lax.broadcasted_iota(jnp.int32, (tm, LANES), 1)
    rk_col = jnp.sum(jnp.where(lane == e, rk_ref[...], 0.0), axis=-1, keepdims=True)
    rk_row = rkt_ref[pl.ds(e, 1), :]
    g_parts = _split(g_ref[...])

    def block(base, rows):
        pos_r = lax.broadcasted_iota(jnp.int32, (rows, tm), 0).astype(F32)
        pos_c = lax.broadcasted_iota(jnp.int32, (tm, rows), 1).astype(F32)
        lane_b = lax.broadcasted_iota(jnp.int32, (rows, LANES), 1)
        take = jnp.where(rk_row - base == pos_r, 1.0, 0.0).astype(BF16)
        put = jnp.where(rk_col - base == pos_c, 1.0, 0.0).astype(BF16)
        hg = _dot(take, h_ref[...]).astype(BF16)
        a = _dot(hg, w1_ref[...])
        act = (a * jax.nn.sigmoid(a) * _dot(hg, w3_ref[...])).astype(BF16)
        f = _dot(act, w2_ref[...])
        gg = _dot(take, g_parts[0]) + _dot(take, g_parts[1])
        ge = jnp.sum(jnp.where(lane_b == e, gg, 0.0), axis=-1, keepdims=True)
        acc_ref[...] += _dot(put, (ge * f).astype(BF16))

    n_routed = cnt_ref[i * N_EXPERTS + e]

    @pl.when(n_routed > 0)
    def _():
        block(jnp.float32(0.0), blk0)

    def tail(jb, carry):
        block((blk0 + jb * blk).astype(F32), blk)
        return carry

    lax.fori_loop(0, (jnp.maximum(n_routed - blk0, 0) + blk - 1) // blk, tail, 0)

    @pl.when(e == N_EXPERTS - 1)
    def _():
        z = DN_ALPHA * x_ref[...] + gate_ref[...] * acc_ref[...]
        o_ref[...] = _layer_norm(z, lng_ref[...], lnb_ref[...])


def _moe(lay, layer, j, x, mods, counts, g, rk, rkt, w1, w3, w2, ln_g, ln_b, tm, blk0, blk):
    row = lambda i, e, cnt: (i, 0)
    grid_spec = pltpu.PrefetchScalarGridSpec(
        num_scalar_prefetch=1,
        grid=(lay.n_tok // tm, N_EXPERTS),
        in_specs=[
            pl.BlockSpec((tm, D_MODEL), row),
            lay.mod_spec(layer, tm, 3), lay.mod_spec(layer, tm, 4), lay.mod_spec(layer, tm, 5),
            pl.BlockSpec((tm, LANES), row),
            pl.BlockSpec((tm, LANES), row),
            pl.BlockSpec((None, N_EXPERTS, tm), lambda i, e, cnt: (i, 0, 0)),
            pl.BlockSpec((None, None, D_MODEL, D_FF_EXPERT), lambda i, e, cnt: (j, e, 0, 0)),
            pl.BlockSpec((None, None, D_MODEL, D_FF_EXPERT), lambda i, e, cnt: (j, e, 0, 0)),
            pl.BlockSpec((None, None, D_FF_EXPERT, D_MODEL), lambda i, e, cnt: (j, e, 0, 0)),
            pl.BlockSpec((None, 1, D_MODEL), lambda i, e, cnt: (layer, 0, 0)),
            pl.BlockSpec((None, 1, D_MODEL), lambda i, e, cnt: (layer, 0, 0)),
        ],
        out_specs=pl.BlockSpec((tm, D_MODEL), row),
        scratch_shapes=[pltpu.VMEM((tm, D_MODEL), BF16), pltpu.VMEM((tm, D_MODEL), F32)],
    )
    return pl.pallas_call(
        functools.partial(_moe_kernel, blk0=blk0, blk=blk),
        grid_spec=grid_spec,
        out_shape=jax.ShapeDtypeStruct((lay.n_tok, D_MODEL), F32),
        compiler_params=_cparams(("parallel", "arbitrary")),
        name="moe_ln2",
    )(counts, x, mods, mods, mods, g, rk, rkt, w1, w3, w2, ln_g, ln_b)


def _rope_tables(ctx_len, seq):
    t = np.arange(seq)
    inv = ROPE_THETA ** (-np.arange(0, AXIS_DIM, 2, dtype=np.float64) / AXIS_DIM)
    ang_r = (t // GRID_W)[:, None].astype(np.float64) * inv
    ang_c = (t % GRID_W)[:, None].astype(np.float64) * inv
    cos = np.concatenate([np.cos(ang_r), np.cos(ang_r), np.cos(ang_c), np.cos(ang_c)], axis=1)
    sin = np.concatenate([-np.sin(ang_r), np.sin(ang_r), -np.sin(ang_c), np.sin(ang_c)], axis=1)
    cos = np.concatenate([cos, np.ones((ctx_len, ATT_QK))], axis=0)
    sin = np.concatenate([sin, np.zeros((ctx_len, ATT_QK))], axis=0)
    reps = LANES // ATT_QK
    return (jnp.asarray(np.tile(cos, (1, reps)), F32), jnp.asarray(np.tile(sin, (1, reps)), F32))


def kernel(x, c, ctx, c_ctx, w_ada, b_ada, w_in, w_out, ln1_g, ln1_b, ln2_g, ln2_b,
           lam_q1, lam_k1, lam_q2, lam_k2, subln_g, rkv_conv, decay_w0, decay_up, iclr_a0, iclr_up,
           gate_up, k_k, k_a, r_k, gn_g, gn_b, conv_w, ffn_w1, ffn_w3, ffn_w2,
           router_w, router_b, moe_w1, moe_w3, moe_w2):
    batch, seq, d = x.shape
    ctx_len = ctx.shape[1]
    depth = w_in.shape[0]
    assert d == D_MODEL and depth == DEPTH and batch < MOD_ROWS
    tm = 256
    tq = 256
    tk = 512
    tm_moe = next(t for t in (1024, 512, 256) if (batch * ctx_len) % t == 0 and seq % t == 0)
    moe_blk = 128
    moe_blk0 = (2 * tm_moe // N_EXPERTS) * 5 // 4
    tri = jnp.asarray(np.tri(tm_moe, k=-1), BF16)
    scan_chunks_per_step = 4
    assert ctx_len % tm == 0 and seq % tm == 0 and (batch * ctx_len) % tm_moe == 0 and seq % tm_moe == 0
    assert seq % ctx_len == 0 and ctx_len % SCAN_CHUNK == 0 and seq % GRID_W == 0
    lay = _Layout(batch, ctx_len, seq)

    tokens = jnp.concatenate([x.reshape(batch * seq, d), ctx.reshape(batch * ctx_len, d)], axis=0)

    c_all = jnp.concatenate([c, c_ctx[None, :], jnp.zeros((MOD_ROWS - batch - 1, d), F32)], axis=0)
    mods = _adaln(c_all, w_ada, b_ada).reshape(depth * MOD_ROWS, 1, 6 * d)

    cos, sin = _rope_tables(ctx_len, seq)

    o = np.cumsum([0, 512, 512, 512, 256, 256, 256, 64, 64, 64, 256, 256, 256])
    wq = w_in[:, :, :o[3]].astype(BF16)
    wr = jnp.concatenate([w_in[:, :, o[3]:o[9]], jnp.zeros((depth, d, RW_W - (o[9] - o[3])), F32)],
                         axis=-1).astype(BF16)
    wc = w_in[:, :, o[9]:].astype(BF16)
    w_out_b = w_out.astype(BF16)

    z = jnp.zeros((depth, DECAY_LORA, RWKV_WIDTH), F32)
    wd = jnp.concatenate([
        jnp.concatenate([decay_up[:, 0], z], axis=-1),
        jnp.concatenate([z, decay_up[:, 1]], axis=-1),
        jnp.zeros((depth, 2 * ICLR_LORA, 2 * RWKV_WIDTH), F32)], axis=1)
    wa = jnp.concatenate([
        jnp.zeros((depth, 2 * DECAY_LORA, 2 * RWKV_WIDTH), F32),
        jnp.concatenate([iclr_up[:, 0], z], axis=-1),
        jnp.concatenate([z, iclr_up[:, 1]], axis=-1)], axis=1)
    wg = jnp.concatenate([gate_up, jnp.zeros((depth, LANES - GATE_LORA, RWKV_WIDTH), F32)], axis=1)
    w0 = decay_w0.reshape(depth, 1, 2 * RWKV_WIDTH)
    a0 = iclr_a0.reshape(depth, 1, 2 * RWKV_WIDTH)
    vec = lambda w: w.reshape(depth, 1, -1)

    lam4 = jnp.stack([lam_q1, lam_k1, lam_q2, lam_k2], axis=1)

    ffn_w1_b, ffn_w3_b, ffn_w2_b = ffn_w1.astype(BF16), ffn_w3.astype(BF16), ffn_w2.astype(BF16)
    moe_w1_b, moe_w3_b, moe_w2_b = moe_w1.astype(BF16), moe_w3.astype(BF16), moe_w2.astype(BF16)
    n_moe = router_w.shape[0]
    router_w_p = jnp.concatenate([router_w, jnp.zeros((n_moe, d, LANES - N_EXPERTS), F32)], axis=-1)
    router_b_p = jnp.concatenate([router_b, jnp.full((n_moe, LANES - N_EXPERTS), -1e30, F32)],
                                 axis=-1).reshape(n_moe, 1, LANES)

    xs = tokens
    for l in range(depth):
        lam_init = 0.8 - 0.6 * math.exp(-0.3 * l)
        qkv, rw, cv = _inproj(lay, l, xs, mods, wq, wr, wc, cos, sin, tm)
        att = _attention(lay, l, qkv, lam4, vec(subln_g), lam_init, tq, tk)
        r, v, kk, lw, bb, kd, bv, g, co = _prep(lay, l, rw, cv, rkv_conv, w0, wd, a0, wa, wg,
                                                vec(k_k), vec(k_a), vec(r_k), conv_w, tm)
        yf, yb = _scan(lay, *_chunk_terms(lay, r, v, kk, lw, bb, kd, scan_chunks_per_step),
                       scan_chunks_per_step)
        xs = _outproj(lay, l, att, yf, yb, bv, g, co, xs, mods, w_out_b, vec(gn_g), vec(gn_b),
                      vec(ln1_g), vec(ln1_b), tm)
        if l % 2 == 0:
            xs = _ffn(lay, l, l // 2, xs, mods, ffn_w1_b, ffn_w3_b, ffn_w2_b, vec(ln2_g), vec(ln2_b), tm)
        else:
            g_tok, rk, rkt, cnt = _router(lay, l, l // 2, xs, mods, router_w_p, router_b_p, tri, tm_moe)
            counts = cnt[:, 0, :N_EXPERTS].reshape(-1)
            xs = _moe(lay, l, l // 2, xs, mods, counts, g_tok, rk, rkt, moe_w1_b, moe_w3_b, moe_w2_b,
                      vec(ln2_g), vec(ln2_b), tm_moe, moe_blk0, moe_blk)
    return xs[:lay.n_lat].reshape(batch, seq, d)
```

```python
import functools
import math

import numpy as np
import jax
import jax.numpy as jnp
from jax import lax
from jax.experimental import pallas as pl
from jax.experimental.pallas import tpu as pltpu

F32 = jnp.float32
BF16 = jnp.bfloat16
HI = lax.Precision.HIGHEST

D_MODEL = 1024
DEPTH = 4
GRID_W = 64
ATT_HEADS = 4
ATT_QK = 64
ATT_V = 128
ATT_WIDTH = 512
AXIS_DIM = 32
ROPE_THETA = 10000.0
SUBLN_EPS = 1e-5
RWKV_HEADS = 4
RWKV_N = 64
RWKV_WIDTH = 256
DECAY_LORA = 32
ICLR_LORA = 32
GATE_LORA = 64
GN_EPS = 64e-5
CONV_WIDTH = 256
D_FF = 2816
N_EXPERTS = 8
D_FF_EXPERT = 1408
DN_ALPHA = (2 * DEPTH) ** 0.25
LN_EPS = 1e-5

LANES = 128
MOD_ROWS = 16
QKV_W = 3 * ATT_WIDTH
RW_W = 1024
CV_W = 3 * CONV_WIDTH
SCAN_CHUNK = 64
VMEM_LIMIT = 56 * 1024 * 1024


def _cparams(sem):
    return pltpu.CompilerParams(dimension_semantics=sem, vmem_limit_bytes=VMEM_LIMIT)


def _dot(a, b):
    return jnp.dot(a, b, preferred_element_type=F32)


def _dot_hi(a, b):
    return jnp.dot(a, b, precision=HI, preferred_element_type=F32)


def _dot_nt_hi(a, b):
    return lax.dot_general(a, b, (((1,), (1,)), ((), ())), precision=HI, preferred_element_type=F32)


def _dot_tn_hi(a, b):
    return lax.dot_general(a, b, (((0,), (0,)), ((), ())), precision=HI, preferred_element_type=F32)


def _split3(a):
    hi = a.astype(BF16)
    rest = a - hi.astype(F32)
    mid = rest.astype(BF16)
    return hi, mid, (rest - mid.astype(F32)).astype(BF16)


def _dot_mask_lhs(mask, a):
    m = jnp.where(mask, 1.0, 0.0).astype(BF16)
    return sum(jnp.dot(m, part, preferred_element_type=F32) for part in _split3(a))


def _dot_mask_rhs(a, mask):
    m = mask.astype(BF16)
    return sum(jnp.dot(part, m, preferred_element_type=F32) for part in _split3(a))


def _layer_norm(z, g, b):
    mu = jnp.mean(z, axis=-1, keepdims=True)
    zc = z - mu
    var = jnp.mean(zc * zc, axis=-1, keepdims=True)
    return zc * lax.rsqrt(var + LN_EPS) * g + b


def _head_ones(width, head):
    r = lax.broadcasted_iota(jnp.int32, (width, width), 0) // head
    c = lax.broadcasted_iota(jnp.int32, (width, width), 1) // head
    return (r == c).astype(F32)


def _ada_kernel(c_ref, w_ref, b_ref, o_ref):
    c = c_ref[...]
    sc = c * jax.nn.sigmoid(c)
    o_ref[...] = _dot(sc.astype(BF16), w_ref[...].astype(BF16)) + b_ref[...]


def _adaln(c_all, w_ada, b_ada):
    depth, d, n = w_ada.shape
    tn = 1536
    return pl.pallas_call(
        _ada_kernel,
        grid=(depth, n // tn),
        in_specs=[
            pl.BlockSpec((MOD_ROWS, d), lambda l, j: (0, 0)),
            pl.BlockSpec((None, d, tn), lambda l, j: (l, 0, j)),
            pl.BlockSpec((None, 1, tn), lambda l, j: (l, 0, j)),
        ],
        out_specs=pl.BlockSpec((None, MOD_ROWS, tn), lambda l, j: (l, 0, j)),
        out_shape=jax.ShapeDtypeStruct((depth, MOD_ROWS, n), F32),
        compiler_params=_cparams(("parallel", "parallel")),
        name="adaln",
    )(c_all, w_ada, b_ada.reshape(depth, 1, n))


class _Layout:
    def __init__(self, batch, ctx_len, seq):
        self.batch, self.ctx_len, self.seq = batch, ctx_len, seq
        self.n_lat = batch * seq
        self.n_tok = self.n_lat + batch * ctx_len

    def mod_spec(self, layer, tm, which):
        n_lat_tiles = self.n_lat // tm
        rows_per = self.seq // tm
        batch = self.batch

        def imap(i, *_):
            row = jnp.where(i < n_lat_tiles, i // rows_per, batch)
            return (layer * MOD_ROWS + row, 0, which)

        return pl.BlockSpec((None, 1, D_MODEL), imap)


def _inproj_kernel(x_ref, sh_ref, sc_ref, wq_ref, wr_ref, wc_ref, cos_ref, sin_ref,
                   oq_ref, or_ref, oc_ref):
    h = (x_ref[...] * (1.0 + sc_ref[...]) + sh_ref[...]).astype(BF16)
    qkv = _dot(h, wq_ref[...])
    cos = cos_ref[...]
    sin = sin_ref[...]
    lane = lax.broadcasted_iota(jnp.int32, cos.shape, 1)
    first_half = (lane % AXIS_DIM) < (AXIS_DIM // 2)
    for s in range(2 * ATT_WIDTH // LANES):
        seg = qkv[:, s * LANES:(s + 1) * LANES]
        swapped = jnp.where(first_half, pltpu.roll(seg, LANES - AXIS_DIM // 2, 1),
                            pltpu.roll(seg, AXIS_DIM // 2, 1))
        rot = seg * cos + swapped * sin
        if s < ATT_WIDTH // LANES:
            rot = rot * (ATT_QK ** -0.5 * math.log2(math.e))
        oq_ref[:, s * LANES:(s + 1) * LANES] = rot.astype(BF16)
    oq_ref[:, 2 * ATT_WIDTH:] = qkv[:, 2 * ATT_WIDTH:].astype(BF16)
    or_ref[...] = _dot(h, wr_ref[...])
    oc_ref[...] = _dot(h, wc_ref[...])


def _inproj(lay, layer, x, mods, wq, wr, wc, cos, sin, tm):
    n_lat_tiles = lay.n_lat // tm
    ctx_tiles_per = lay.ctx_len // tm
    lat_tiles_per = lay.seq // tm

    def rope_map(i):
        return (jnp.where(i < n_lat_tiles, i % lat_tiles_per,
                          lat_tiles_per + (i - n_lat_tiles) % ctx_tiles_per), 0)

    row = lambda i: (i, 0)
    wmap = lambda i: (layer, 0, 0)
    return pl.pallas_call(
        _inproj_kernel,
        grid=(lay.n_tok // tm,),
        in_specs=[
            pl.BlockSpec((tm, D_MODEL), row),
            lay.mod_spec(layer, tm, 0),
            lay.mod_spec(layer, tm, 1),
            pl.BlockSpec((None, D_MODEL, QKV_W), wmap),
            pl.BlockSpec((None, D_MODEL, RW_W), wmap),
            pl.BlockSpec((None, D_MODEL, CV_W), wmap),
            pl.BlockSpec((tm, LANES), rope_map),
            pl.BlockSpec((tm, LANES), rope_map),
        ],
        out_specs=[
            pl.BlockSpec((tm, QKV_W), row),
            pl.BlockSpec((tm, RW_W), row),
            pl.BlockSpec((tm, CV_W), row),
        ],
        out_shape=[
            jax.ShapeDtypeStruct((lay.n_tok, QKV_W), BF16),
            jax.ShapeDtypeStruct((lay.n_tok, RW_W), F32),
            jax.ShapeDtypeStruct((lay.n_tok, CV_W), F32),
        ],
        compiler_params=_cparams(("parallel",)),
        name="inproj",
    )(x, mods, mods, wq, wr, wc, cos, sin)


def _attn_kernel(q_ref, kc_ref, vc_ref, kl_ref, vl_ref, lam_ref, g_ref, o_ref, vxc_ref, vxl_ref,
                 *, lam_init, n_lat_q, tq, tk):
    @pl.when(pl.program_id(2) == 0)
    def _():
        vxc_ref[:, :ATT_V] = vc_ref[...]
        vxc_ref[:, ATT_V:] = jnp.ones((vxc_ref.shape[0], LANES), BF16)
        vxl_ref[:, :ATT_V] = vl_ref[...]
        vxl_ref[:, ATT_V:] = jnp.ones((vxl_ref.shape[0], LANES), BF16)

    q = q_ref[...]
    lane = lax.broadcasted_iota(jnp.int32, q.shape, 1)
    zero = jnp.zeros_like(q)
    qq = jnp.concatenate([jnp.where(lane < ATT_QK, q, zero), jnp.where(lane >= ATT_QK, q, zero)], axis=0)
    lam4 = lam_ref[...]
    lam = (jnp.exp(jnp.sum(lam4[0:1] * lam4[1:2], axis=-1, keepdims=True))
           - jnp.exp(jnp.sum(lam4[2:3] * lam4[3:4], axis=-1, keepdims=True)) + lam_init)

    def scores(blk):
        k_ref, _, start, size = blk
        return lax.dot_general(qq, k_ref[start:start + size, :], _NT, preferred_element_type=F32)

    def attend(blocks):
        s = scores(blocks[0])
        m = acc = None
        for j, (_, vx_ref, start, size) in enumerate(blocks):
            s_next = scores(blocks[j + 1]) if j + 1 < len(blocks) else None
            bm = jnp.max(s, axis=-1, keepdims=True)
            m_new = bm if m is None else jnp.maximum(m, bm)
            p = jnp.exp2(s - m_new).astype(BF16)
            pv = _dot(p, vx_ref[start:start + size, :])
            acc = pv if m is None else acc * jnp.exp2(m - m_new) + pv
            m, s = m_new, s_next
        o = acc[:, :ATT_V] / acc[:, ATT_V:]
        dlt = o[:tq] - lam * o[tq:]
        ms = jnp.mean(dlt * dlt, axis=-1, keepdims=True)
        o_ref[...] = (dlt * lax.rsqrt(ms + SUBLN_EPS) * g_ref[...] * (1.0 - lam_init)).astype(o_ref.dtype)

    ctx_blocks = [(kc_ref, vxc_ref, 0, kc_ref.shape[0])]
    lat_blocks = [(kl_ref, vxl_ref, s0, tk) for s0 in range(0, kl_ref.shape[0], tk)]
    is_ctx = pl.program_id(2) >= n_lat_q

    @pl.when(is_ctx)
    def _():
        attend(ctx_blocks)

    @pl.when(jnp.logical_not(is_ctx))
    def _():
        attend(ctx_blocks + lat_blocks)


def _attention(lay, layer, qkv, lam4, subln_g, lam_init, tq, tk):
    n_ctx_q = lay.ctx_len // tq
    n_lat_q = lay.seq // tq
    n_lat_tiles = lay.n_lat // tq
    ctx_blk0 = lay.n_lat // lay.ctx_len
    hq = ATT_WIDTH // LANES

    def qmap(b, h, qi):
        return (jnp.where(qi < n_lat_q, b * n_lat_q + qi, n_lat_tiles + b * n_ctx_q + (qi - n_lat_q)), h)

    kern = functools.partial(_attn_kernel, lam_init=lam_init, n_lat_q=n_lat_q, tq=tq, tk=tk)
    return pl.pallas_call(
        kern,
        grid=(lay.batch, ATT_HEADS, n_lat_q + n_ctx_q),
        in_specs=[
            pl.BlockSpec((tq, LANES), qmap),
            pl.BlockSpec((lay.ctx_len, LANES), lambda b, h, qi: (ctx_blk0 + b, hq + h)),
            pl.BlockSpec((lay.ctx_len, LANES), lambda b, h, qi: (ctx_blk0 + b, 2 * hq + h)),
            pl.BlockSpec((lay.seq, LANES), lambda b, h, qi: (b, hq + h)),
            pl.BlockSpec((lay.seq, LANES), lambda b, h, qi: (b, 2 * hq + h)),
            pl.BlockSpec((None, 4, ATT_QK), lambda b, h, qi: (layer, 0, 0)),
            pl.BlockSpec((None, 1, ATT_V), lambda b, h, qi: (layer, 0, 0)),
        ],
        out_specs=pl.BlockSpec((tq, LANES), qmap),
        out_shape=jax.ShapeDtypeStruct((lay.n_tok, ATT_WIDTH), BF16),
        scratch_shapes=[pltpu.VMEM((lay.ctx_len, ATT_V + LANES), BF16),
                        pltpu.VMEM((lay.seq, ATT_V + LANES), BF16)],
        compiler_params=_cparams(("parallel", "parallel", "arbitrary")),
        name="diff_attn",
    )(qkv, qkv, qkv, qkv, qkv, lam4, subln_g)


def _prep_kernel(rw_ref, rwp_ref, rwn_ref, cv_ref, cvp_ref, cvn_ref,
                 rkvw_ref, w0_ref, wd_ref, a0_ref, wa_ref, wg_ref, kk_w_ref, ka_ref, rk_ref, cw_ref,
                 r_ref, v_ref, kk_ref, lw_ref, bb_ref, kd_ref, bv_ref, g_ref, co_ref,
                 *, tm, n_lat, ctx_len, seq):
    i = pl.program_id(0)
    start = i * tm
    in_lat = start < n_lat
    seg_pos = jnp.where(in_lat, start % seq, (start - n_lat) % ctx_len)
    seg_len = jnp.where(in_lat, seq, ctx_len)
    has_prev = seg_pos != 0
    has_next = seg_pos + tm != seg_len

    def shifted(u, prev_row, next_row):
        rows = lax.broadcasted_iota(jnp.int32, u.shape, 0)
        prev_row = jnp.where(has_prev, prev_row, jnp.zeros_like(prev_row))
        next_row = jnp.where(has_next, next_row, jnp.zeros_like(next_row))
        up = jnp.where(rows == 0, prev_row, pltpu.roll(u, 1, 0))
        un = jnp.where(rows == tm - 1, next_row, pltpu.roll(u, tm - 1, 0))
        return up, un

    def conv3(u, prev_row, next_row, w):
        up, un = shifted(u, prev_row, next_row)
        return up * w[0:1] + u * w[1:2] + un * w[2:3]

    rw = rw_ref[...]
    nrkv = 3 * RWKV_WIDTH
    rkv = conv3(rw[:, :nrkv], rwp_ref[7:8, :nrkv], rwn_ref[0:1, :nrkv], rkvw_ref[...])
    r = rkv[:, :RWKV_WIDTH]
    k = rkv[:, RWKV_WIDTH:2 * RWKV_WIDTH]
    v = rkv[:, 2 * RWKV_WIDTH:]

    lora = rw[:, nrkv:nrkv + LANES]
    gate = rw[:, nrkv + LANES:]
    wl = _dot3(_split(jnp.tanh(lora)), _split(wd_ref[...])) + w0_ref[...]
    lw = (-math.exp(-0.5)) * jax.nn.sigmoid(wl)
    a = jax.nn.sigmoid(_dot3(_split(lora), _split(wa_ref[...])) + a0_ref[...])
    g = _dot3(_split(jax.nn.sigmoid(gate)), _split(wg_ref[...]))

    ones = _head_ones(RWKV_WIDTH, RWKV_N)
    kraw = k * kk_w_ref[...]
    ss = _dot_mask_rhs(kraw * kraw, ones)
    kk = kraw * lax.rsqrt(jnp.maximum(ss, 1e-24))
    k2 = jnp.concatenate([k, k], axis=1)
    ka2 = jnp.concatenate([ka_ref[...], ka_ref[...]], axis=1)
    kd = k2 * (1.0 + (a - 1.0) * ka2)
    bb = jnp.concatenate([kk, kk], axis=1) * a
    bonus = _dot_mask_rhs(r * (kd[:, :RWKV_WIDTH] + kd[:, RWKV_WIDTH:]) * rk_ref[...], ones)

    r_ref[...] = r
    v_ref[...] = v
    kk_ref[...] = kk
    lw_ref[...] = lw
    bb_ref[...] = bb
    kd_ref[...] = kd
    bv_ref[...] = bonus * v
    g_ref[...] = g

    def gated(ref):
        return ref[:, 2 * CONV_WIDTH:] * ref[:, :CONV_WIDTH]

    cv_u = gated(cv_ref)
    conv = conv3(cv_u, gated(cvp_ref)[7:8], gated(cvn_ref)[0:1], cw_ref[...])
    co_ref[...] = (cv_ref[:, CONV_WIDTH:2 * CONV_WIDTH] * conv).astype(co_ref.dtype)


def _split(a):
    hi = a.astype(BF16)
    return hi, (a - hi.astype(F32)).astype(BF16)


_NN = (((1,), (0,)), ((), ()))
_NT = (((1,), (1,)), ((), ()))
_TN = (((0,), (0,)), ((), ()))


def _dot3(a, b, dims=_NN):
    (ah, al), (bh, bl) = a, b
    dg = lambda p, q: lax.dot_general(p, q, dims, preferred_element_type=F32)
    return dg(ah, bh) + dg(ah, bl) + dg(al, bh)


def _dot1(a, b, dims=_NN):
    return lax.dot_general(a.astype(BF16), b.astype(BF16), dims, preferred_element_type=F32)


def _chunk_terms_kernel(r_ref, v_ref, kk_ref, lw_ref, bb_ref, kd_ref, g_ref, h_ref, rq_ref, y0_ref):
    c = SCAN_CHUNK
    n = RWKV_N
    row = lax.broadcasted_iota(jnp.int32, (c, c), 0)
    col = lax.broadcasted_iota(jnp.int32, (c, c), 1)
    row2 = lax.broadcasted_iota(jnp.int32, (c, 2 * c), 0)
    col2 = lax.broadcasted_iota(jnp.int32, (c, 2 * c), 1) % c
    eye = row == col
    n_chunks = r_ref.shape[0] // c

    chains = []
    for ck in range(n_chunks):
        rows = slice(ck * c, (ck + 1) * c)
        r_all = r_ref[rows, :]
        v_all = v_ref[rows, :]
        kk = kk_ref[rows, :]
        for d in range(2):
            sgn = 1 if d == 0 else -1
            dsl = slice(d * RWKV_WIDTH, (d + 1) * RWKV_WIDTH)
            incl = (col - row) * sgn <= 0
            lw = lw_ref[rows, dsl]
            bb = bb_ref[rows, dsl]
            kd = kd_ref[rows, dsl]
            lp = _dot_mask_lhs(incl, lw)
            lt = jnp.sum(lw, axis=0, keepdims=True)
            p_inv = jnp.exp(-lp)
            p_end = jnp.exp(lt - lp)
            a_t = -kk * jnp.exp(lp - lw)
            b_t = bb * p_inv
            k_t = kd * p_inv
            r_t = r_all * jnp.exp(lp)
            k_e = kd * p_end
            b_e = bb * p_end
            p_tot = jnp.exp(lt)
            for h in range(RWKV_HEADS):
                sl = slice(h * n, (h + 1) * n)
                chains.append(dict(
                    strict=(col - row) * sgn < 0, incl2=(col2 - row2) * sgn <= 0,
                    ah=a_t[:, sl], rh=r_t[:, sl], vh=v_all[:, sl], bt=b_t[:, sl], kt=k_t[:, sl],
                    be=b_e[:, sl], ke=k_e[:, sl], ptot=p_tot[:, sl]))

    for ch in chains:
        ch["sc"] = _dot3(_split(jnp.concatenate([ch["ah"], ch["rh"]], axis=0)),
                         _split(jnp.concatenate([ch["bt"], ch["kt"]], axis=0)), _NT)
    for ch in chains:
        sc = ch["sc"]
        ch["nmat"] = jnp.where(ch["strict"], sc[:c, :c], 0.0).astype(BF16)
        ch["mr"] = jnp.where(ch["incl2"], sc[c:, :], 0.0).astype(BF16)
        ch["mkv"] = _dot1(jnp.where(ch["strict"], sc[:c, c:], 0.0), ch["vh"])
    for ch in chains:
        ch["z"] = jnp.concatenate([ch["ah"], ch["mkv"]], axis=1)
    for p in range(6):
        for ch in chains:
            ch["z"] = ch["z"] + _dot1(ch["nmat"], ch["z"])
        if p < 5:
            for ch in chains:
                ch["nmat"] = _dot1(ch["nmat"], ch["nmat"]).astype(BF16)
    for ch in chains:
        z = ch["z"]
        ch["ws"] = z[:, :n].astype(BF16)
        ch["uv"] = jnp.concatenate([z[:, n:], ch["vh"]], axis=0).astype(BF16)
    for ch in chains:
        mr = ch["mr"]
        ch["rq"] = ch["rh"] + _dot1(mr[:, :c], ch["ws"])
        ch["y0"] = _dot1(mr, ch["uv"])
        ch["g"] = jnp.where(eye, ch["ptot"], 0.0) + _dot1(ch["ws"], ch["be"], _TN)
        ch["h"] = _dot1(ch["uv"], jnp.concatenate([ch["be"], ch["ke"]], axis=0), _TN)
    for ck in range(n_chunks):
        rows = slice(ck * c, (ck + 1) * c)
        for d in range(2):
            first = (ck * 2 + d) * RWKV_HEADS
            part = chains[first:first + RWKV_HEADS]
            g_ref[d, rows, :] = jnp.concatenate([ch["g"] for ch in part], axis=1)
            h_ref[d, rows, :] = jnp.concatenate([ch["h"] for ch in part], axis=1)
            rq_ref[d, rows, :] = jnp.concatenate([ch["rq"] for ch in part], axis=1)
            y0_ref[d, rows, :] = jnp.concatenate([ch["y0"] for ch in part], axis=1)


def _prep_chunk_kernel(*refs, tm, n_lat, ctx_len, seq):
    ins, (bv_ref, g_ref, co_ref), terms, scratch = refs[:16], refs[16:19], refs[19:23], refs[23:]
    _prep_kernel(*ins, *scratch, bv_ref, g_ref, co_ref, tm=tm, n_lat=n_lat, ctx_len=ctx_len, seq=seq)
    _chunk_terms_kernel(*scratch, *terms)


def _prep_chunk_terms(lay, layer, rw, cv, rkv_conv, w0, wd, a0, wa, wg, k_k, k_a, r_k, conv_w, tm):
    n8 = lay.n_tok // 8
    t8 = tm // 8
    row = lambda i: (i, 0)
    prev = lambda i: (jnp.maximum(i * t8 - 1, 0), 0)
    nxt = lambda i: (jnp.minimum((i + 1) * t8, n8 - 1), 0)
    lmap = lambda i: (layer, 0, 0)
    w2 = 2 * RWKV_WIDTH
    kern = functools.partial(_prep_chunk_kernel, tm=tm, n_lat=lay.n_lat, ctx_len=lay.ctx_len, seq=lay.seq)
    f32 = lambda w: jax.ShapeDtypeStruct((lay.n_tok, w), F32)
    term = pl.BlockSpec((2, tm, RWKV_WIDTH), lambda i: (0, i, 0))
    term_shape = jax.ShapeDtypeStruct((2, lay.n_tok, RWKV_WIDTH), F32)
    return pl.pallas_call(
        kern,
        grid=(lay.n_tok // tm,),
        in_specs=[
            pl.BlockSpec((tm, RW_W), row), pl.BlockSpec((8, RW_W), prev), pl.BlockSpec((8, RW_W), nxt),
            pl.BlockSpec((tm, CV_W), row), pl.BlockSpec((8, CV_W), prev), pl.BlockSpec((8, CV_W), nxt),
            pl.BlockSpec((None, 3, 3 * RWKV_WIDTH), lmap),
            pl.BlockSpec((None, 1, w2), lmap),
            pl.BlockSpec((None, LANES, w2), lmap),
            pl.BlockSpec((None, 1, w2), lmap),
            pl.BlockSpec((None, LANES, w2), lmap),
            pl.BlockSpec((None, LANES, RWKV_WIDTH), lmap),
            pl.BlockSpec((None, 1, RWKV_WIDTH), lmap),
            pl.BlockSpec((None, 1, RWKV_WIDTH), lmap),
            pl.BlockSpec((None, 1, RWKV_WIDTH), lmap),
            pl.BlockSpec((None, 3, CONV_WIDTH), lmap),
        ],
        out_specs=[pl.BlockSpec((tm, RWKV_WIDTH), row), pl.BlockSpec((tm, RWKV_WIDTH), row),
                   pl.BlockSpec((tm, CONV_WIDTH), row), term, term, term, term],
        out_shape=[f32(RWKV_WIDTH), f32(RWKV_WIDTH), jax.ShapeDtypeStruct((lay.n_tok, CONV_WIDTH), BF16),
                   term_shape, term_shape, term_shape, term_shape],
        scratch_shapes=[pltpu.VMEM((tm, RWKV_WIDTH), F32)] * 3 + [pltpu.VMEM((tm, w2), F32)] * 3,
        compiler_params=_cparams(("parallel",)),
        name="rwkv_prep_chunk_terms",
    )(rw, rw, rw, cv, cv, cv, rkv_conv, w0, wd, a0, wa, wg, k_k, k_a, r_k, conv_w)


def _scan_kernel(gf_ref, hf_ref, rqf_ref, y0f_ref, gb_ref, hb_ref, rqb_ref, y0b_ref, yf_ref, yb_ref, st_ref):
    @pl.when(pl.program_id(1) == 0)
    def _():
        st_ref[...] = jnp.zeros_like(st_ref)

    dirs = ((gf_ref, hf_ref, rqf_ref, y0f_ref, yf_ref), (gb_ref, hb_ref, rqb_ref, y0b_ref, yb_ref))
    heads = [slice(h * RWKV_N, (h + 1) * RWKV_N) for h in range(RWKV_HEADS)]
    n_sub = gf_ref.shape[0] // SCAN_CHUNK
    state = [[st_ref[d, h] for h in range(RWKV_HEADS)] for d in range(2)]
    for step in range(n_sub):
        rows = [slice(k * SCAN_CHUNK, (k + 1) * SCAN_CHUNK) for k in (step, n_sub - 1 - step)]
        s0 = [[_split(state[d][h]) for h in range(RWKV_HEADS)] for d in range(2)]
        state = [[_dot1(s0[d][h][0], dirs[d][0][rows[d], sl]) + _dot1(s0[d][h][1], dirs[d][0][rows[d], sl])
                  + dirs[d][1][rows[d], sl] for h, sl in enumerate(heads)] for d in range(2)]
        for d in range(2):
            rq_ref, y0_ref, y_ref = dirs[d][2:]
            y_ref[rows[d], :] = jnp.concatenate(
                [_dot1(rq_ref[rows[d], sl], s0[d][h][0], _NT) + y0_ref[rows[d], sl]
                 for h, sl in enumerate(heads)], axis=1)
    for d in range(2):
        for h in range(RWKV_HEADS):
            st_ref[d, h] = state[d][h]


def _scan(lay, g, h, rq, y0, chunks_per_step):
    c = SCAN_CHUNK * chunks_per_step
    nc_ctx = lay.ctx_len // c
    nc_lat = lay.seq // c
    lat_blocks = lay.n_lat // c

    def blk(b, d, ci):
        if d == 0:
            return jnp.where(ci < nc_ctx, lat_blocks + b * nc_ctx + ci, b * nc_lat + (ci - nc_ctx))
        return jnp.where(ci < nc_ctx, lat_blocks + b * nc_ctx + (nc_ctx - 1 - ci),
                         b * nc_lat + (nc_lat - 1 - (ci - nc_ctx)))

    def term(d):
        return pl.BlockSpec((None, c, RWKV_WIDTH), lambda b, ci: (d, blk(b, d, ci), 0))

    def yspec(d):
        return pl.BlockSpec((c, RWKV_WIDTH), lambda b, ci: (blk(b, d, ci), 0))

    shp = jax.ShapeDtypeStruct((lay.n_tok, RWKV_WIDTH), F32)
    return pl.pallas_call(
        _scan_kernel,
        grid=(lay.batch, nc_ctx + nc_lat),
        in_specs=[term(0)] * 4 + [term(1)] * 4,
        out_specs=[yspec(0), yspec(1)],
        out_shape=[shp, shp],
        scratch_shapes=[pltpu.VMEM((2, RWKV_HEADS, RWKV_N, RWKV_N), F32)],
        compiler_params=_cparams(("parallel", "arbitrary")),
        name="rwkv_scan",
    )(g, h, rq, y0, g, h, rq, y0)


def _outproj_kernel(att_ref, yf_ref, yb_ref, bv_ref, g_ref, co_ref, x_ref, gate_ref, w_ref,
                    gng_ref, gnb_ref, lng_ref, lnb_ref, o_ref):
    ones = _head_ones(RWKV_WIDTH, RWKV_N)
    y = yf_ref[...] + yb_ref[...]
    mu = _dot_mask_rhs(y, ones) * (1.0 / RWKV_N)
    yc = y - mu
    var = _dot_mask_rhs(yc * yc, ones) * (1.0 / RWKV_N)
    yn = yc * lax.rsqrt(var + GN_EPS) * gng_ref[...] + gnb_ref[...]
    rwkv = ((yn + bv_ref[...]) * g_ref[...]).astype(BF16)
    mix = (_dot(att_ref[...], w_ref[:ATT_WIDTH, :])
           + _dot(rwkv, w_ref[ATT_WIDTH:ATT_WIDTH + RWKV_WIDTH, :])
           + _dot(co_ref[...], w_ref[ATT_WIDTH + RWKV_WIDTH:, :]))
    z = DN_ALPHA * x_ref[...] + gate_ref[...] * mix
    o_ref[...] = _layer_norm(z, lng_ref[...], lnb_ref[...])


def _outproj(lay, layer, att, yf, yb, bv, g, co, x, mods, w_out, gn_g, gn_b, ln_g, ln_b, tm, n_rows):
    row = lambda i: (i, 0)
    lmap = lambda i: (layer, 0, 0)
    return pl.pallas_call(
        _outproj_kernel,
        grid=(n_rows // tm,),
        in_specs=[
            pl.BlockSpec((tm, ATT_WIDTH), row),
            pl.BlockSpec((tm, RWKV_WIDTH), row),
            pl.BlockSpec((tm, RWKV_WIDTH), row),
            pl.BlockSpec((tm, RWKV_WIDTH), row),
            pl.BlockSpec((tm, RWKV_WIDTH), row),
            pl.BlockSpec((tm, CONV_WIDTH), row),
            pl.BlockSpec((tm, D_MODEL), row),
            lay.mod_spec(layer, tm, 2),
            pl.BlockSpec((None, D_MODEL, D_MODEL), lmap),
            pl.BlockSpec((None, 1, RWKV_WIDTH), lmap),
            pl.BlockSpec((None, 1, RWKV_WIDTH), lmap),
            pl.BlockSpec((None, 1, D_MODEL), lmap),
            pl.BlockSpec((None, 1, D_MODEL), lmap),
        ],
        out_specs=pl.BlockSpec((tm, D_MODEL), row),
        out_shape=jax.ShapeDtypeStruct((n_rows, D_MODEL), F32),
        compiler_params=_cparams(("parallel",)),
        name="outproj_ln1",
    )(att, yf, yb, bv, g, co, x, mods, w_out, gn_g, gn_b, ln_g, ln_b)


def _ffn_kernel(x_ref, sh_ref, sc_ref, gate_ref, w1_ref, w3_ref, w2_ref, lng_ref, lnb_ref, o_ref, *, n_split):
    x = x_ref[...]
    h = (x * (1.0 + sc_ref[...]) + sh_ref[...]).astype(BF16)
    step = D_FF // n_split
    f = jnp.zeros(x.shape, F32)
    for s in range(n_split):
        cols = slice(s * step, (s + 1) * step)
        a = _dot(h, w1_ref[:, cols])
        act = (a * jax.nn.sigmoid(a) * _dot(h, w3_ref[:, cols])).astype(BF16)
        f = f + _dot(act, w2_ref[cols, :])
    z = DN_ALPHA * x + gate_ref[...] * f
    o_ref[...] = _layer_norm(z, lng_ref[...], lnb_ref[...])


def _ffn(lay, layer, j, x, mods, w1, w3, w2, ln_g, ln_b, tm):
    row = lambda i: (i, 0)
    once = pl.Buffered(1)
    return pl.pallas_call(
        functools.partial(_ffn_kernel, n_split=2),
        grid=(lay.n_tok // tm,),
        in_specs=[
            pl.BlockSpec((tm, D_MODEL), row),
            lay.mod_spec(layer, tm, 3), lay.mod_spec(layer, tm, 4), lay.mod_spec(layer, tm, 5),
            pl.BlockSpec((None, D_MODEL, D_FF), lambda i: (j, 0, 0), pipeline_mode=once),
            pl.BlockSpec((None, D_MODEL, D_FF), lambda i: (j, 0, 0), pipeline_mode=once),
            pl.BlockSpec((None, D_FF, D_MODEL), lambda i: (j, 0, 0), pipeline_mode=once),
            pl.BlockSpec((None, 1, D_MODEL), lambda i: (layer, 0, 0)),
            pl.BlockSpec((None, 1, D_MODEL), lambda i: (layer, 0, 0)),
        ],
        out_specs=pl.BlockSpec((tm, D_MODEL), row),
        out_shape=jax.ShapeDtypeStruct((lay.n_tok, D_MODEL), F32),
        compiler_params=_cparams(("parallel",)),
        name="ffn_ln2",
    )(x, mods, mods, mods, w1, w3, w2, ln_g, ln_b)


def _router_kernel(x_ref, sh_ref, sc_ref, rw_ref, rb_ref, tri_ref, g_ref, rk_ref, rkt_ref, cnt_ref):
    h = x_ref[...] * (1.0 + sc_ref[...]) + sh_ref[...]
    logits = _dot3(_split(h), _split(rw_ref[...])) + rb_ref[...]
    lane = lax.broadcasted_iota(jnp.int32, logits.shape, 1)
    m1 = jnp.max(logits, axis=-1, keepdims=True)
    i1 = jnp.min(jnp.where(logits == m1, lane, LANES), axis=-1, keepdims=True)
    rest = jnp.where(lane == i1, -jnp.inf, logits)
    m2 = jnp.max(rest, axis=-1, keepdims=True)
    i2 = jnp.min(jnp.where(rest == m2, lane, LANES), axis=-1, keepdims=True)
    e2 = jnp.exp(m2 - m1)
    den = 1.0 + e2
    g_ref[...] = jnp.where(lane == i1, 1.0 / den, 0.0) + jnp.where(lane == i2, e2 / den, 0.0)
    sel = jnp.where(lane == i1, 1.0, 0.0) + jnp.where(lane == i2, 1.0, 0.0)
    rank = _dot(tri_ref[...], sel.astype(BF16))
    rk = jnp.where(sel > 0.0, rank, -1.0)
    rk_ref[...] = rk
    rkt_ref[...] = jnp.transpose(rk)[:N_EXPERTS, :]
    cnt = jnp.sum(sel, axis=0, keepdims=True).astype(jnp.int32)
    cnt_ref[...] = jnp.broadcast_to(cnt, cnt_ref.shape)


def _router(lay, layer, j, x, mods, router_w, router_b, tri, tm, n_rows):
    n_tiles = n_rows // tm
    row = lambda i: (i, 0)
    return pl.pallas_call(
        _router_kernel,
        grid=(n_tiles,),
        in_specs=[
            pl.BlockSpec((tm, D_MODEL), row),
            lay.mod_spec(layer, tm, 3), lay.mod_spec(layer, tm, 4),
            pl.BlockSpec((None, D_MODEL, LANES), lambda i: (j, 0, 0)),
            pl.BlockSpec((None, 1, LANES), lambda i: (j, 0, 0)),
            pl.BlockSpec((tm, tm), lambda i: (0, 0)),
        ],
        out_specs=[
            pl.BlockSpec((tm, LANES), row),
            pl.BlockSpec((tm, LANES), row),
            pl.BlockSpec((None, N_EXPERTS, tm), lambda i: (i, 0, 0)),
            pl.BlockSpec((None, 8, LANES), lambda i: (i, 0, 0)),
        ],
        out_shape=[
            jax.ShapeDtypeStruct((n_rows, LANES), F32),
            jax.ShapeDtypeStruct((n_rows, LANES), F32),
            jax.ShapeDtypeStruct((n_tiles, N_EXPERTS, tm), F32),
            jax.ShapeDtypeStruct((n_tiles, 8, LANES), jnp.int32),
        ],
        compiler_params=_cparams(("parallel",)),
        name="moe_router",
    )(x, mods, mods, router_w, router_b, tri)


def _moe_kernel(cnt_ref, x_ref, sh_ref, sc_ref, gate_ref, g_ref, rk_ref, rkt_ref, w1_ref, w3_ref, w2_ref,
                lng_ref, lnb_ref, o_ref, h_ref, acc_ref, *, blk0, blk):
    i = pl.program_id(0)
    e = pl.program_id(1)
    tm = x_ref.shape[0]

    @pl.when(e == 0)
    def _():
        h_ref[...] = (x_ref[...] * (1.0 + sc_ref[...]) + sh_ref[...]).astype(BF16)
        acc_ref[...] = jnp.zeros_like(acc_ref)

    lane = lax.broadcasted_iota(jnp.int32, (tm, LANES), 1)
    rk_col = jnp.sum(jnp.where(lane == e, rk_ref[...], 0.0), axis=-1, keepdims=True)
    rk_row = rkt_ref[pl.ds(e, 1), :]
    g_parts = _split(g_ref[...])

    def block(base, rows):
        pos_r = lax.broadcasted_iota(jnp.int32, (rows, tm), 0).astype(F32)
        pos_c = lax.broadcasted_iota(jnp.int32, (tm, rows), 1).astype(F32)
        lane_b = lax.broadcasted_iota(jnp.int32, (rows, LANES), 1)
        take = jnp.where(rk_row - base == pos_r, 1.0, 0.0).astype(BF16)
        put = jnp.where(rk_col - base == pos_c, 1.0, 0.0).astype(BF16)
        hg = _dot(take, h_ref[...]).astype(BF16)
        a = _dot(hg, w1_ref[...])
        act = (a * jax.nn.sigmoid(a) * _dot(hg, w3_ref[...])).astype(BF16)
        f = _dot(act, w2_ref[...])
        gg = _dot(take, g_parts[0]) + _dot(take, g_parts[1])
        ge = jnp.sum(jnp.where(lane_b == e, gg, 0.0), axis=-1, keepdims=True)
        acc_ref[...] += _dot(put, (ge * f).astype(BF16))

    n_routed = cnt_ref[i * N_EXPERTS + e]

    @pl.when(n_routed > 0)
    def _():
        block(jnp.float32(0.0), blk0)

    def tail(jb, carry):
        block((blk0 + jb * blk).astype(F32), blk)
        return carry

    lax.fori_loop(0, (jnp.maximum(n_routed - blk0, 0) + blk - 1) // blk, tail, 0)

    @pl.when(e == N_EXPERTS - 1)
    def _():
        z = DN_ALPHA * x_ref[...] + gate_ref[...] * acc_ref[...]
        o_ref[...] = _layer_norm(z, lng_ref[...], lnb_ref[...])


def _moe(lay, layer, j, x, mods, counts, g, rk, rkt, w1, w3, w2, ln_g, ln_b, tm, blk0, blk, n_rows):
    row = lambda i, e, cnt: (i, 0)
    grid_spec = pltpu.PrefetchScalarGridSpec(
        num_scalar_prefetch=1,
        grid=(n_rows // tm, N_EXPERTS),
        in_specs=[
            pl.BlockSpec((tm, D_MODEL), row),
            lay.mod_spec(layer, tm, 3), lay.mod_spec(layer, tm, 4), lay.mod_spec(layer, tm, 5),
            pl.BlockSpec((tm, LANES), row),
            pl.BlockSpec((tm, LANES), row),
            pl.BlockSpec((None, N_EXPERTS, tm), lambda i, e, cnt: (i, 0, 0)),
            pl.BlockSpec((None, None, D_MODEL, D_FF_EXPERT), lambda i, e, cnt: (j, e, 0, 0)),
            pl.BlockSpec((None, None, D_MODEL, D_FF_EXPERT), lambda i, e, cnt: (j, e, 0, 0)),
            pl.BlockSpec((None, None, D_FF_EXPERT, D_MODEL), lambda i, e, cnt: (j, e, 0, 0)),
            pl.BlockSpec((None, 1, D_MODEL), lambda i, e, cnt: (layer, 0, 0)),
            pl.BlockSpec((None, 1, D_MODEL), lambda i, e, cnt: (layer, 0, 0)),
        ],
        out_specs=pl.BlockSpec((tm, D_MODEL), row),
        scratch_shapes=[pltpu.VMEM((tm, D_MODEL), BF16), pltpu.VMEM((tm, D_MODEL), F32)],
    )
    return pl.pallas_call(
        functools.partial(_moe_kernel, blk0=blk0, blk=blk),
        grid_spec=grid_spec,
        out_shape=jax.ShapeDtypeStruct((n_rows, D_MODEL), F32),
        compiler_params=_cparams(("parallel", "arbitrary")),
        name="moe_ln2",
    )(counts, x, mods, mods, mods, g, rk, rkt, w1, w3, w2, ln_g, ln_b)


def _rope_tables(ctx_len, seq):
    t = np.arange(seq)
    inv = ROPE_THETA ** (-np.arange(0, AXIS_DIM, 2, dtype=np.float64) / AXIS_DIM)
    ang_r = (t // GRID_W)[:, None].astype(np.float64) * inv
    ang_c = (t % GRID_W)[:, None].astype(np.float64) * inv
    cos = np.concatenate([np.cos(ang_r), np.cos(ang_r), np.cos(ang_c), np.cos(ang_c)], axis=1)
    sin = np.concatenate([-np.sin(ang_r), np.sin(ang_r), -np.sin(ang_c), np.sin(ang_c)], axis=1)
    cos = np.concatenate([cos, np.ones((ctx_len, ATT_QK))], axis=0)
    sin = np.concatenate([sin, np.zeros((ctx_len, ATT_QK))], axis=0)
    reps = LANES // ATT_QK
    return (jnp.asarray(np.tile(cos, (1, reps)), F32), jnp.asarray(np.tile(sin, (1, reps)), F32))


def kernel(x, c, ctx, c_ctx, w_ada, b_ada, w_in, w_out, ln1_g, ln1_b, ln2_g, ln2_b,
           lam_q1, lam_k1, lam_q2, lam_k2, subln_g, rkv_conv, decay_w0, decay_up, iclr_a0, iclr_up,
           gate_up, k_k, k_a, r_k, gn_g, gn_b, conv_w, ffn_w1, ffn_w3, ffn_w2,
           router_w, router_b, moe_w1, moe_w3, moe_w2):
    batch, seq, d = x.shape
    ctx_len = ctx.shape[1]
    depth = w_in.shape[0]
    assert d == D_MODEL and depth == DEPTH and batch < MOD_ROWS
    tm = 256
    tq = 256
    tk = 1024
    tm_moe = next(t for t in (1024, 512, 256) if (batch * ctx_len) % t == 0 and seq % t == 0)
    moe_blk = 128
    moe_blk0 = (2 * tm_moe // N_EXPERTS) * 5 // 4
    tri = jnp.asarray(np.tri(tm_moe, k=-1), BF16)
    scan_chunks_per_step = 4
    assert ctx_len % tm == 0 and seq % tm == 0 and (batch * ctx_len) % tm_moe == 0 and seq % tm_moe == 0
    assert seq % ctx_len == 0 and ctx_len % SCAN_CHUNK == 0 and seq % GRID_W == 0
    lay = _Layout(batch, ctx_len, seq)

    tokens = jnp.concatenate([x.reshape(batch * seq, d), ctx.reshape(batch * ctx_len, d)], axis=0)

    c_all = jnp.concatenate([c, c_ctx[None, :], jnp.zeros((MOD_ROWS - batch - 1, d), F32)], axis=0)
    mods = _adaln(c_all, w_ada, b_ada).reshape(depth * MOD_ROWS, 1, 6 * d)

    cos, sin = _rope_tables(ctx_len, seq)

    o = np.cumsum([0, 512, 512, 512, 256, 256, 256, 64, 64, 64, 256, 256, 256])
    wq = w_in[:, :, :o[3]].astype(BF16)
    wr = jnp.concatenate([w_in[:, :, o[3]:o[9]], jnp.zeros((depth, d, RW_W - (o[9] - o[3])), F32)],
                         axis=-1).astype(BF16)
    wc = w_in[:, :, o[9]:].astype(BF16)
    w_out_b = w_out.astype(BF16)

    z = jnp.zeros((depth, DECAY_LORA, RWKV_WIDTH), F32)
    wd = jnp.concatenate([
        jnp.concatenate([decay_up[:, 0], z], axis=-1),
        jnp.concatenate([z, decay_up[:, 1]], axis=-1),
        jnp.zeros((depth, 2 * ICLR_LORA, 2 * RWKV_WIDTH), F32)], axis=1)
    wa = jnp.concatenate([
        jnp.zeros((depth, 2 * DECAY_LORA, 2 * RWKV_WIDTH), F32),
        jnp.concatenate([iclr_up[:, 0], z], axis=-1),
        jnp.concatenate([z, iclr_up[:, 1]], axis=-1)], axis=1)
    wg = jnp.concatenate([gate_up, jnp.zeros((depth, LANES - GATE_LORA, RWKV_WIDTH), F32)], axis=1)
    w0 = decay_w0.reshape(depth, 1, 2 * RWKV_WIDTH)
    a0 = iclr_a0.reshape(depth, 1, 2 * RWKV_WIDTH)
    vec = lambda w: w.reshape(depth, 1, -1)

    lam4 = jnp.stack([lam_q1, lam_k1, lam_q2, lam_k2], axis=1)

    ffn_w1_b, ffn_w3_b, ffn_w2_b = ffn_w1.astype(BF16), ffn_w3.astype(BF16), ffn_w2.astype(BF16)
    moe_w1_b, moe_w3_b, moe_w2_b = moe_w1.astype(BF16), moe_w3.astype(BF16), moe_w2.astype(BF16)
    n_moe = router_w.shape[0]
    router_w_p = jnp.concatenate([router_w, jnp.zeros((n_moe, d, LANES - N_EXPERTS), F32)], axis=-1)
    router_b_p = jnp.concatenate([router_b, jnp.full((n_moe, LANES - N_EXPERTS), -1e30, F32)],
                                 axis=-1).reshape(n_moe, 1, LANES)

    xs = tokens
    for l in range(depth):
        lam_init = 0.8 - 0.6 * math.exp(-0.3 * l)
        qkv, rw, cv = _inproj(lay, l, xs, mods, wq, wr, wc, cos, sin, tm)
        att = _attention(lay, l, qkv, lam4, vec(subln_g), lam_init, tq, tk)
        bv, g, co, *terms = _prep_chunk_terms(lay, l, rw, cv, rkv_conv, w0, wd, a0, wa, wg,
                                              vec(k_k), vec(k_a), vec(r_k), conv_w, tm)
        yf, yb = _scan(lay, *terms, scan_chunks_per_step)
        last_moe = l == depth - 1 and l % 2 == 1
        n_rows = lay.n_lat if last_moe else lay.n_tok
        xs = _outproj(lay, l, att, yf, yb, bv, g, co, xs, mods, w_out_b, vec(gn_g), vec(gn_b),
                      vec(ln1_g), vec(ln1_b), tm, n_rows)
        if l % 2 == 0:
            xs = _ffn(lay, l, l // 2, xs, mods, ffn_w1_b, ffn_w3_b, ffn_w2_b, vec(ln2_g), vec(ln2_b), tm)
        else:
            g_tok, rk, rkt, cnt = _router(lay, l, l // 2, xs, mods, router_w_p, router_b_p, tri, tm_moe,
                                          n_rows)
            counts = cnt[:, 0, :N_EXPERTS].reshape(-1)
            xs = _moe(lay, l, l // 2, xs, mods, counts, g_tok, rk, rkt, moe_w1_b, moe_w3_b, moe_w2_b,
                      vec(ln2_g), vec(ln2_b), tm_moe, moe_blk0, moe_blk, n_rows)
    return xs[:lay.n_lat].reshape(batch, seq, d)
```

```python
import functools
import math

import numpy as np
import jax
import jax.numpy as jnp
from jax import lax
from jax.experimental import pallas as pl
from jax.experimental.pallas import tpu as pltpu

F32 = jnp.float32
BF16 = jnp.bfloat16
HI = lax.Precision.HIGHEST

D_MODEL = 1024
DEPTH = 4
GRID_W = 64
ATT_HEADS = 4
ATT_QK = 64
ATT_V = 128
ATT_WIDTH = 512
AXIS_DIM = 32
ROPE_THETA = 10000.0
SUBLN_EPS = 1e-5
RWKV_HEADS = 4
RWKV_N = 64
RWKV_WIDTH = 256
DECAY_LORA = 32
ICLR_LORA = 32
GATE_LORA = 64
GN_EPS = 64e-5
CONV_WIDTH = 256
D_FF = 2816
N_EXPERTS = 8
D_FF_EXPERT = 1408
DN_ALPHA = (2 * DEPTH) ** 0.25
LN_EPS = 1e-5

LANES = 128
MOD_ROWS = 16
QKV_W = 3 * ATT_WIDTH
RW_W = 1024
CV_W = 3 * CONV_WIDTH
SCAN_CHUNK = 64
VMEM_LIMIT = 56 * 1024 * 1024


def _cparams(sem):
    return pltpu.CompilerParams(dimension_semantics=sem, vmem_limit_bytes=VMEM_LIMIT)


def _dot(a, b):
    return jnp.dot(a, b, preferred_element_type=F32)


def _dot_hi(a, b):
    return jnp.dot(a, b, precision=HI, preferred_element_type=F32)


def _dot_nt_hi(a, b):
    return lax.dot_general(a, b, (((1,), (1,)), ((), ())), precision=HI, preferred_element_type=F32)


def _dot_tn_hi(a, b):
    return lax.dot_general(a, b, (((0,), (0,)), ((), ())), precision=HI, preferred_element_type=F32)


def _split3(a):
    hi = a.astype(BF16)
    rest = a - hi.astype(F32)
    mid = rest.astype(BF16)
    return hi, mid, (rest - mid.astype(F32)).astype(BF16)


def _dot_mask_lhs(mask, a):
    m = jnp.where(mask, 1.0, 0.0).astype(BF16)
    return sum(jnp.dot(m, part, preferred_element_type=F32) for part in _split3(a))


def _dot_mask_rhs(a, mask):
    m = mask.astype(BF16)
    return sum(jnp.dot(part, m, preferred_element_type=F32) for part in _split3(a))


def _layer_norm(z, g, b):
    mu = jnp.mean(z, axis=-1, keepdims=True)
    zc = z - mu
    var = jnp.mean(zc * zc, axis=-1, keepdims=True)
    return zc * lax.rsqrt(var + LN_EPS) * g + b


def _head_ones(width, head):
    r = lax.broadcasted_iota(jnp.int32, (width, width), 0) // head
    c = lax.broadcasted_iota(jnp.int32, (width, width), 1) // head
    return (r == c).astype(F32)


def _ada_kernel(c_ref, w_ref, b_ref, o_ref):
    c = c_ref[...]
    sc = c * jax.nn.sigmoid(c)
    o_ref[...] = _dot(sc.astype(BF16), w_ref[...].astype(BF16)) + b_ref[...]


def _adaln(c_all, w_ada, b_ada):
    depth, d, n = w_ada.shape
    tn = 1536
    return pl.pallas_call(
        _ada_kernel,
        grid=(depth, n // tn),
        in_specs=[
            pl.BlockSpec((MOD_ROWS, d), lambda l, j: (0, 0)),
            pl.BlockSpec((None, d, tn), lambda l, j: (l, 0, j)),
            pl.BlockSpec((None, 1, tn), lambda l, j: (l, 0, j)),
        ],
        out_specs=pl.BlockSpec((None, MOD_ROWS, tn), lambda l, j: (l, 0, j)),
        out_shape=jax.ShapeDtypeStruct((depth, MOD_ROWS, n), F32),
        compiler_params=_cparams(("parallel", "parallel")),
        name="adaln",
    )(c_all, w_ada, b_ada.reshape(depth, 1, n))


class _Layout:
    def __init__(self, batch, ctx_len, seq):
        self.batch, self.ctx_len, self.seq = batch, ctx_len, seq
        self.n_lat = batch * seq
        self.n_tok = self.n_lat + batch * ctx_len

    def mod_spec(self, layer, tm, which):
        n_lat_tiles = self.n_lat // tm
        rows_per = self.seq // tm
        batch = self.batch

        def imap(i, *_):
            row = jnp.where(i < n_lat_tiles, i // rows_per, batch)
            return (layer * MOD_ROWS + row, 0, which)

        return pl.BlockSpec((None, 1, D_MODEL), imap)


def _inproj_kernel(x_ref, sh_ref, sc_ref, wq_ref, wr_ref, wc_ref, cos_ref, sin_ref,
                   oq_ref, or_ref, oc_ref):
    h = (x_ref[...] * (1.0 + sc_ref[...]) + sh_ref[...]).astype(BF16)
    qkv = _dot(h, wq_ref[...])
    cos = cos_ref[...]
    sin = sin_ref[...]
    lane = lax.broadcasted_iota(jnp.int32, cos.shape, 1)
    first_half = (lane % AXIS_DIM) < (AXIS_DIM // 2)
    for s in range(2 * ATT_WIDTH // LANES):
        seg = qkv[:, s * LANES:(s + 1) * LANES]
        swapped = jnp.where(first_half, pltpu.roll(seg, LANES - AXIS_DIM // 2, 1),
                            pltpu.roll(seg, AXIS_DIM // 2, 1))
        rot = seg * cos + swapped * sin
        if s < ATT_WIDTH // LANES:
            rot = rot * (ATT_QK ** -0.5 * math.log2(math.e))
        oq_ref[:, s * LANES:(s + 1) * LANES] = rot.astype(BF16)
    oq_ref[:, 2 * ATT_WIDTH:] = qkv[:, 2 * ATT_WIDTH:].astype(BF16)
    or_ref[...] = _dot(h, wr_ref[...])
    oc_ref[...] = _dot(h, wc_ref[...])


def _inproj(lay, layer, x, mods, wq, wr, wc, cos, sin, tm):
    n_lat_tiles = lay.n_lat // tm
    ctx_tiles_per = lay.ctx_len // tm
    lat_tiles_per = lay.seq // tm

    def rope_map(i):
        return (jnp.where(i < n_lat_tiles, i % lat_tiles_per,
                          lat_tiles_per + (i - n_lat_tiles) % ctx_tiles_per), 0)

    row = lambda i: (i, 0)
    wmap = lambda i: (layer, 0, 0)
    return pl.pallas_call(
        _inproj_kernel,
        grid=(lay.n_tok // tm,),
        in_specs=[
            pl.BlockSpec((tm, D_MODEL), row),
            lay.mod_spec(layer, tm, 0),
            lay.mod_spec(layer, tm, 1),
            pl.BlockSpec((None, D_MODEL, QKV_W), wmap),
            pl.BlockSpec((None, D_MODEL, RW_W), wmap),
            pl.BlockSpec((None, D_MODEL, CV_W), wmap),
            pl.BlockSpec((tm, LANES), rope_map),
            pl.BlockSpec((tm, LANES), rope_map),
        ],
        out_specs=[
            pl.BlockSpec((tm, QKV_W), row),
            pl.BlockSpec((tm, RW_W), row),
            pl.BlockSpec((tm, CV_W), row),
        ],
        out_shape=[
            jax.ShapeDtypeStruct((lay.n_tok, QKV_W), BF16),
            jax.ShapeDtypeStruct((lay.n_tok, RW_W), F32),
            jax.ShapeDtypeStruct((lay.n_tok, CV_W), F32),
        ],
        compiler_params=_cparams(("parallel",)),
        name="inproj",
    )(x, mods, mods, wq, wr, wc, cos, sin)


def _attn_kernel(q_ref, kc_ref, vc_ref, kl_ref, vl_ref, lam_ref, g_ref, o_ref, vxc_ref, vxl_ref,
                 *, lam_init, n_lat_q, tq, tk):
    @pl.when(pl.program_id(2) == 0)
    def _():
        vxc_ref[:, :ATT_V] = vc_ref[...]
        vxc_ref[:, ATT_V:] = jnp.ones((vxc_ref.shape[0], LANES), BF16)
        vxl_ref[:, :ATT_V] = vl_ref[...]
        vxl_ref[:, ATT_V:] = jnp.ones((vxl_ref.shape[0], LANES), BF16)

    q = q_ref[...]
    lane = lax.broadcasted_iota(jnp.int32, q.shape, 1)
    zero = jnp.zeros_like(q)
    qq = jnp.concatenate([jnp.where(lane < ATT_QK, q, zero), jnp.where(lane >= ATT_QK, q, zero)], axis=0)
    lam4 = lam_ref[...]
    lam = (jnp.exp(jnp.sum(lam4[0:1] * lam4[1:2], axis=-1, keepdims=True))
           - jnp.exp(jnp.sum(lam4[2:3] * lam4[3:4], axis=-1, keepdims=True)) + lam_init)

    def scores(blk):
        k_ref, _, start, size = blk
        return lax.dot_general(qq, k_ref[start:start + size, :], _NT, preferred_element_type=F32)

    def attend(blocks):
        s = scores(blocks[0])
        m = acc = None
        for j, (_, vx_ref, start, size) in enumerate(blocks):
            s_next = scores(blocks[j + 1]) if j + 1 < len(blocks) else None
            bm = jnp.max(s, axis=-1, keepdims=True)
            m_new = bm if m is None else jnp.maximum(m, bm)
            p = jnp.exp2(s - m_new).astype(BF16)
            pv = _dot(p, vx_ref[start:start + size, :])
            acc = pv if m is None else acc * jnp.exp2(m - m_new) + pv
            m, s = m_new, s_next
        o = acc[:, :ATT_V] / acc[:, ATT_V:]
        dlt = o[:tq] - lam * o[tq:]
        ms = jnp.mean(dlt * dlt, axis=-1, keepdims=True)
        o_ref[...] = (dlt * lax.rsqrt(ms + SUBLN_EPS) * g_ref[...] * (1.0 - lam_init)).astype(o_ref.dtype)

    ctx_blocks = [(kc_ref, vxc_ref, 0, kc_ref.shape[0])]
    lat_blocks = [(kl_ref, vxl_ref, s0, tk) for s0 in range(0, kl_ref.shape[0], tk)]
    is_ctx = pl.program_id(2) >= n_lat_q

    @pl.when(is_ctx)
    def _():
        attend(ctx_blocks)

    @pl.when(jnp.logical_not(is_ctx))
    def _():
        attend(ctx_blocks + lat_blocks)


def _attention(lay, layer, qkv, lam4, subln_g, lam_init, tq, tk):
    n_ctx_q = lay.ctx_len // tq
    n_lat_q = lay.seq // tq
    n_lat_tiles = lay.n_lat // tq
    ctx_blk0 = lay.n_lat // lay.ctx_len
    hq = ATT_WIDTH // LANES

    def qmap(b, h, qi):
        return (jnp.where(qi < n_lat_q, b * n_lat_q + qi, n_lat_tiles + b * n_ctx_q + (qi - n_lat_q)), h)

    kern = functools.partial(_attn_kernel, lam_init=lam_init, n_lat_q=n_lat_q, tq=tq, tk=tk)
    return pl.pallas_call(
        kern,
        grid=(lay.batch, ATT_HEADS, n_lat_q + n_ctx_q),
        in_specs=[
            pl.BlockSpec((tq, LANES), qmap),
            pl.BlockSpec((lay.ctx_len, LANES), lambda b, h, qi: (ctx_blk0 + b, hq + h)),
            pl.BlockSpec((lay.ctx_len, LANES), lambda b, h, qi: (ctx_blk0 + b, 2 * hq + h)),
            pl.BlockSpec((lay.seq, LANES), lambda b, h, qi: (b, hq + h)),
            pl.BlockSpec((lay.seq, LANES), lambda b, h, qi: (b, 2 * hq + h)),
            pl.BlockSpec((None, 4, ATT_QK), lambda b, h, qi: (layer, 0, 0)),
            pl.BlockSpec((None, 1, ATT_V), lambda b, h, qi: (layer, 0, 0)),
        ],
        out_specs=pl.BlockSpec((tq, LANES), qmap),
        out_shape=jax.ShapeDtypeStruct((lay.n_tok, ATT_WIDTH), BF16),
        scratch_shapes=[pltpu.VMEM((lay.ctx_len, ATT_V + LANES), BF16),
                        pltpu.VMEM((lay.seq, ATT_V + LANES), BF16)],
        compiler_params=_cparams(("parallel", "parallel", "arbitrary")),
        name="diff_attn",
    )(qkv, qkv, qkv, qkv, qkv, lam4, subln_g)


def _prep_kernel(rw_ref, rwp_ref, rwn_ref, cv_ref, cvp_ref, cvn_ref,
                 rkvw_ref, w0_ref, wd_ref, a0_ref, wa_ref, wg_ref, kk_w_ref, ka_ref, rk_ref, cw_ref,
                 r_ref, v_ref, kk_ref, lw_ref, bb_ref, kd_ref, bv_ref, g_ref, co_ref,
                 *, tm, n_lat, ctx_len, seq):
    i = pl.program_id(0)
    start = i * tm
    in_lat = start < n_lat
    seg_pos = jnp.where(in_lat, start % seq, (start - n_lat) % ctx_len)
    seg_len = jnp.where(in_lat, seq, ctx_len)
    has_prev = seg_pos != 0
    has_next = seg_pos + tm != seg_len

    def shifted(u, prev_row, next_row):
        rows = lax.broadcasted_iota(jnp.int32, u.shape, 0)
        prev_row = jnp.where(has_prev, prev_row, jnp.zeros_like(prev_row))
        next_row = jnp.where(has_next, next_row, jnp.zeros_like(next_row))
        up = jnp.where(rows == 0, prev_row, pltpu.roll(u, 1, 0))
        un = jnp.where(rows == tm - 1, next_row, pltpu.roll(u, tm - 1, 0))
        return up, un

    def conv3(u, prev_row, next_row, w):
        up, un = shifted(u, prev_row, next_row)
        return up * w[0:1] + u * w[1:2] + un * w[2:3]

    rw = rw_ref[...]
    nrkv = 3 * RWKV_WIDTH
    rkv = conv3(rw[:, :nrkv], rwp_ref[7:8, :nrkv], rwn_ref[0:1, :nrkv], rkvw_ref[...])
    r = rkv[:, :RWKV_WIDTH]
    k = rkv[:, RWKV_WIDTH:2 * RWKV_WIDTH]
    v = rkv[:, 2 * RWKV_WIDTH:]

    lora = rw[:, nrkv:nrkv + LANES]
    gate = rw[:, nrkv + LANES:]
    wl = _dot3(_split(jnp.tanh(lora)), _split(wd_ref[...])) + w0_ref[...]
    lw = (-math.exp(-0.5)) * jax.nn.sigmoid(wl)
    a = jax.nn.sigmoid(_dot3(_split(lora), _split(wa_ref[...])) + a0_ref[...])
    g = _dot3(_split(jax.nn.sigmoid(gate)), _split(wg_ref[...]))

    ones = _head_ones(RWKV_WIDTH, RWKV_N)
    kraw = k * kk_w_ref[...]
    ss = _dot_mask_rhs(kraw * kraw, ones)
    kk = kraw * lax.rsqrt(jnp.maximum(ss, 1e-24))
    k2 = jnp.concatenate([k, k], axis=1)
    ka2 = jnp.concatenate([ka_ref[...], ka_ref[...]], axis=1)
    kd = k2 * (1.0 + (a - 1.0) * ka2)
    bb = jnp.concatenate([kk, kk], axis=1) * a
    bonus = _dot_mask_rhs(r * (kd[:, :RWKV_WIDTH] + kd[:, RWKV_WIDTH:]) * rk_ref[...], ones)

    r_ref[...] = r
    v_ref[...] = v
    kk_ref[...] = kk
    lw_ref[...] = lw
    bb_ref[...] = bb
    kd_ref[...] = kd
    bv_ref[...] = bonus * v
    g_ref[...] = g

    def gated(ref):
        return ref[:, 2 * CONV_WIDTH:] * ref[:, :CONV_WIDTH]

    cv_u = gated(cv_ref)
    conv = conv3(cv_u, gated(cvp_ref)[7:8], gated(cvn_ref)[0:1], cw_ref[...])
    co_ref[...] = (cv_ref[:, CONV_WIDTH:2 * CONV_WIDTH] * conv).astype(co_ref.dtype)


def _split(a):
    hi = a.astype(BF16)
    return hi, (a - hi.astype(F32)).astype(BF16)


_NN = (((1,), (0,)), ((), ()))
_NT = (((1,), (1,)), ((), ()))
_TN = (((0,), (0,)), ((), ()))


def _dot3(a, b, dims=_NN):
    (ah, al), (bh, bl) = a, b
    dg = lambda p, q: lax.dot_general(p, q, dims, preferred_element_type=F32)
    return dg(ah, bh) + dg(ah, bl) + dg(al, bh)


def _dot1(a, b, dims=_NN):
    return lax.dot_general(a.astype(BF16), b.astype(BF16), dims, preferred_element_type=F32)


def _chunk_terms_kernel(r_ref, v_ref, kk_ref, lw_ref, bb_ref, kd_ref, g_ref, h_ref, rq_ref, y0_ref):
    c = SCAN_CHUNK
    n = RWKV_N
    row = lax.broadcasted_iota(jnp.int32, (c, c), 0)
    col = lax.broadcasted_iota(jnp.int32, (c, c), 1)
    row2 = lax.broadcasted_iota(jnp.int32, (c, 2 * c), 0)
    col2 = lax.broadcasted_iota(jnp.int32, (c, 2 * c), 1) % c
    eye = lax.broadcasted_iota(jnp.int32, (n, n), 0) == lax.broadcasted_iota(jnp.int32, (n, n), 1)
    n_chunks = r_ref.shape[0] // c

    chains = []
    for ck in range(n_chunks):
        rows = slice(ck * c, (ck + 1) * c)
        r_all = r_ref[rows, :]
        v_all = v_ref[rows, :]
        kk = kk_ref[rows, :]
        for d in range(2):
            sgn = 1 if d == 0 else -1
            dsl = slice(d * RWKV_WIDTH, (d + 1) * RWKV_WIDTH)
            incl = (col - row) * sgn <= 0
            lw = lw_ref[rows, dsl]
            bb = bb_ref[rows, dsl]
            kd = kd_ref[rows, dsl]
            lp = _dot_mask_lhs(incl, lw)
            lt = jnp.sum(lw, axis=0, keepdims=True)
            p_inv = jnp.exp(-lp)
            p_end = jnp.exp(lt - lp)
            a_t = -kk * jnp.exp(lp - lw)
            b_t = bb * p_inv
            k_t = kd * p_inv
            r_t = r_all * jnp.exp(lp)
            k_e = kd * p_end
            b_e = bb * p_end
            p_tot = jnp.exp(lt)
            for h in range(RWKV_HEADS):
                sl = slice(h * n, (h + 1) * n)
                chains.append(dict(
                    strict=(col - row) * sgn < 0, incl2=(col2 - row2) * sgn <= 0,
                    ah=a_t[:, sl], rh=r_t[:, sl], vh=v_all[:, sl], bt=b_t[:, sl], kt=k_t[:, sl],
                    be=b_e[:, sl], ke=k_e[:, sl], ptot=p_tot[:, sl]))

    for ch in chains:
        ch["sc"] = _dot3(_split(jnp.concatenate([ch["ah"], ch["rh"]], axis=0)),
                         _split(jnp.concatenate([ch["bt"], ch["kt"]], axis=0)), _NT)
    for ch in chains:
        sc = ch["sc"]
        ch["nmat"] = jnp.where(ch["strict"], sc[:c, :c], 0.0).astype(BF16)
        ch["mr"] = jnp.where(ch["incl2"], sc[c:, :], 0.0).astype(BF16)
        ch["mkv"] = _dot1(jnp.where(ch["strict"], sc[:c, c:], 0.0), ch["vh"])
    for ch in chains:
        ch["z"] = jnp.concatenate([ch["ah"], ch["mkv"]], axis=1)
    n_factors = c.bit_length() - 1
    for p in range(n_factors):
        for ch in chains:
            ch["z"] = ch["z"] + _dot1(ch["nmat"], ch["z"])
        if p < n_factors - 1:
            for ch in chains:
                ch["nmat"] = _dot1(ch["nmat"], ch["nmat"]).astype(BF16)
    for ch in chains:
        z = ch["z"]
        ch["ws"] = z[:, :n].astype(BF16)
        ch["uv"] = jnp.concatenate([z[:, n:], ch["vh"]], axis=0).astype(BF16)
    for ch in chains:
        mr = ch["mr"]
        ch["rq"] = ch["rh"] + _dot1(mr[:, :c], ch["ws"])
        ch["y0"] = _dot1(mr, ch["uv"])
        ch["g"] = jnp.where(eye, ch["ptot"], 0.0) + _dot1(ch["ws"], ch["be"], _TN)
        ch["h"] = _dot1(ch["uv"], jnp.concatenate([ch["be"], ch["ke"]], axis=0), _TN)
    for ck in range(n_chunks):
        rows = slice(ck * c, (ck + 1) * c)
        for d in range(2):
            first = (ck * 2 + d) * RWKV_HEADS
            part = chains[first:first + RWKV_HEADS]
            for ref, key in ((g_ref, "g"), (h_ref, "h")):
                top = jnp.concatenate([ch[key] for ch in part], axis=1)
                if c > n:
                    top = jnp.concatenate([top, jnp.zeros((c - n, RWKV_WIDTH), F32)], axis=0)
                ref[d, rows, :] = top
            rq_ref[d, rows, :] = jnp.concatenate([ch["rq"] for ch in part], axis=1)
            y0_ref[d, rows, :] = jnp.concatenate([ch["y0"] for ch in part], axis=1)


def _prep_chunk_kernel(*refs, tm, n_lat, ctx_len, seq):
    ins, (bv_ref, g_ref, co_ref), terms, scratch = refs[:16], refs[16:19], refs[19:23], refs[23:]
    _prep_kernel(*ins, *scratch, bv_ref, g_ref, co_ref, tm=tm, n_lat=n_lat, ctx_len=ctx_len, seq=seq)
    _chunk_terms_kernel(*scratch, *terms)


def _prep_chunk_terms(lay, layer, rw, cv, rkv_conv, w0, wd, a0, wa, wg, k_k, k_a, r_k, conv_w, tm):
    n8 = lay.n_tok // 8
    t8 = tm // 8
    row = lambda i: (i, 0)
    prev = lambda i: (jnp.maximum(i * t8 - 1, 0), 0)
    nxt = lambda i: (jnp.minimum((i + 1) * t8, n8 - 1), 0)
    lmap = lambda i: (layer, 0, 0)
    w2 = 2 * RWKV_WIDTH
    kern = functools.partial(_prep_chunk_kernel, tm=tm, n_lat=lay.n_lat, ctx_len=lay.ctx_len, seq=lay.seq)
    f32 = lambda w: jax.ShapeDtypeStruct((lay.n_tok, w), F32)
    term = pl.BlockSpec((2, tm, RWKV_WIDTH), lambda i: (0, i, 0))
    term_shape = jax.ShapeDtypeStruct((2, lay.n_tok, RWKV_WIDTH), F32)
    return pl.pallas_call(
        kern,
        grid=(lay.n_tok // tm,),
        in_specs=[
            pl.BlockSpec((tm, RW_W), row), pl.BlockSpec((8, RW_W), prev), pl.BlockSpec((8, RW_W), nxt),
            pl.BlockSpec((tm, CV_W), row), pl.BlockSpec((8, CV_W), prev), pl.BlockSpec((8, CV_W), nxt),
            pl.BlockSpec((None, 3, 3 * RWKV_WIDTH), lmap),
            pl.BlockSpec((None, 1, w2), lmap),
            pl.BlockSpec((None, LANES, w2), lmap),
            pl.BlockSpec((None, 1, w2), lmap),
            pl.BlockSpec((None, LANES, w2), lmap),
            pl.BlockSpec((None, LANES, RWKV_WIDTH), lmap),
            pl.BlockSpec((None, 1, RWKV_WIDTH), lmap),
            pl.BlockSpec((None, 1, RWKV_WIDTH), lmap),
            pl.BlockSpec((None, 1, RWKV_WIDTH), lmap),
            pl.BlockSpec((None, 3, CONV_WIDTH), lmap),
        ],
        out_specs=[pl.BlockSpec((tm, RWKV_WIDTH), row), pl.BlockSpec((tm, RWKV_WIDTH), row),
                   pl.BlockSpec((tm, CONV_WIDTH), row), term, term, term, term],
        out_shape=[f32(RWKV_WIDTH), f32(RWKV_WIDTH), jax.ShapeDtypeStruct((lay.n_tok, CONV_WIDTH), BF16),
                   term_shape, term_shape, term_shape, term_shape],
        scratch_shapes=[pltpu.VMEM((tm, RWKV_WIDTH), F32)] * 3 + [pltpu.VMEM((tm, w2), F32)] * 3,
        compiler_params=_cparams(("parallel",)),
        name="rwkv_prep_chunk_terms",
    )(rw, rw, rw, cv, cv, cv, rkv_conv, w0, wd, a0, wa, wg, k_k, k_a, r_k, conv_w)


def _scan_kernel(gf_ref, hf_ref, rqf_ref, y0f_ref, gb_ref, hb_ref, rqb_ref, y0b_ref, yf_ref, yb_ref, st_ref):
    @pl.when(pl.program_id(1) == 0)
    def _():
        st_ref[...] = jnp.zeros_like(st_ref)

    dirs = ((gf_ref, hf_ref, rqf_ref, y0f_ref, yf_ref), (gb_ref, hb_ref, rqb_ref, y0b_ref, yb_ref))
    heads = [slice(h * RWKV_N, (h + 1) * RWKV_N) for h in range(RWKV_HEADS)]
    n_sub = gf_ref.shape[0] // SCAN_CHUNK
    state = [[st_ref[d, h] for h in range(RWKV_HEADS)] for d in range(2)]
    for step in range(n_sub):
        rows = [slice(k * SCAN_CHUNK, (k + 1) * SCAN_CHUNK) for k in (step, n_sub - 1 - step)]
        top = [slice(k * SCAN_CHUNK, k * SCAN_CHUNK + RWKV_N) for k in (step, n_sub - 1 - step)]
        s0 = [[_split(state[d][h]) for h in range(RWKV_HEADS)] for d in range(2)]
        state = [[_dot1(s0[d][h][0], dirs[d][0][top[d], sl]) + _dot1(s0[d][h][1], dirs[d][0][top[d], sl])
                  + dirs[d][1][top[d], sl] for h, sl in enumerate(heads)] for d in range(2)]
        for d in range(2):
            rq_ref, y0_ref, y_ref = dirs[d][2:]
            y_ref[rows[d], :] = jnp.concatenate(
                [_dot1(rq_ref[rows[d], sl], s0[d][h][0], _NT) + y0_ref[rows[d], sl]
                 for h, sl in enumerate(heads)], axis=1)
    for d in range(2):
        for h in range(RWKV_HEADS):
            st_ref[d, h] = state[d][h]


def _scan(lay, g, h, rq, y0, chunks_per_step):
    c = SCAN_CHUNK * chunks_per_step
    nc_ctx = lay.ctx_len // c
    nc_lat = lay.seq // c
    lat_blocks = lay.n_lat // c

    def blk(b, d, ci):
        if d == 0:
            return jnp.where(ci < nc_ctx, lat_blocks + b * nc_ctx + ci, b * nc_lat + (ci - nc_ctx))
        return jnp.where(ci < nc_ctx, lat_blocks + b * nc_ctx + (nc_ctx - 1 - ci),
                         b * nc_lat + (nc_lat - 1 - (ci - nc_ctx)))

    def term(d):
        return pl.BlockSpec((None, c, RWKV_WIDTH), lambda b, ci: (d, blk(b, d, ci), 0))

    def yspec(d):
        return pl.BlockSpec((c, RWKV_WIDTH), lambda b, ci: (blk(b, d, ci), 0))

    shp = jax.ShapeDtypeStruct((lay.n_tok, RWKV_WIDTH), F32)
    return pl.pallas_call(
        _scan_kernel,
        grid=(lay.batch, nc_ctx + nc_lat),
        in_specs=[term(0)] * 4 + [term(1)] * 4,
        out_specs=[yspec(0), yspec(1)],
        out_shape=[shp, shp],
        scratch_shapes=[pltpu.VMEM((2, RWKV_HEADS, RWKV_N, RWKV_N), F32)],
        compiler_params=_cparams(("parallel", "arbitrary")),
        name="rwkv_scan",
    )(g, h, rq, y0, g, h, rq, y0)


def _outproj_kernel(att_ref, yf_ref, yb_ref, bv_ref, g_ref, co_ref, x_ref, gate_ref, w_ref,
                    gng_ref, gnb_ref, lng_ref, lnb_ref, o_ref):
    ones = _head_ones(RWKV_WIDTH, RWKV_N)
    y = yf_ref[...] + yb_ref[...]
    mu = _dot_mask_rhs(y, ones) * (1.0 / RWKV_N)
    yc = y - mu
    var = _dot_mask_rhs(yc * yc, ones) * (1.0 / RWKV_N)
    yn = yc * lax.rsqrt(var + GN_EPS) * gng_ref[...] + gnb_ref[...]
    rwkv = ((yn + bv_ref[...]) * g_ref[...]).astype(BF16)
    mix = (_dot(att_ref[...], w_ref[:ATT_WIDTH, :])
           + _dot(rwkv, w_ref[ATT_WIDTH:ATT_WIDTH + RWKV_WIDTH, :])
           + _dot(co_ref[...], w_ref[ATT_WIDTH + RWKV_WIDTH:, :]))
    z = DN_ALPHA * x_ref[...] + gate_ref[...] * mix
    o_ref[...] = _layer_norm(z, lng_ref[...], lnb_ref[...])


def _outproj(lay, layer, att, yf, yb, bv, g, co, x, mods, w_out, gn_g, gn_b, ln_g, ln_b, tm, n_rows):
    row = lambda i: (i, 0)
    lmap = lambda i: (layer, 0, 0)
    return pl.pallas_call(
        _outproj_kernel,
        grid=(n_rows // tm,),
        in_specs=[
            pl.BlockSpec((tm, ATT_WIDTH), row),
            pl.BlockSpec((tm, RWKV_WIDTH), row),
            pl.BlockSpec((tm, RWKV_WIDTH), row),
            pl.BlockSpec((tm, RWKV_WIDTH), row),
            pl.BlockSpec((tm, RWKV_WIDTH), row),
            pl.BlockSpec((tm, CONV_WIDTH), row),
            pl.BlockSpec((tm, D_MODEL), row),
            lay.mod_spec(layer, tm, 2),
            pl.BlockSpec((None, D_MODEL, D_MODEL), lmap),
            pl.BlockSpec((None, 1, RWKV_WIDTH), lmap),
            pl.BlockSpec((None, 1, RWKV_WIDTH), lmap),
            pl.BlockSpec((None, 1, D_MODEL), lmap),
            pl.BlockSpec((None, 1, D_MODEL), lmap),
        ],
        out_specs=pl.BlockSpec((tm, D_MODEL), row),
        out_shape=jax.ShapeDtypeStruct((n_rows, D_MODEL), F32),
        compiler_params=_cparams(("parallel",)),
        name="outproj_ln1",
    )(att, yf, yb, bv, g, co, x, mods, w_out, gn_g, gn_b, ln_g, ln_b)


def _ffn_kernel(x_ref, sh_ref, sc_ref, gate_ref, w1_ref, w3_ref, w2_ref, lng_ref, lnb_ref, o_ref, *, n_split):
    x = x_ref[...]
    h = (x * (1.0 + sc_ref[...]) + sh_ref[...]).astype(BF16)
    step = D_FF // n_split
    f = jnp.zeros(x.shape, F32)
    for s in range(n_split):
        cols = slice(s * step, (s + 1) * step)
        a = _dot(h, w1_ref[:, cols])
        act = (a * jax.nn.sigmoid(a) * _dot(h, w3_ref[:, cols])).astype(BF16)
        f = f + _dot(act, w2_ref[cols, :])
    z = DN_ALPHA * x + gate_ref[...] * f
    o_ref[...] = _layer_norm(z, lng_ref[...], lnb_ref[...])


def _ffn(lay, layer, j, x, mods, w1, w3, w2, ln_g, ln_b, tm):
    row = lambda i: (i, 0)
    once = pl.Buffered(1)
    return pl.pallas_call(
        functools.partial(_ffn_kernel, n_split=2),
        grid=(lay.n_tok // tm,),
        in_specs=[
            pl.BlockSpec((tm, D_MODEL), row),
            lay.mod_spec(layer, tm, 3), lay.mod_spec(layer, tm, 4), lay.mod_spec(layer, tm, 5),
            pl.BlockSpec((None, D_MODEL, D_FF), lambda i: (j, 0, 0), pipeline_mode=once),
            pl.BlockSpec((None, D_MODEL, D_FF), lambda i: (j, 0, 0), pipeline_mode=once),
            pl.BlockSpec((None, D_FF, D_MODEL), lambda i: (j, 0, 0), pipeline_mode=once),
            pl.BlockSpec((None, 1, D_MODEL), lambda i: (layer, 0, 0)),
            pl.BlockSpec((None, 1, D_MODEL), lambda i: (layer, 0, 0)),
        ],
        out_specs=pl.BlockSpec((tm, D_MODEL), row),
        out_shape=jax.ShapeDtypeStruct((lay.n_tok, D_MODEL), F32),
        compiler_params=_cparams(("parallel",)),
        name="ffn_ln2",
    )(x, mods, mods, mods, w1, w3, w2, ln_g, ln_b)


def _router_kernel(x_ref, sh_ref, sc_ref, rw_ref, rb_ref, tri_ref, g_ref, rk_ref, rkt_ref, cnt_ref):
    h = x_ref[...] * (1.0 + sc_ref[...]) + sh_ref[...]
    logits = _dot3(_split(h), _split(rw_ref[...])) + rb_ref[...]
    lane = lax.broadcasted_iota(jnp.int32, logits.shape, 1)
    m1 = jnp.max(logits, axis=-1, keepdims=True)
    i1 = jnp.min(jnp.where(logits == m1, lane, LANES), axis=-1, keepdims=True)
    rest = jnp.where(lane == i1, -jnp.inf, logits)
    m2 = jnp.max(rest, axis=-1, keepdims=True)
    i2 = jnp.min(jnp.where(rest == m2, lane, LANES), axis=-1, keepdims=True)
    e2 = jnp.exp(m2 - m1)
    den = 1.0 + e2
    g_ref[...] = jnp.where(lane == i1, 1.0 / den, 0.0) + jnp.where(lane == i2, e2 / den, 0.0)
    sel = jnp.where(lane == i1, 1.0, 0.0) + jnp.where(lane == i2, 1.0, 0.0)
    rank = _dot(tri_ref[...], sel.astype(BF16))
    rk = jnp.where(sel > 0.0, rank, -1.0)
    rk_ref[...] = rk
    rkt_ref[...] = jnp.transpose(rk)[:N_EXPERTS, :]
    cnt = jnp.sum(sel, axis=0, keepdims=True).astype(jnp.int32)
    cnt_ref[...] = jnp.broadcast_to(cnt, cnt_ref.shape)


def _router(lay, layer, j, x, mods, router_w, router_b, tri, tm, n_rows):
    n_tiles = n_rows // tm
    row = lambda i: (i, 0)
    return pl.pallas_call(
        _router_kernel,
        grid=(n_tiles,),
        in_specs=[
            pl.BlockSpec((tm, D_MODEL), row),
            lay.mod_spec(layer, tm, 3), lay.mod_spec(layer, tm, 4),
            pl.BlockSpec((None, D_MODEL, LANES), lambda i: (j, 0, 0)),
            pl.BlockSpec((None, 1, LANES), lambda i: (j, 0, 0)),
            pl.BlockSpec((tm, tm), lambda i: (0, 0)),
        ],
        out_specs=[
            pl.BlockSpec((tm, LANES), row),
            pl.BlockSpec((tm, LANES), row),
            pl.BlockSpec((None, N_EXPERTS, tm), lambda i: (i, 0, 0)),
            pl.BlockSpec((None, 8, LANES), lambda i: (i, 0, 0)),
        ],
        out_shape=[
            jax.ShapeDtypeStruct((n_rows, LANES), F32),
            jax.ShapeDtypeStruct((n_rows, LANES), F32),
            jax.ShapeDtypeStruct((n_tiles, N_EXPERTS, tm), F32),
            jax.ShapeDtypeStruct((n_tiles, 8, LANES), jnp.int32),
        ],
        compiler_params=_cparams(("parallel",)),
        name="moe_router",
    )(x, mods, mods, router_w, router_b, tri)


def _moe_kernel(cnt_ref, x_ref, sh_ref, sc_ref, gate_ref, g_ref, rk_ref, rkt_ref, w1_ref, w3_ref, w2_ref,
                lng_ref, lnb_ref, o_ref, h_ref, acc_ref, *, blk):
    i = pl.program_id(0)
    e = pl.program_id(1)
    tm = x_ref.shape[0]

    @pl.when(e == 0)
    def _():
        h_ref[...] = (x_ref[...] * (1.0 + sc_ref[...]) + sh_ref[...]).astype(BF16)
        acc_ref[...] = jnp.zeros_like(acc_ref)

    lane = lax.broadcasted_iota(jnp.int32, (tm, LANES), 1)
    rk_col = jnp.sum(jnp.where(lane == e, rk_ref[...], 0.0), axis=-1, keepdims=True)
    rk_row = rkt_ref[pl.ds(e, 1), :]
    g_parts = _split(g_ref[...])

    pos_r = lax.broadcasted_iota(jnp.int32, (blk, tm), 0).astype(F32)
    pos_c = lax.broadcasted_iota(jnp.int32, (tm, blk), 1).astype(F32)
    lane_b = lax.broadcasted_iota(jnp.int32, (blk, LANES), 1)

    def block(jb, carry):
        base = (jb * blk).astype(F32)
        take = jnp.where(rk_row - base == pos_r, 1.0, 0.0).astype(BF16)
        put = jnp.where(rk_col - base == pos_c, 1.0, 0.0).astype(BF16)
        hg = _dot(take, h_ref[...]).astype(BF16)
        a = _dot(hg, w1_ref[...])
        act = (a * jax.nn.sigmoid(a) * _dot(hg, w3_ref[...])).astype(BF16)
        f = _dot(act, w2_ref[...])
        gg = _dot(take, g_parts[0]) + _dot(take, g_parts[1])
        ge = jnp.sum(jnp.where(lane_b == e, gg, 0.0), axis=-1, keepdims=True)
        acc_ref[...] += _dot(put, (ge * f).astype(BF16))
        return carry

    n_routed = cnt_ref[i * N_EXPERTS + e]
    lax.fori_loop(0, (n_routed + blk - 1) // blk, block, 0)

    @pl.when(e == N_EXPERTS - 1)
    def _():
        z = DN_ALPHA * x_ref[...] + gate_ref[...] * acc_ref[...]
        o_ref[...] = _layer_norm(z, lng_ref[...], lnb_ref[...])


def _moe(lay, layer, j, x, mods, counts, g, rk, rkt, w1, w3, w2, ln_g, ln_b, tm, blk, n_rows):
    row = lambda i, e, cnt: (i, 0)
    grid_spec = pltpu.PrefetchScalarGridSpec(
        num_scalar_prefetch=1,
        grid=(n_rows // tm, N_EXPERTS),
        in_specs=[
            pl.BlockSpec((tm, D_MODEL), row),
            lay.mod_spec(layer, tm, 3), lay.mod_spec(layer, tm, 4), lay.mod_spec(layer, tm, 5),
            pl.BlockSpec((tm, LANES), row),
            pl.BlockSpec((tm, LANES), row),
            pl.BlockSpec((None, N_EXPERTS, tm), lambda i, e, cnt: (i, 0, 0)),
            pl.BlockSpec((None, None, D_MODEL, D_FF_EXPERT), lambda i, e, cnt: (j, e, 0, 0)),
            pl.BlockSpec((None, None, D_MODEL, D_FF_EXPERT), lambda i, e, cnt: (j, e, 0, 0)),
            pl.BlockSpec((None, None, D_FF_EXPERT, D_MODEL), lambda i, e, cnt: (j, e, 0, 0)),
            pl.BlockSpec((None, 1, D_MODEL), lambda i, e, cnt: (layer, 0, 0)),
            pl.BlockSpec((None, 1, D_MODEL), lambda i, e, cnt: (layer, 0, 0)),
        ],
        out_specs=pl.BlockSpec((tm, D_MODEL), row),
        scratch_shapes=[pltpu.VMEM((tm, D_MODEL), BF16), pltpu.VMEM((tm, D_MODEL), F32)],
    )
    return pl.pallas_call(
        functools.partial(_moe_kernel, blk=blk),
        grid_spec=grid_spec,
        out_shape=jax.ShapeDtypeStruct((n_rows, D_MODEL), F32),
        compiler_params=_cparams(("parallel", "arbitrary")),
        name="moe_ln2",
    )(counts, x, mods, mods, mods, g, rk, rkt, w1, w3, w2, ln_g, ln_b)


def _rope_tables(ctx_len, seq):
    t = np.arange(seq)
    inv = ROPE_THETA ** (-np.arange(0, AXIS_DIM, 2, dtype=np.float64) / AXIS_DIM)
    ang_r = (t // GRID_W)[:, None].astype(np.float64) * inv
    ang_c = (t % GRID_W)[:, None].astype(np.float64) * inv
    cos = np.concatenate([np.cos(ang_r), np.cos(ang_r), np.cos(ang_c), np.cos(ang_c)], axis=1)
    sin = np.concatenate([-np.sin(ang_r), np.sin(ang_r), -np.sin(ang_c), np.sin(ang_c)], axis=1)
    cos = np.concatenate([cos, np.ones((ctx_len, ATT_QK))], axis=0)
    sin = np.concatenate([sin, np.zeros((ctx_len, ATT_QK))], axis=0)
    reps = LANES // ATT_QK
    return (jnp.asarray(np.tile(cos, (1, reps)), F32), jnp.asarray(np.tile(sin, (1, reps)), F32))


def kernel(x, c, ctx, c_ctx, w_ada, b_ada, w_in, w_out, ln1_g, ln1_b, ln2_g, ln2_b,
           lam_q1, lam_k1, lam_q2, lam_k2, subln_g, rkv_conv, decay_w0, decay_up, iclr_a0, iclr_up,
           gate_up, k_k, k_a, r_k, gn_g, gn_b, conv_w, ffn_w1, ffn_w3, ffn_w2,
           router_w, router_b, moe_w1, moe_w3, moe_w2):
    batch, seq, d = x.shape
    ctx_len = ctx.shape[1]
    depth = w_in.shape[0]
    assert d == D_MODEL and depth == DEPTH and batch < MOD_ROWS
    tm = 256
    tq = 256
    tk = 1024
    tm_moe = next(t for t in (1024, 512, 256) if (batch * ctx_len) % t == 0 and seq % t == 0)
    moe_blk = 128
    tri = jnp.asarray(np.tri(tm_moe, k=-1), BF16)
    scan_chunks_per_step = 4
    assert ctx_len % tm == 0 and seq % tm == 0 and (batch * ctx_len) % tm_moe == 0 and seq % tm_moe == 0
    assert seq % ctx_len == 0 and ctx_len % SCAN_CHUNK == 0 and seq % GRID_W == 0
    lay = _Layout(batch, ctx_len, seq)

    tokens = jnp.concatenate([x.reshape(batch * seq, d), ctx.reshape(batch * ctx_len, d)], axis=0)

    c_all = jnp.concatenate([c, c_ctx[None, :], jnp.zeros((MOD_ROWS - batch - 1, d), F32)], axis=0)
    mods = _adaln(c_all, w_ada, b_ada).reshape(depth * MOD_ROWS, 1, 6 * d)

    cos, sin = _rope_tables(ctx_len, seq)

    o = np.cumsum([0, 512, 512, 512, 256, 256, 256, 64, 64, 64, 256, 256, 256])
    wq = w_in[:, :, :o[3]].astype(BF16)
    wr = jnp.concatenate([w_in[:, :, o[3]:o[9]], jnp.zeros((depth, d, RW_W - (o[9] - o[3])), F32)],
                         axis=-1).astype(BF16)
    wc = w_in[:, :, o[9]:].astype(BF16)
    w_out_b = w_out.astype(BF16)

    z = jnp.zeros((depth, DECAY_LORA, RWKV_WIDTH), F32)
    wd = jnp.concatenate([
        jnp.concatenate([decay_up[:, 0], z], axis=-1),
        jnp.concatenate([z, decay_up[:, 1]], axis=-1),
        jnp.zeros((depth, 2 * ICLR_LORA, 2 * RWKV_WIDTH), F32)], axis=1)
    wa = jnp.concatenate([
        jnp.zeros((depth, 2 * DECAY_LORA, 2 * RWKV_WIDTH), F32),
        jnp.concatenate([iclr_up[:, 0], z], axis=-1),
        jnp.concatenate([z, iclr_up[:, 1]], axis=-1)], axis=1)
    wg = jnp.concatenate([gate_up, jnp.zeros((depth, LANES - GATE_LORA, RWKV_WIDTH), F32)], axis=1)
    w0 = decay_w0.reshape(depth, 1, 2 * RWKV_WIDTH)
    a0 = iclr_a0.reshape(depth, 1, 2 * RWKV_WIDTH)
    vec = lambda w: w.reshape(depth, 1, -1)

    lam4 = jnp.stack([lam_q1, lam_k1, lam_q2, lam_k2], axis=1)

    ffn_w1_b, ffn_w3_b, ffn_w2_b = ffn_w1.astype(BF16), ffn_w3.astype(BF16), ffn_w2.astype(BF16)
    moe_w1_b, moe_w3_b, moe_w2_b = moe_w1.astype(BF16), moe_w3.astype(BF16), moe_w2.astype(BF16)
    n_moe = router_w.shape[0]
    router_w_p = jnp.concatenate([router_w, jnp.zeros((n_moe, d, LANES - N_EXPERTS), F32)], axis=-1)
    router_b_p = jnp.concatenate([router_b, jnp.full((n_moe, LANES - N_EXPERTS), -1e30, F32)],
                                 axis=-1).reshape(n_moe, 1, LANES)

    xs = tokens
    for l in range(depth):
        lam_init = 0.8 - 0.6 * math.exp(-0.3 * l)
        qkv, rw, cv = _inproj(lay, l, xs, mods, wq, wr, wc, cos, sin, tm)
        att = _attention(lay, l, qkv, lam4, vec(subln_g), lam_init, tq, tk)
        bv, g, co, *terms = _prep_chunk_terms(lay, l, rw, cv, rkv_conv, w0, wd, a0, wa, wg,
                                              vec(k_k), vec(k_a), vec(r_k), conv_w, tm)
        yf, yb = _scan(lay, *terms, scan_chunks_per_step)
        last_moe = l == depth - 1 and l % 2 == 1
        n_rows = lay.n_lat if last_moe else lay.n_tok
        xs = _outproj(lay, l, att, yf, yb, bv, g, co, xs, mods, w_out_b, vec(gn_g), vec(gn_b),
                      vec(ln1_g), vec(ln1_b), tm, n_rows)
        if l % 2 == 0:
            xs = _ffn(lay, l, l // 2, xs, mods, ffn_w1_b, ffn_w3_b, ffn_w2_b, vec(ln2_g), vec(ln2_b), tm)
        else:
            g_tok, rk, rkt, cnt = _router(lay, l, l // 2, xs, mods, router_w_p, router_b_p, tri, tm_moe,
                                          n_rows)
            counts = cnt[:, 0, :N_EXPERTS].reshape(-1)
            xs = _moe(lay, l, l // 2, xs, mods, counts, g_tok, rk, rkt, moe_w1_b, moe_w3_b, moe_w2_b,
                      vec(ln2_g), vec(ln2_b), tm_moe, moe_blk, n_rows)
    return xs[:lay.n_lat].reshape(batch, seq, d)
```

```python
import functools
import math

import numpy as np
import jax
import jax.numpy as jnp
from jax import lax
from jax.experimental import pallas as pl
from jax.experimental.pallas import tpu as pltpu

F32 = jnp.float32
BF16 = jnp.bfloat16

D_MODEL = 1024
DEPTH = 4
GRID_W = 64
ATT_HEADS = 4
ATT_QK = 64
ATT_V = 128
ATT_WIDTH = 512
AXIS_DIM = 32
ROPE_THETA = 10000.0
SUBLN_EPS = 1e-5
RWKV_HEADS = 4
RWKV_N = 64
RWKV_WIDTH = 256
DECAY_LORA = 32
ICLR_LORA = 32
GATE_LORA = 64
GN_EPS = 64e-5
CONV_WIDTH = 256
D_FF = 2816
N_EXPERTS = 8
D_FF_EXPERT = 1408
DN_ALPHA = (2 * DEPTH) ** 0.25
LN_EPS = 1e-5

LANES = 128
MOD_ROWS = 16
QKV_W = 3 * ATT_WIDTH
RW_W = 1024
CV_W = 3 * CONV_WIDTH
SCAN_CHUNK = 64
VMEM_LIMIT = 56 * 1024 * 1024


def _cparams(sem):
    return pltpu.CompilerParams(dimension_semantics=sem, vmem_limit_bytes=VMEM_LIMIT)


def _dot(a, b):
    return jnp.dot(a, b, preferred_element_type=F32)


def _split3(a):
    hi = a.astype(BF16)
    rest = a - hi.astype(F32)
    mid = rest.astype(BF16)
    return hi, mid, (rest - mid.astype(F32)).astype(BF16)


def _dot_mask_lhs(mask, a):
    m = jnp.where(mask, 1.0, 0.0).astype(BF16)
    return sum(jnp.dot(m, part, preferred_element_type=F32) for part in _split3(a))


def _dot_mask_rhs(a, mask):
    m = mask.astype(BF16)
    return sum(jnp.dot(part, m, preferred_element_type=F32) for part in _split3(a))


def _layer_norm(z, g, b):
    mu = jnp.mean(z, axis=-1, keepdims=True)
    zc = z - mu
    var = jnp.mean(zc * zc, axis=-1, keepdims=True)
    return zc * lax.rsqrt(var + LN_EPS) * g + b


def _head_ones(width, head):
    r = lax.broadcasted_iota(jnp.int32, (width, width), 0) // head
    c = lax.broadcasted_iota(jnp.int32, (width, width), 1) // head
    return (r == c).astype(F32)


def _ada_kernel(c_ref, w_ref, b_ref, o_ref):
    c = c_ref[...]
    sc = c * jax.nn.sigmoid(c)
    o_ref[...] = _dot(sc.astype(BF16), w_ref[...].astype(BF16)) + b_ref[...]


def _adaln(c_all, w_ada, b_ada):
    depth, d, n = w_ada.shape
    tn = 1536
    return pl.pallas_call(
        _ada_kernel,
        grid=(depth, n // tn),
        in_specs=[
            pl.BlockSpec((MOD_ROWS, d), lambda l, j: (0, 0)),
            pl.BlockSpec((None, d, tn), lambda l, j: (l, 0, j)),
            pl.BlockSpec((None, 1, tn), lambda l, j: (l, 0, j)),
        ],
        out_specs=pl.BlockSpec((None, MOD_ROWS, tn), lambda l, j: (l, 0, j)),
        out_shape=jax.ShapeDtypeStruct((depth, MOD_ROWS, n), F32),
        compiler_params=_cparams(("parallel", "parallel")),
        name="adaln",
    )(c_all, w_ada, b_ada.reshape(depth, 1, n))


class _Layout:
    def __init__(self, batch, ctx_len, seq):
        self.batch, self.ctx_len, self.seq = batch, ctx_len, seq
        self.n_lat = batch * seq
        self.n_tok = self.n_lat + batch * ctx_len

    def mod_spec(self, layer, tm, which):
        n_lat_tiles = self.n_lat // tm
        rows_per = self.seq // tm
        batch = self.batch

        def imap(i, *_):
            row = jnp.where(i < n_lat_tiles, i // rows_per, batch)
            return (layer * MOD_ROWS + row, 0, which)

        return pl.BlockSpec((None, 1, D_MODEL), imap)


def _inproj_kernel(x_ref, sh_ref, sc_ref, wq_ref, wr_ref, wc_ref, cos_ref, sin_ref,
                   oq_ref, or_ref, oc_ref):
    h = (x_ref[...] * (1.0 + sc_ref[...]) + sh_ref[...]).astype(BF16)
    qkv = _dot(h, wq_ref[...])
    cos = cos_ref[...]
    sin = sin_ref[...]
    lane = lax.broadcasted_iota(jnp.int32, cos.shape, 1)
    first_half = (lane % AXIS_DIM) < (AXIS_DIM // 2)
    for s in range(2 * ATT_WIDTH // LANES):
        seg = qkv[:, s * LANES:(s + 1) * LANES]
        swapped = jnp.where(first_half, pltpu.roll(seg, LANES - AXIS_DIM // 2, 1),
                            pltpu.roll(seg, AXIS_DIM // 2, 1))
        rot = seg * cos + swapped * sin
        if s < ATT_WIDTH // LANES:
            rot = rot * (ATT_QK ** -0.5 * math.log2(math.e))
        oq_ref[:, s * LANES:(s + 1) * LANES] = rot.astype(BF16)
    oq_ref[:, 2 * ATT_WIDTH:] = qkv[:, 2 * ATT_WIDTH:].astype(BF16)
    or_ref[...] = _dot(h, wr_ref[...])
    oc_ref[...] = _dot(h, wc_ref[...])


def _inproj(lay, layer, x, mods, wq, wr, wc, cos, sin, tm):
    n_lat_tiles = lay.n_lat // tm
    ctx_tiles_per = lay.ctx_len // tm
    lat_tiles_per = lay.seq // tm

    def rope_map(i):
        return (jnp.where(i < n_lat_tiles, i % lat_tiles_per,
                          lat_tiles_per + (i - n_lat_tiles) % ctx_tiles_per), 0)

    row = lambda i: (i, 0)
    wmap = lambda i: (layer, 0, 0)
    return pl.pallas_call(
        _inproj_kernel,
        grid=(lay.n_tok // tm,),
        in_specs=[
            pl.BlockSpec((tm, D_MODEL), row),
            lay.mod_spec(layer, tm, 0),
            lay.mod_spec(layer, tm, 1),
            pl.BlockSpec((None, D_MODEL, QKV_W), wmap),
            pl.BlockSpec((None, D_MODEL, RW_W), wmap),
            pl.BlockSpec((None, D_MODEL, CV_W), wmap),
            pl.BlockSpec((tm, LANES), rope_map),
            pl.BlockSpec((tm, LANES), rope_map),
        ],
        out_specs=[
            pl.BlockSpec((tm, QKV_W), row),
            pl.BlockSpec((tm, RW_W), row),
            pl.BlockSpec((tm, CV_W), row),
        ],
        out_shape=[
            jax.ShapeDtypeStruct((lay.n_tok, QKV_W), BF16),
            jax.ShapeDtypeStruct((lay.n_tok, RW_W), F32),
            jax.ShapeDtypeStruct((lay.n_tok, CV_W), F32),
        ],
        compiler_params=_cparams(("parallel",)),
        name="inproj",
    )(x, mods, mods, wq, wr, wc, cos, sin)


def _attn_kernel(*refs, lam_init, tq, tk):
    n_kv = (len(refs) - 4) // 3
    q_ref = refs[0]
    k_refs = refs[1:1 + 2 * n_kv:2]
    v_refs = refs[2:2 + 2 * n_kv:2]
    lam_ref, g_ref, o_ref = refs[1 + 2 * n_kv:4 + 2 * n_kv]
    vx_refs = refs[4 + 2 * n_kv:]

    @pl.when(pl.program_id(2) == 0)
    def _():
        for v_ref, vx_ref in zip(v_refs, vx_refs):
            vx_ref[:, :ATT_V] = v_ref[...]
            vx_ref[:, ATT_V:] = jnp.ones((vx_ref.shape[0], LANES), BF16)

    q = q_ref[...]
    lane = lax.broadcasted_iota(jnp.int32, q.shape, 1)
    zero = jnp.zeros_like(q)
    qq = jnp.concatenate([jnp.where(lane < ATT_QK, q, zero), jnp.where(lane >= ATT_QK, q, zero)], axis=0)
    lam4 = lam_ref[...]
    lam = (jnp.exp(jnp.sum(lam4[0:1] * lam4[1:2], axis=-1, keepdims=True))
           - jnp.exp(jnp.sum(lam4[2:3] * lam4[3:4], axis=-1, keepdims=True)) + lam_init)

    def scores(blk):
        k_ref, _, start, size = blk
        return lax.dot_general(qq, k_ref[start:start + size, :], _NT, preferred_element_type=F32)

    def attend(blocks):
        s = scores(blocks[0])
        m = acc = None
        for j, (_, vx_ref, start, size) in enumerate(blocks):
            s_next = scores(blocks[j + 1]) if j + 1 < len(blocks) else None
            bm = jnp.max(s, axis=-1, keepdims=True)
            m_new = bm if m is None else jnp.maximum(m, bm)
            p = jnp.exp2(s - m_new).astype(BF16)
            pv = _dot(p, vx_ref[start:start + size, :])
            acc = pv if m is None else acc * jnp.exp2(m - m_new) + pv
            m, s = m_new, s_next
        o = acc[:, :ATT_V] / acc[:, ATT_V:]
        dlt = o[:tq] - lam * o[tq:]
        ms = jnp.mean(dlt * dlt, axis=-1, keepdims=True)
        o_ref[...] = (dlt * lax.rsqrt(ms + SUBLN_EPS) * g_ref[...] * (1.0 - lam_init)).astype(o_ref.dtype)

    blocks = []
    for k_ref, vx_ref in zip(k_refs, vx_refs):
        step = min(tk, k_ref.shape[0])
        blocks += [(k_ref, vx_ref, s0, step) for s0 in range(0, k_ref.shape[0], step)]
    attend(blocks)


def _attention(lay, layer, qkv, lam4, subln_g, lam_init, tq, tk, latent):
    seg = lay.seq if latent else lay.ctx_len
    n_q = seg // tq
    q_blk0 = 0 if latent else lay.n_lat // tq
    ctx_blk0 = lay.n_lat // lay.ctx_len
    hq = ATT_WIDTH // LANES

    kv_specs = [pl.BlockSpec((lay.ctx_len, LANES), lambda b, h, qi: (ctx_blk0 + b, hq + h)),
                pl.BlockSpec((lay.ctx_len, LANES), lambda b, h, qi: (ctx_blk0 + b, 2 * hq + h))]
    scratch = [pltpu.VMEM((lay.ctx_len, ATT_V + LANES), BF16)]
    if latent:
        kv_specs += [pl.BlockSpec((lay.seq, LANES), lambda b, h, qi: (b, hq + h)),
                     pl.BlockSpec((lay.seq, LANES), lambda b, h, qi: (b, 2 * hq + h))]
        scratch += [pltpu.VMEM((lay.seq, ATT_V + LANES), BF16)]
    kern = functools.partial(_attn_kernel, lam_init=lam_init, tq=tq, tk=tk)
    return pl.pallas_call(
        kern,
        grid=(lay.batch, ATT_HEADS, n_q),
        in_specs=[pl.BlockSpec((tq, LANES), lambda b, h, qi: (q_blk0 + b * n_q + qi, h))] + kv_specs + [
            pl.BlockSpec((None, 4, ATT_QK), lambda b, h, qi: (layer, 0, 0)),
            pl.BlockSpec((None, 1, ATT_V), lambda b, h, qi: (layer, 0, 0)),
        ],
        out_specs=pl.BlockSpec((tq, LANES), lambda b, h, qi: (b * n_q + qi, h)),
        out_shape=jax.ShapeDtypeStruct((lay.batch * seg, ATT_WIDTH), BF16),
        scratch_shapes=scratch,
        compiler_params=_cparams(("parallel", "parallel", "arbitrary")),
        name="diff_attn" if latent else "diff_attn_ctx",
    )(*([qkv] * (1 + len(kv_specs))), lam4, subln_g)


def _prep_kernel(rw_ref, rwp_ref, rwn_ref, cv_ref, cvp_ref, cvn_ref,
                 rkvw_ref, w0_ref, wd_ref, a0_ref, wa_ref, wg_ref, kk_w_ref, ka_ref, rk_ref, cw_ref,
                 r_ref, v_ref, kk_ref, lw_ref, bb_ref, kd_ref, bv_ref, g_ref, co_ref,
                 *, tm, n_lat, ctx_len, seq):
    i = pl.program_id(0)
    start = i * tm
    in_lat = start < n_lat
    seg_pos = jnp.where(in_lat, start % seq, (start - n_lat) % ctx_len)
    seg_len = jnp.where(in_lat, seq, ctx_len)
    has_prev = seg_pos != 0
    has_next = seg_pos + tm != seg_len

    def shifted(u, prev_row, next_row):
        rows = lax.broadcasted_iota(jnp.int32, u.shape, 0)
        prev_row = jnp.where(has_prev, prev_row, jnp.zeros_like(prev_row))
        next_row = jnp.where(has_next, next_row, jnp.zeros_like(next_row))
        up = jnp.where(rows == 0, prev_row, pltpu.roll(u, 1, 0))
        un = jnp.where(rows == tm - 1, next_row, pltpu.roll(u, tm - 1, 0))
        return up, un

    def conv3(u, prev_row, next_row, w):
        up, un = shifted(u, prev_row, next_row)
        return up * w[0:1] + u * w[1:2] + un * w[2:3]

    rw = rw_ref[...]
    nrkv = 3 * RWKV_WIDTH
    rkv = conv3(rw[:, :nrkv], rwp_ref[7:8, :nrkv], rwn_ref[0:1, :nrkv], rkvw_ref[...])
    r = rkv[:, :RWKV_WIDTH]
    k = rkv[:, RWKV_WIDTH:2 * RWKV_WIDTH]
    v = rkv[:, 2 * RWKV_WIDTH:]

    lora = rw[:, nrkv:nrkv + LANES]
    gate = rw[:, nrkv + LANES:]
    wl = _dot3(_split(jnp.tanh(lora)), _split(wd_ref[...])) + w0_ref[...]
    lw = (-math.exp(-0.5)) * jax.nn.sigmoid(wl)
    a = jax.nn.sigmoid(_dot3(_split(lora), _split(wa_ref[...])) + a0_ref[...])
    g = _dot3(_split(jax.nn.sigmoid(gate)), _split(wg_ref[...]))

    ones = _head_ones(RWKV_WIDTH, RWKV_N)
    kraw = k * kk_w_ref[...]
    ss = _dot_mask_rhs(kraw * kraw, ones)
    kk = kraw * lax.rsqrt(jnp.maximum(ss, 1e-24))
    k2 = jnp.concatenate([k, k], axis=1)
    ka2 = jnp.concatenate([ka_ref[...], ka_ref[...]], axis=1)
    kd = k2 * (1.0 + (a - 1.0) * ka2)
    bb = jnp.concatenate([kk, kk], axis=1) * a
    bonus = _dot_mask_rhs(r * (kd[:, :RWKV_WIDTH] + kd[:, RWKV_WIDTH:]) * rk_ref[...], ones)

    r_ref[...] = r
    v_ref[...] = v
    kk_ref[...] = kk
    lw_ref[...] = lw
    bb_ref[...] = bb
    kd_ref[...] = kd
    bv_ref[...] = bonus * v
    g_ref[...] = g

    def gated(ref):
        return ref[:, 2 * CONV_WIDTH:] * ref[:, :CONV_WIDTH]

    cv_u = gated(cv_ref)
    conv = conv3(cv_u, gated(cvp_ref)[7:8], gated(cvn_ref)[0:1], cw_ref[...])
    co_ref[...] = (cv_ref[:, CONV_WIDTH:2 * CONV_WIDTH] * conv).astype(co_ref.dtype)


def _split(a):
    hi = a.astype(BF16)
    return hi, (a - hi.astype(F32)).astype(BF16)


_NN = (((1,), (0,)), ((), ()))
_NT = (((1,), (1,)), ((), ()))
_TN = (((0,), (0,)), ((), ()))


def _dot3(a, b, dims=_NN):
    (ah, al), (bh, bl) = a, b
    dg = lambda p, q: lax.dot_general(p, q, dims, preferred_element_type=F32)
    return dg(ah, bh) + dg(ah, bl) + dg(al, bh)


def _dot1(a, b, dims=_NN):
    return lax.dot_general(a.astype(BF16), b.astype(BF16), dims, preferred_element_type=F32)


def _chunk_terms_kernel(r_ref, v_ref, kk_ref, lw_ref, bb_ref, kd_ref, g_ref, h_ref, rq_ref, y0_ref):
    c = SCAN_CHUNK
    n = RWKV_N
    row = lax.broadcasted_iota(jnp.int32, (c, c), 0)
    col = lax.broadcasted_iota(jnp.int32, (c, c), 1)
    row2 = lax.broadcasted_iota(jnp.int32, (c, 2 * c), 0)
    col2 = lax.broadcasted_iota(jnp.int32, (c, 2 * c), 1) % c
    eye = lax.broadcasted_iota(jnp.int32, (n, n), 0) == lax.broadcasted_iota(jnp.int32, (n, n), 1)
    n_chunks = r_ref.shape[0] // c

    chains = []
    for ck in range(n_chunks):
        rows = slice(ck * c, (ck + 1) * c)
        r_all = r_ref[rows, :]
        v_all = v_ref[rows, :]
        kk = kk_ref[rows, :]
        for d in range(2):
            sgn = 1 if d == 0 else -1
            dsl = slice(d * RWKV_WIDTH, (d + 1) * RWKV_WIDTH)
            incl = (col - row) * sgn <= 0
            lw = lw_ref[rows, dsl]
            bb = bb_ref[rows, dsl]
            kd = kd_ref[rows, dsl]
            lp = _dot_mask_lhs(incl, lw)
            lt = jnp.sum(lw, axis=0, keepdims=True)
            p_inv = jnp.exp(-lp)
            p_end = jnp.exp(lt - lp)
            a_t = -kk * jnp.exp(lp - lw)
            b_t = bb * p_inv
            k_t = kd * p_inv
            r_t = r_all * jnp.exp(lp)
            k_e = kd * p_end
            b_e = bb * p_end
            p_tot = jnp.exp(lt)
            for h in range(RWKV_HEADS):
                sl = slice(h * n, (h + 1) * n)
                chains.append(dict(
                    strict=(col - row) * sgn < 0, incl2=(col2 - row2) * sgn <= 0,
                    ah=a_t[:, sl], rh=r_t[:, sl], vh=v_all[:, sl], bt=b_t[:, sl], kt=k_t[:, sl],
                    be=b_e[:, sl], ke=k_e[:, sl], ptot=p_tot[:, sl]))

    for ch in chains:
        ch["sc"] = _dot3(_split(jnp.concatenate([ch["ah"], ch["rh"]], axis=0)),
                         _split(jnp.concatenate([ch["bt"], ch["kt"]], axis=0)), _NT)
    for ch in chains:
        sc = ch["sc"]
        ch["nmat"] = jnp.where(ch["strict"], sc[:c, :c], 0.0).astype(BF16)
        ch["mr"] = jnp.where(ch["incl2"], sc[c:, :], 0.0).astype(BF16)
        ch["mkv"] = _dot1(jnp.where(ch["strict"], sc[:c, c:], 0.0), ch["vh"])
    for ch in chains:
        ch["z"] = jnp.concatenate([ch["ah"], ch["mkv"]], axis=1)
    n_factors = c.bit_length() - 1
    for p in range(n_factors):
        for ch in chains:
            ch["z"] = ch["z"] + _dot1(ch["nmat"], ch["z"])
        if p < n_factors - 1:
            for ch in chains:
                ch["nmat"] = _dot1(ch["nmat"], ch["nmat"]).astype(BF16)
    for ch in chains:
        z = ch["z"]
        ch["ws"] = z[:, :n].astype(BF16)
        ch["uv"] = jnp.concatenate([z[:, n:], ch["vh"]], axis=0).astype(BF16)
    for ch in chains:
        mr = ch["mr"]
        ch["rq"] = ch["rh"] + _dot1(mr[:, :c], ch["ws"])
        ch["y0"] = _dot1(mr, ch["uv"])
        ch["g"] = jnp.where(eye, ch["ptot"], 0.0) + _dot1(ch["ws"], ch["be"], _TN)
        ch["h"] = _dot1(ch["uv"], jnp.concatenate([ch["be"], ch["ke"]], axis=0), _TN)
    for ck in range(n_chunks):
        rows = slice(ck * c, (ck + 1) * c)
        for d in range(2):
            first = (ck * 2 + d) * RWKV_HEADS
            part = chains[first:first + RWKV_HEADS]
            for ref, key in ((g_ref, "g"), (h_ref, "h")):
                top = jnp.concatenate([ch[key] for ch in part], axis=1)
                if c > n:
                    top = jnp.concatenate([top, jnp.zeros((c - n, RWKV_WIDTH), F32)], axis=0)
                ref[d, rows, :] = top
            rq_ref[d, rows, :] = jnp.concatenate([ch["rq"] for ch in part], axis=1)
            y0_ref[d, rows, :] = jnp.concatenate([ch["y0"] for ch in part], axis=1)


def _prep_chunk_kernel(*refs, tm, n_lat, ctx_len, seq):
    ins, (bv_ref, g_ref, co_ref), terms, scratch = refs[:16], refs[16:19], refs[19:23], refs[23:]
    _prep_kernel(*ins, *scratch, bv_ref, g_ref, co_ref, tm=tm, n_lat=n_lat, ctx_len=ctx_len, seq=seq)
    _chunk_terms_kernel(*scratch, *terms)


def _prep_chunk_terms(lay, layer, rw, cv, rkv_conv, w0, wd, a0, wa, wg, k_k, k_a, r_k, conv_w, tm):
    n8 = lay.n_tok // 8
    t8 = tm // 8
    row = lambda i: (i, 0)
    prev = lambda i: (jnp.maximum(i * t8 - 1, 0), 0)
    nxt = lambda i: (jnp.minimum((i + 1) * t8, n8 - 1), 0)
    lmap = lambda i: (layer, 0, 0)
    w2 = 2 * RWKV_WIDTH
    kern = functools.partial(_prep_chunk_kernel, tm=tm, n_lat=lay.n_lat, ctx_len=lay.ctx_len, seq=lay.seq)
    f32 = lambda w: jax.ShapeDtypeStruct((lay.n_tok, w), F32)
    term = pl.BlockSpec((2, tm, RWKV_WIDTH), lambda i: (0, i, 0))
    term_shape = jax.ShapeDtypeStruct((2, lay.n_tok, RWKV_WIDTH), F32)
    return pl.pallas_call(
        kern,
        grid=(lay.n_tok // tm,),
        in_specs=[
            pl.BlockSpec((tm, RW_W), row), pl.BlockSpec((8, RW_W), prev), pl.BlockSpec((8, RW_W), nxt),
            pl.BlockSpec((tm, CV_W), row), pl.BlockSpec((8, CV_W), prev), pl.BlockSpec((8, CV_W), nxt),
            pl.BlockSpec((None, 3, 3 * RWKV_WIDTH), lmap),
            pl.BlockSpec((None, 1, w2), lmap),
            pl.BlockSpec((None, LANES, w2), lmap),
            pl.BlockSpec((None, 1, w2), lmap),
            pl.BlockSpec((None, LANES, w2), lmap),
            pl.BlockSpec((None, LANES, RWKV_WIDTH), lmap),
            pl.BlockSpec((None, 1, RWKV_WIDTH), lmap),
            pl.BlockSpec((None, 1, RWKV_WIDTH), lmap),
            pl.BlockSpec((None, 1, RWKV_WIDTH), lmap),
            pl.BlockSpec((None, 3, CONV_WIDTH), lmap),
        ],
        out_specs=[pl.BlockSpec((tm, RWKV_WIDTH), row), pl.BlockSpec((tm, RWKV_WIDTH), row),
                   pl.BlockSpec((tm, CONV_WIDTH), row), term, term, term, term],
        out_shape=[f32(RWKV_WIDTH), f32(RWKV_WIDTH), jax.ShapeDtypeStruct((lay.n_tok, CONV_WIDTH), BF16),
                   term_shape, term_shape, term_shape, term_shape],
        scratch_shapes=[pltpu.VMEM((tm, RWKV_WIDTH), F32)] * 3 + [pltpu.VMEM((tm, w2), F32)] * 3,
        compiler_params=_cparams(("parallel",)),
        name="rwkv_prep_chunk_terms",
    )(rw, rw, rw, cv, cv, cv, rkv_conv, w0, wd, a0, wa, wg, k_k, k_a, r_k, conv_w)


def _scan_kernel(gf_ref, hf_ref, rqf_ref, y0f_ref, gb_ref, hb_ref, rqb_ref, y0b_ref, yf_ref, yb_ref, st_ref):
    @pl.when(pl.program_id(1) == 0)
    def _():
        st_ref[...] = jnp.zeros_like(st_ref)

    dirs = ((gf_ref, hf_ref, rqf_ref, y0f_ref, yf_ref), (gb_ref, hb_ref, rqb_ref, y0b_ref, yb_ref))
    heads = [slice(h * RWKV_N, (h + 1) * RWKV_N) for h in range(RWKV_HEADS)]
    n_sub = gf_ref.shape[0] // SCAN_CHUNK
    state = [[st_ref[d, h] for h in range(RWKV_HEADS)] for d in range(2)]
    for step in range(n_sub):
        rows = [slice(k * SCAN_CHUNK, (k + 1) * SCAN_CHUNK) for k in (step, n_sub - 1 - step)]
        top = [slice(k * SCAN_CHUNK, k * SCAN_CHUNK + RWKV_N) for k in (step, n_sub - 1 - step)]
        s0 = [[_split(state[d][h]) for h in range(RWKV_HEADS)] for d in range(2)]
        state = [[_dot1(s0[d][h][0], dirs[d][0][top[d], sl]) + _dot1(s0[d][h][1], dirs[d][0][top[d], sl])
                  + dirs[d][1][top[d], sl] for h, sl in enumerate(heads)] for d in range(2)]
        for d in range(2):
            rq_ref, y0_ref, y_ref = dirs[d][2:]
            y_ref[rows[d], :] = jnp.concatenate(
                [_dot1(rq_ref[rows[d], sl], s0[d][h][0], _NT) + y0_ref[rows[d], sl]
                 for h, sl in enumerate(heads)], axis=1)
    for d in range(2):
        for h in range(RWKV_HEADS):
            st_ref[d, h] = state[d][h]


def _scan(lay, g, h, rq, y0, chunks_per_step):
    c = SCAN_CHUNK * chunks_per_step
    nc_ctx = lay.ctx_len // c
    nc_lat = lay.seq // c
    lat_blocks = lay.n_lat // c

    def blk(b, d, ci):
        if d == 0:
            return jnp.where(ci < nc_ctx, lat_blocks + b * nc_ctx + ci, b * nc_lat + (ci - nc_ctx))
        return jnp.where(ci < nc_ctx, lat_blocks + b * nc_ctx + (nc_ctx - 1 - ci),
                         b * nc_lat + (nc_lat - 1 - (ci - nc_ctx)))

    def term(d):
        return pl.BlockSpec((None, c, RWKV_WIDTH), lambda b, ci: (d, blk(b, d, ci), 0))

    def yspec(d):
        return pl.BlockSpec((c, RWKV_WIDTH), lambda b, ci: (blk(b, d, ci), 0))

    shp = jax.ShapeDtypeStruct((lay.n_tok, RWKV_WIDTH), F32)
    return pl.pallas_call(
        _scan_kernel,
        grid=(lay.batch, nc_ctx + nc_lat),
        in_specs=[term(0)] * 4 + [term(1)] * 4,
        out_specs=[yspec(0), yspec(1)],
        out_shape=[shp, shp],
        scratch_shapes=[pltpu.VMEM((2, RWKV_HEADS, RWKV_N, RWKV_N), F32)],
        compiler_params=_cparams(("parallel", "arbitrary")),
        name="rwkv_scan",
    )(g, h, rq, y0, g, h, rq, y0)


def _outproj_kernel(att_lat_ref, att_ctx_ref, yf_ref, yb_ref, bv_ref, g_ref, co_ref, x_ref, gate_ref, w_ref,
                    gng_ref, gnb_ref, lng_ref, lnb_ref, o_ref, *, n_lat_tiles):
    att = jnp.where(pl.program_id(0) < n_lat_tiles, att_lat_ref[...], att_ctx_ref[...])
    ones = _head_ones(RWKV_WIDTH, RWKV_N)
    y = yf_ref[...] + yb_ref[...]
    mu = _dot_mask_rhs(y, ones) * (1.0 / RWKV_N)
    yc = y - mu
    var = _dot_mask_rhs(yc * yc, ones) * (1.0 / RWKV_N)
    yn = yc * lax.rsqrt(var + GN_EPS) * gng_ref[...] + gnb_ref[...]
    rwkv = ((yn + bv_ref[...]) * g_ref[...]).astype(BF16)
    mix = (_dot(att, w_ref[:ATT_WIDTH, :])
           + _dot(rwkv, w_ref[ATT_WIDTH:ATT_WIDTH + RWKV_WIDTH, :])
           + _dot(co_ref[...], w_ref[ATT_WIDTH + RWKV_WIDTH:, :]))
    z = DN_ALPHA * x_ref[...] + gate_ref[...] * mix
    o_ref[...] = _layer_norm(z, lng_ref[...], lnb_ref[...])


def _outproj(lay, layer, att_lat, att_ctx, yf, yb, bv, g, co, x, mods, w_out, gn_g, gn_b, ln_g, ln_b, tm, n_rows):
    row = lambda i: (i, 0)
    lmap = lambda i: (layer, 0, 0)
    n_lat_tiles = lay.n_lat // tm
    return pl.pallas_call(
        functools.partial(_outproj_kernel, n_lat_tiles=n_lat_tiles),
        grid=(n_rows // tm,),
        in_specs=[
            pl.BlockSpec((tm, ATT_WIDTH), lambda i: (jnp.minimum(i, n_lat_tiles - 1), 0)),
            pl.BlockSpec((tm, ATT_WIDTH), lambda i: (jnp.maximum(i - n_lat_tiles, 0), 0)),
            pl.BlockSpec((tm, RWKV_WIDTH), row),
            pl.BlockSpec((tm, RWKV_WIDTH), row),
            pl.BlockSpec((tm, RWKV_WIDTH), row),
            pl.BlockSpec((tm, RWKV_WIDTH), row),
            pl.BlockSpec((tm, CONV_WIDTH), row),
            pl.BlockSpec((tm, D_MODEL), row),
            lay.mod_spec(layer, tm, 2),
            pl.BlockSpec((None, D_MODEL, D_MODEL), lmap),
            pl.BlockSpec((None, 1, RWKV_WIDTH), lmap),
            pl.BlockSpec((None, 1, RWKV_WIDTH), lmap),
            pl.BlockSpec((None, 1, D_MODEL), lmap),
            pl.BlockSpec((None, 1, D_MODEL), lmap),
        ],
        out_specs=pl.BlockSpec((tm, D_MODEL), row),
        out_shape=jax.ShapeDtypeStruct((n_rows, D_MODEL), F32),
        compiler_params=_cparams(("parallel",)),
        name="outproj_ln1",
    )(att_lat, att_ctx, yf, yb, bv, g, co, x, mods, w_out, gn_g, gn_b, ln_g, ln_b)


def _ffn_kernel(x_ref, sh_ref, sc_ref, gate_ref, w1_ref, w3_ref, w2_ref, lng_ref, lnb_ref, o_ref, *, n_split):
    x = x_ref[...]
    h = (x * (1.0 + sc_ref[...]) + sh_ref[...]).astype(BF16)
    step = D_FF // n_split
    f = jnp.zeros(x.shape, F32)
    for s in range(n_split):
        cols = slice(s * step, (s + 1) * step)
        a = _dot(h, w1_ref[:, cols])
        act = (a * jax.nn.sigmoid(a) * _dot(h, w3_ref[:, cols])).astype(BF16)
        f = f + _dot(act, w2_ref[cols, :])
    z = DN_ALPHA * x + gate_ref[...] * f
    o_ref[...] = _layer_norm(z, lng_ref[...], lnb_ref[...])


def _ffn(lay, layer, j, x, mods, w1, w3, w2, ln_g, ln_b, tm):
    row = lambda i: (i, 0)
    once = pl.Buffered(1)
    return pl.pallas_call(
        functools.partial(_ffn_kernel, n_split=2),
        grid=(lay.n_tok // tm,),
        in_specs=[
            pl.BlockSpec((tm, D_MODEL), row),
            lay.mod_spec(layer, tm, 3), lay.mod_spec(layer, tm, 4), lay.mod_spec(layer, tm, 5),
            pl.BlockSpec((None, D_MODEL, D_FF), lambda i: (j, 0, 0), pipeline_mode=once),
            pl.BlockSpec((None, D_MODEL, D_FF), lambda i: (j, 0, 0), pipeline_mode=once),
            pl.BlockSpec((None, D_FF, D_MODEL), lambda i: (j, 0, 0), pipeline_mode=once),
            pl.BlockSpec((None, 1, D_MODEL), lambda i: (layer, 0, 0)),
            pl.BlockSpec((None, 1, D_MODEL), lambda i: (layer, 0, 0)),
        ],
        out_specs=pl.BlockSpec((tm, D_MODEL), row),
        out_shape=jax.ShapeDtypeStruct((lay.n_tok, D_MODEL), F32),
        compiler_params=_cparams(("parallel",)),
        name="ffn_ln2",
    )(x, mods, mods, mods, w1, w3, w2, ln_g, ln_b)


def _router_kernel(x_ref, sh_ref, sc_ref, rw_ref, rb_ref, tri_ref, g_ref, rk_ref, rkt_ref, cnt_ref):
    h = x_ref[...] * (1.0 + sc_ref[...]) + sh_ref[...]
    logits = _dot3(_split(h), _split(rw_ref[...])) + rb_ref[...]
    lane = lax.broadcasted_iota(jnp.int32, logits.shape, 1)
    m1 = jnp.max(logits, axis=-1, keepdims=True)
    i1 = jnp.min(jnp.where(logits == m1, lane, LANES), axis=-1, keepdims=True)
    rest = jnp.where(lane == i1, -jnp.inf, logits)
    m2 = jnp.max(rest, axis=-1, keepdims=True)
    i2 = jnp.min(jnp.where(rest == m2, lane, LANES), axis=-1, keepdims=True)
    e2 = jnp.exp(m2 - m1)
    den = 1.0 + e2
    g_ref[...] = jnp.where(lane == i1, 1.0 / den, 0.0) + jnp.where(lane == i2, e2 / den, 0.0)
    sel = jnp.where(lane == i1, 1.0, 0.0) + jnp.where(lane == i2, 1.0, 0.0)
    rank = _dot(tri_ref[...], sel.astype(BF16))
    rk = jnp.where(sel > 0.0, rank, -1.0)
    rk_ref[...] = rk
    rkt_ref[...] = jnp.transpose(rk)[:N_EXPERTS, :]
    cnt = jnp.sum(sel, axis=0, keepdims=True).astype(jnp.int32)
    cnt_ref[...] = jnp.broadcast_to(cnt, cnt_ref.shape)


def _router(lay, layer, j, x, mods, router_w, router_b, tri, tm, n_rows):
    n_tiles = n_rows // tm
    row = lambda i: (i, 0)
    return pl.pallas_call(
        _router_kernel,
        grid=(n_tiles,),
        in_specs=[
            pl.BlockSpec((tm, D_MODEL), row),
            lay.mod_spec(layer, tm, 3), lay.mod_spec(layer, tm, 4),
            pl.BlockSpec((None, D_MODEL, LANES), lambda i: (j, 0, 0)),
            pl.BlockSpec((None, 1, LANES), lambda i: (j, 0, 0)),
            pl.BlockSpec((tm, tm), lambda i: (0, 0)),
        ],
        out_specs=[
            pl.BlockSpec((tm, LANES), row),
            pl.BlockSpec((tm, LANES), row),
            pl.BlockSpec((None, N_EXPERTS, tm), lambda i: (i, 0, 0)),
            pl.BlockSpec((None, 8, LANES), lambda i: (i, 0, 0)),
        ],
        out_shape=[
            jax.ShapeDtypeStruct((n_rows, LANES), F32),
            jax.ShapeDtypeStruct((n_rows, LANES), F32),
            jax.ShapeDtypeStruct((n_tiles, N_EXPERTS, tm), F32),
            jax.ShapeDtypeStruct((n_tiles, 8, LANES), jnp.int32),
        ],
        compiler_params=_cparams(("parallel",)),
        name="moe_router",
    )(x, mods, mods, router_w, router_b, tri)


def _moe_kernel(cnt_ref, x_ref, sh_ref, sc_ref, gate_ref, g_ref, rk_ref, rkt_ref, w1_ref, w3_ref, w2_ref,
                lng_ref, lnb_ref, o_ref, h_ref, acc_ref, *, blk):
    i = pl.program_id(0)
    e = pl.program_id(1)
    tm = x_ref.shape[0]

    @pl.when(e == 0)
    def _():
        h_ref[...] = (x_ref[...] * (1.0 + sc_ref[...]) + sh_ref[...]).astype(BF16)
        acc_ref[...] = jnp.zeros_like(acc_ref)

    lane = lax.broadcasted_iota(jnp.int32, (tm, LANES), 1)
    rk_col = jnp.sum(jnp.where(lane == e, rk_ref[...], 0.0), axis=-1, keepdims=True)
    rk_row = rkt_ref[pl.ds(e, 1), :]
    g_parts = _split(g_ref[...])

    pos_r = lax.broadcasted_iota(jnp.int32, (blk, tm), 0).astype(F32)
    pos_c = lax.broadcasted_iota(jnp.int32, (tm, blk), 1).astype(F32)
    lane_b = lax.broadcasted_iota(jnp.int32, (blk, LANES), 1)

    def block(jb, carry):
        base = (jb * blk).astype(F32)
        take = jnp.where(rk_row - base == pos_r, 1.0, 0.0).astype(BF16)
        put = jnp.where(rk_col - base == pos_c, 1.0, 0.0).astype(BF16)
        hg = _dot(take, h_ref[...]).astype(BF16)
        a = _dot(hg, w1_ref[...])
        act = (a * jax.nn.sigmoid(a) * _dot(hg, w3_ref[...])).astype(BF16)
        f = _dot(act, w2_ref[...])
        gg = _dot(take, g_parts[0]) + _dot(take, g_parts[1])
        ge = jnp.sum(jnp.where(lane_b == e, gg, 0.0), axis=-1, keepdims=True)
        acc_ref[...] += _dot(put, (ge * f).astype(BF16))
        return carry

    n_routed = cnt_ref[i * N_EXPERTS + e]
    lax.fori_loop(0, (n_routed + blk - 1) // blk, block, 0)

    @pl.when(e == N_EXPERTS - 1)
    def _():
        z = DN_ALPHA * x_ref[...] + gate_ref[...] * acc_ref[...]
        o_ref[...] = _layer_norm(z, lng_ref[...], lnb_ref[...])


def _moe(lay, layer, j, x, mods, counts, g, rk, rkt, w1, w3, w2, ln_g, ln_b, tm, blk, n_rows):
    row = lambda i, e, cnt: (i, 0)
    grid_spec = pltpu.PrefetchScalarGridSpec(
        num_scalar_prefetch=1,
        grid=(n_rows // tm, N_EXPERTS),
        in_specs=[
            pl.BlockSpec((tm, D_MODEL), row),
            lay.mod_spec(layer, tm, 3), lay.mod_spec(layer, tm, 4), lay.mod_spec(layer, tm, 5),
            pl.BlockSpec((tm, LANES), row),
            pl.BlockSpec((tm, LANES), row),
            pl.BlockSpec((None, N_EXPERTS, tm), lambda i, e, cnt: (i, 0, 0)),
            pl.BlockSpec((None, None, D_MODEL, D_FF_EXPERT), lambda i, e, cnt: (j, e, 0, 0)),
            pl.BlockSpec((None, None, D_MODEL, D_FF_EXPERT), lambda i, e, cnt: (j, e, 0, 0)),
            pl.BlockSpec((None, None, D_FF_EXPERT, D_MODEL), lambda i, e, cnt: (j, e, 0, 0)),
            pl.BlockSpec((None, 1, D_MODEL), lambda i, e, cnt: (layer, 0, 0)),
            pl.BlockSpec((None, 1, D_MODEL), lambda i, e, cnt: (layer, 0, 0)),
        ],
        out_specs=pl.BlockSpec((tm, D_MODEL), row),
        scratch_shapes=[pltpu.VMEM((tm, D_MODEL), BF16), pltpu.VMEM((tm, D_MODEL), F32)],
    )
    return pl.pallas_call(
        functools.partial(_moe_kernel, blk=blk),
        grid_spec=grid_spec,
        out_shape=jax.ShapeDtypeStruct((n_rows, D_MODEL), F32),
        compiler_params=_cparams(("parallel", "arbitrary")),
        name="moe_ln2",
    )(counts, x, mods, mods, mods, g, rk, rkt, w1, w3, w2, ln_g, ln_b)


def _rope_tables(ctx_len, seq):
    t = np.arange(seq)
    inv = ROPE_THETA ** (-np.arange(0, AXIS_DIM, 2, dtype=np.float64) / AXIS_DIM)
    ang_r = (t // GRID_W)[:, None].astype(np.float64) * inv
    ang_c = (t % GRID_W)[:, None].astype(np.float64) * inv
    cos = np.concatenate([np.cos(ang_r), np.cos(ang_r), np.cos(ang_c), np.cos(ang_c)], axis=1)
    sin = np.concatenate([-np.sin(ang_r), np.sin(ang_r), -np.sin(ang_c), np.sin(ang_c)], axis=1)
    cos = np.concatenate([cos, np.ones((ctx_len, ATT_QK))], axis=0)
    sin = np.concatenate([sin, np.zeros((ctx_len, ATT_QK))], axis=0)
    reps = LANES // ATT_QK
    return (jnp.asarray(np.tile(cos, (1, reps)), F32), jnp.asarray(np.tile(sin, (1, reps)), F32))


def _tiles(batch, ctx_len, seq):
    tm = 4 * SCAN_CHUNK
    tq = 512
    tq_ctx = 256
    tk = next(t for t in (1024, 512, 256) if seq % t == 0)
    tm_moe = next(t for t in (1024, 512, 256) if (batch * ctx_len) % t == 0 and seq % t == 0)
    moe_blk = LANES
    scan_chunks_per_step = 4
    assert ctx_len % tm == 0 and seq % tm == 0 and ctx_len % tq_ctx == 0 and seq % tq == 0 and seq % tk == 0
    assert (batch * seq) % tq_ctx == 0
    assert seq % ctx_len == 0 and ctx_len % (SCAN_CHUNK * scan_chunks_per_step) == 0 and seq % GRID_W == 0
    return tm, tq, tq_ctx, tk, tm_moe, moe_blk, scan_chunks_per_step


def kernel(x, c, ctx, c_ctx, w_ada, b_ada, w_in, w_out, ln1_g, ln1_b, ln2_g, ln2_b,
           lam_q1, lam_k1, lam_q2, lam_k2, subln_g, rkv_conv, decay_w0, decay_up, iclr_a0, iclr_up,
           gate_up, k_k, k_a, r_k, gn_g, gn_b, conv_w, ffn_w1, ffn_w3, ffn_w2,
           router_w, router_b, moe_w1, moe_w3, moe_w2):
    batch, seq, d = x.shape
    ctx_len = ctx.shape[1]
    depth = w_in.shape[0]
    assert d == D_MODEL and depth == DEPTH and batch < MOD_ROWS
    tm, tq, tq_ctx, tk, tm_moe, moe_blk, scan_chunks_per_step = _tiles(batch, ctx_len, seq)
    tri = jnp.asarray(np.tri(tm_moe, k=-1), BF16)
    lay = _Layout(batch, ctx_len, seq)

    tokens = jnp.concatenate([x.reshape(batch * seq, d), ctx.reshape(batch * ctx_len, d)], axis=0)

    c_all = jnp.concatenate([c, c_ctx[None, :], jnp.zeros((MOD_ROWS - batch - 1, d), F32)], axis=0)
    mods = _adaln(c_all, w_ada, b_ada).reshape(depth * MOD_ROWS, 1, 6 * d)

    cos, sin = _rope_tables(ctx_len, seq)

    o = np.cumsum([0, 512, 512, 512, 256, 256, 256, 64, 64, 64, 256, 256, 256])
    wq = w_in[:, :, :o[3]].astype(BF16)
    wr = jnp.concatenate([w_in[:, :, o[3]:o[9]], jnp.zeros((depth, d, RW_W - (o[9] - o[3])), F32)],
                         axis=-1).astype(BF16)
    wc = w_in[:, :, o[9]:].astype(BF16)
    w_out_b = w_out.astype(BF16)

    z = jnp.zeros((depth, DECAY_LORA, RWKV_WIDTH), F32)
    wd = jnp.concatenate([
        jnp.concatenate([decay_up[:, 0], z], axis=-1),
        jnp.concatenate([z, decay_up[:, 1]], axis=-1),
        jnp.zeros((depth, 2 * ICLR_LORA, 2 * RWKV_WIDTH), F32)], axis=1)
    wa = jnp.concatenate([
        jnp.zeros((depth, 2 * DECAY_LORA, 2 * RWKV_WIDTH), F32),
        jnp.concatenate([iclr_up[:, 0], z], axis=-1),
        jnp.concatenate([z, iclr_up[:, 1]], axis=-1)], axis=1)
    wg = jnp.concatenate([gate_up, jnp.zeros((depth, LANES - GATE_LORA, RWKV_WIDTH), F32)], axis=1)
    w0 = decay_w0.reshape(depth, 1, 2 * RWKV_WIDTH)
    a0 = iclr_a0.reshape(depth, 1, 2 * RWKV_WIDTH)
    vec = lambda w: w.reshape(depth, 1, -1)

    lam4 = jnp.stack([lam_q1, lam_k1, lam_q2, lam_k2], axis=1)

    ffn_w1_b, ffn_w3_b, ffn_w2_b = ffn_w1.astype(BF16), ffn_w3.astype(BF16), ffn_w2.astype(BF16)
    moe_w1_b, moe_w3_b, moe_w2_b = moe_w1.astype(BF16), moe_w3.astype(BF16), moe_w2.astype(BF16)
    n_moe = router_w.shape[0]
    router_w_p = jnp.concatenate([router_w, jnp.zeros((n_moe, d, LANES - N_EXPERTS), F32)], axis=-1)
    router_b_p = jnp.concatenate([router_b, jnp.full((n_moe, LANES - N_EXPERTS), -1e30, F32)],
                                 axis=-1).reshape(n_moe, 1, LANES)

    xs = tokens
    for l in range(depth):
        lam_init = 0.8 - 0.6 * math.exp(-0.3 * l)
        qkv, rw, cv = _inproj(lay, l, xs, mods, wq, wr, wc, cos, sin, tm)
        last = l == depth - 1
        att_lat = _attention(lay, l, qkv, lam4, vec(subln_g), lam_init, tq, tk, latent=True)
        att_ctx = att_lat if last else _attention(lay, l, qkv, lam4, vec(subln_g), lam_init,
                                                  tq_ctx, tk, latent=False)
        bv, g, co, *terms = _prep_chunk_terms(lay, l, rw, cv, rkv_conv, w0, wd, a0, wa, wg,
                                              vec(k_k), vec(k_a), vec(r_k), conv_w, tm)
        yf, yb = _scan(lay, *terms, scan_chunks_per_step)
        last_moe = l == depth - 1 and l % 2 == 1
        n_rows = lay.n_lat if last_moe else lay.n_tok
        xs = _outproj(lay, l, att_lat, att_ctx, yf, yb, bv, g, co, xs, mods, w_out_b, vec(gn_g), vec(gn_b),
                      vec(ln1_g), vec(ln1_b), tm, n_rows)
        if l % 2 == 0:
            xs = _ffn(lay, l, l // 2, xs, mods, ffn_w1_b, ffn_w3_b, ffn_w2_b, vec(ln2_g), vec(ln2_b), tm)
        else:
            g_tok, rk, rkt, cnt = _router(lay, l, l // 2, xs, mods, router_w_p, router_b_p, tri, tm_moe,
                                          n_rows)
            counts = cnt[:, 0, :N_EXPERTS].reshape(-1)
            xs = _moe(lay, l, l // 2, xs, mods, counts, g_tok, rk, rkt, moe_w1_b, moe_w3_b, moe_w2_b,
                      vec(ln2_g), vec(ln2_b), tm_moe, moe_blk, n_rows)
    return xs[:lay.n_lat].reshape(batch, seq, d)
```

```python
import functools
import math

import numpy as np
import jax
import jax.numpy as jnp
from jax import lax
from jax.experimental import pallas as pl
from jax.experimental.pallas import tpu as pltpu

F32 = jnp.float32
BF16 = jnp.bfloat16

D_MODEL = 1024
DEPTH = 4
GRID_W = 64
ATT_HEADS = 4
ATT_QK = 64
ATT_V = 128
ATT_WIDTH = 512
AXIS_DIM = 32
ROPE_THETA = 10000.0
SUBLN_EPS = 1e-5
RWKV_HEADS = 4
RWKV_N = 64
RWKV_WIDTH = 256
DECAY_LORA = 32
ICLR_LORA = 32
GATE_LORA = 64
GN_EPS = 64e-5
CONV_WIDTH = 256
D_FF = 2816
N_EXPERTS = 8
D_FF_EXPERT = 1408
DN_ALPHA = (2 * DEPTH) ** 0.25
LN_EPS = 1e-5

LANES = 128
MOD_ROWS = 16
QKV_W = 3 * ATT_WIDTH
RW_W = 1024
CV_W = 3 * CONV_WIDTH
SCAN_CHUNK = 64
VMEM_LIMIT = 56 * 1024 * 1024


def _cparams(sem):
    return pltpu.CompilerParams(dimension_semantics=sem, vmem_limit_bytes=VMEM_LIMIT)


def _dot(a, b):
    return jnp.dot(a, b, preferred_element_type=F32)


def _split3(a):
    hi = a.astype(BF16)
    rest = a - hi.astype(F32)
    mid = rest.astype(BF16)
    return hi, mid, (rest - mid.astype(F32)).astype(BF16)


def _dot_mask_lhs(mask, a):
    m = jnp.where(mask, 1.0, 0.0).astype(BF16)
    return sum(jnp.dot(m, part, preferred_element_type=F32) for part in _split3(a))


def _dot_mask_rhs(a, mask):
    m = mask.astype(BF16)
    return sum(jnp.dot(part, m, preferred_element_type=F32) for part in _split3(a))


def _layer_norm(z, g, b):
    mu = jnp.mean(z, axis=-1, keepdims=True)
    zc = z - mu
    var = jnp.mean(zc * zc, axis=-1, keepdims=True)
    return zc * lax.rsqrt(var + LN_EPS) * g + b


def _head_ones(width, head):
    r = lax.broadcasted_iota(jnp.int32, (width, width), 0) // head
    c = lax.broadcasted_iota(jnp.int32, (width, width), 1) // head
    return (r == c).astype(F32)


def _ada_kernel(c_ref, w_ref, b_ref, o_ref):
    c = c_ref[...]
    sc = c * jax.nn.sigmoid(c)
    o_ref[...] = _dot(sc.astype(BF16), w_ref[...].astype(BF16)) + b_ref[...]


def _adaln(c_all, w_ada, b_ada):
    depth, d, n = w_ada.shape
    tn = 1536
    return pl.pallas_call(
        _ada_kernel,
        grid=(depth, n // tn),
        in_specs=[
            pl.BlockSpec((MOD_ROWS, d), lambda l, j: (0, 0)),
            pl.BlockSpec((None, d, tn), lambda l, j: (l, 0, j)),
            pl.BlockSpec((None, 1, tn), lambda l, j: (l, 0, j)),
        ],
        out_specs=pl.BlockSpec((None, MOD_ROWS, tn), lambda l, j: (l, 0, j)),
        out_shape=jax.ShapeDtypeStruct((depth, MOD_ROWS, n), F32),
        compiler_params=_cparams(("parallel", "parallel")),
        name="adaln",
    )(c_all, w_ada, b_ada.reshape(depth, 1, n))


class _Layout:
    def __init__(self, batch, ctx_len, seq):
        self.batch, self.ctx_len, self.seq = batch, ctx_len, seq
        self.n_lat = batch * seq
        self.n_tok = self.n_lat + batch * ctx_len

    def mod_spec(self, layer, tm, which):
        n_lat_tiles = self.n_lat // tm
        rows_per = self.seq // tm
        batch = self.batch

        def imap(i, *_):
            row = jnp.where(i < n_lat_tiles, i // rows_per, batch)
            return (layer * MOD_ROWS + row, 0, which)

        return pl.BlockSpec((None, 1, D_MODEL), imap)


def _inproj_kernel(x_ref, sh_ref, sc_ref, wq_ref, wr_ref, wc_ref, cos_ref, sin_ref,
                   oq_ref, or_ref, oc_ref):
    h = (x_ref[...] * (1.0 + sc_ref[...]) + sh_ref[...]).astype(BF16)
    qkv = _dot(h, wq_ref[...])
    cos = cos_ref[...]
    sin = sin_ref[...]
    lane = lax.broadcasted_iota(jnp.int32, cos.shape, 1)
    first_half = (lane % AXIS_DIM) < (AXIS_DIM // 2)
    for s in range(2 * ATT_WIDTH // LANES):
        seg = qkv[:, s * LANES:(s + 1) * LANES]
        swapped = jnp.where(first_half, pltpu.roll(seg, LANES - AXIS_DIM // 2, 1),
                            pltpu.roll(seg, AXIS_DIM // 2, 1))
        rot = seg * cos + swapped * sin
        if s < ATT_WIDTH // LANES:
            rot = rot * (ATT_QK ** -0.5 * math.log2(math.e))
        oq_ref[:, s * LANES:(s + 1) * LANES] = rot.astype(BF16)
    oq_ref[:, 2 * ATT_WIDTH:] = qkv[:, 2 * ATT_WIDTH:].astype(BF16)
    or_ref[...] = _dot(h, wr_ref[...])
    oc_ref[...] = _dot(h, wc_ref[...])


def _inproj(lay, layer, x, mods, wq, wr, wc, cos, sin, tm):
    n_lat_tiles = lay.n_lat // tm
    ctx_tiles_per = lay.ctx_len // tm
    lat_tiles_per = lay.seq // tm

    def rope_map(i):
        return (jnp.where(i < n_lat_tiles, i % lat_tiles_per,
                          lat_tiles_per + (i - n_lat_tiles) % ctx_tiles_per), 0)

    row = lambda i: (i, 0)
    wmap = lambda i: (layer, 0, 0)
    return pl.pallas_call(
        _inproj_kernel,
        grid=(lay.n_tok // tm,),
        in_specs=[
            pl.BlockSpec((tm, D_MODEL), row),
            lay.mod_spec(layer, tm, 0),
            lay.mod_spec(layer, tm, 1),
            pl.BlockSpec((None, D_MODEL, QKV_W), wmap),
            pl.BlockSpec((None, D_MODEL, RW_W), wmap),
            pl.BlockSpec((None, D_MODEL, CV_W), wmap),
            pl.BlockSpec((tm, LANES), rope_map),
            pl.BlockSpec((tm, LANES), rope_map),
        ],
        out_specs=[
            pl.BlockSpec((tm, QKV_W), row),
            pl.BlockSpec((tm, RW_W), row),
            pl.BlockSpec((tm, CV_W), row),
        ],
        out_shape=[
            jax.ShapeDtypeStruct((lay.n_tok, QKV_W), BF16),
            jax.ShapeDtypeStruct((lay.n_tok, RW_W), F32),
            jax.ShapeDtypeStruct((lay.n_tok, CV_W), F32),
        ],
        compiler_params=_cparams(("parallel",)),
        name="inproj",
    )(x, mods, mods, wq, wr, wc, cos, sin)


def _attn_kernel(*refs, lam_init, tq, tk):
    n_kv = (len(refs) - 4) // 3
    q_ref = refs[0]
    k_refs = refs[1:1 + 2 * n_kv:2]
    v_refs = refs[2:2 + 2 * n_kv:2]
    lam_ref, g_ref, o_ref = refs[1 + 2 * n_kv:4 + 2 * n_kv]
    vx_refs = refs[4 + 2 * n_kv:]

    @pl.when(pl.program_id(2) == 0)
    def _():
        for v_ref, vx_ref in zip(v_refs, vx_refs):
            vx_ref[:, :ATT_V] = v_ref[...]
            vx_ref[:, ATT_V:] = jnp.ones((vx_ref.shape[0], LANES), BF16)

    q = q_ref[...]
    lane = lax.broadcasted_iota(jnp.int32, q.shape, 1)
    zero = jnp.zeros_like(q)
    qq = jnp.concatenate([jnp.where(lane < ATT_QK, q, zero), jnp.where(lane >= ATT_QK, q, zero)], axis=0)
    lam4 = lam_ref[...]
    lam = (jnp.exp(jnp.sum(lam4[0:1] * lam4[1:2], axis=-1, keepdims=True))
           - jnp.exp(jnp.sum(lam4[2:3] * lam4[3:4], axis=-1, keepdims=True)) + lam_init)

    def scores(blk):
        k_ref, _, start, size = blk
        return lax.dot_general(qq, k_ref[start:start + size, :], _NT, preferred_element_type=F32)

    def attend(blocks):
        s = scores(blocks[0])
        m = acc = None
        for j, (_, vx_ref, start, size) in enumerate(blocks):
            s_next = scores(blocks[j + 1]) if j + 1 < len(blocks) else None
            bm = jnp.max(s, axis=-1, keepdims=True)
            m_new = bm if m is None else jnp.maximum(m, bm)
            p = jnp.exp2(s - m_new).astype(BF16)
            pv = _dot(p, vx_ref[start:start + size, :])
            acc = pv if m is None else acc * jnp.exp2(m - m_new) + pv
            m, s = m_new, s_next
        o = acc[:, :ATT_V] / acc[:, ATT_V:]
        dlt = o[:tq] - lam * o[tq:]
        ms = jnp.mean(dlt * dlt, axis=-1, keepdims=True)
        o_ref[...] = (dlt * lax.rsqrt(ms + SUBLN_EPS) * g_ref[...] * (1.0 - lam_init)).astype(o_ref.dtype)

    blocks = []
    for k_ref, vx_ref in zip(k_refs, vx_refs):
        step = min(tk, k_ref.shape[0])
        blocks += [(k_ref, vx_ref, s0, step) for s0 in range(0, k_ref.shape[0], step)]
    attend(blocks)


def _attention(lay, layer, qkv, lam4, subln_g, lam_init, tq, tk, latent):
    seg = lay.seq if latent else lay.ctx_len
    n_q = seg // tq
    q_blk0 = 0 if latent else lay.n_lat // tq
    ctx_blk0 = lay.n_lat // lay.ctx_len
    hq = ATT_WIDTH // LANES

    kv_specs = [pl.BlockSpec((lay.ctx_len, LANES), lambda b, h, qi: (ctx_blk0 + b, hq + h)),
                pl.BlockSpec((lay.ctx_len, LANES), lambda b, h, qi: (ctx_blk0 + b, 2 * hq + h))]
    scratch = [pltpu.VMEM((lay.ctx_len, ATT_V + LANES), BF16)]
    if latent:
        kv_specs += [pl.BlockSpec((lay.seq, LANES), lambda b, h, qi: (b, hq + h)),
                     pl.BlockSpec((lay.seq, LANES), lambda b, h, qi: (b, 2 * hq + h))]
        scratch += [pltpu.VMEM((lay.seq, ATT_V + LANES), BF16)]
    kern = functools.partial(_attn_kernel, lam_init=lam_init, tq=tq, tk=tk)
    return pl.pallas_call(
        kern,
        grid=(lay.batch, ATT_HEADS, n_q),
        in_specs=[pl.BlockSpec((tq, LANES), lambda b, h, qi: (q_blk0 + b * n_q + qi, h))] + kv_specs + [
            pl.BlockSpec((None, 4, ATT_QK), lambda b, h, qi: (layer, 0, 0)),
            pl.BlockSpec((None, 1, ATT_V), lambda b, h, qi: (layer, 0, 0)),
        ],
        out_specs=pl.BlockSpec((tq, LANES), lambda b, h, qi: (b * n_q + qi, h)),
        out_shape=jax.ShapeDtypeStruct((lay.batch * seg, ATT_WIDTH), BF16),
        scratch_shapes=scratch,
        compiler_params=_cparams(("parallel", "parallel", "arbitrary")),
        name="diff_attn" if latent else "diff_attn_ctx",
    )(*([qkv] * (1 + len(kv_specs))), lam4, subln_g)


def _prep_kernel(rw_ref, rwp_ref, rwn_ref, cv_ref, cvp_ref, cvn_ref,
                 rkvw_ref, w0_ref, wd_ref, a0_ref, wa_ref, wg_ref, kk_w_ref, ka_ref, rk_ref, cw_ref,
                 r_ref, v_ref, kk_ref, lw_ref, bb_ref, kd_ref, bv_ref, g_ref, co_ref,
                 *, tm, n_lat, ctx_len, seq):
    i = pl.program_id(0)
    start = i * tm
    in_lat = start < n_lat
    seg_pos = jnp.where(in_lat, start % seq, (start - n_lat) % ctx_len)
    seg_len = jnp.where(in_lat, seq, ctx_len)
    has_prev = seg_pos != 0
    has_next = seg_pos + tm != seg_len

    def shifted(u, prev_row, next_row):
        rows = lax.broadcasted_iota(jnp.int32, u.shape, 0)
        prev_row = jnp.where(has_prev, prev_row, jnp.zeros_like(prev_row))
        next_row = jnp.where(has_next, next_row, jnp.zeros_like(next_row))
        up = jnp.where(rows == 0, prev_row, pltpu.roll(u, 1, 0))
        un = jnp.where(rows == tm - 1, next_row, pltpu.roll(u, tm - 1, 0))
        return up, un

    def conv3(u, prev_row, next_row, w):
        up, un = shifted(u, prev_row, next_row)
        return up * w[0:1] + u * w[1:2] + un * w[2:3]

    rw = rw_ref[...]
    nrkv = 3 * RWKV_WIDTH
    rkv = conv3(rw[:, :nrkv], rwp_ref[7:8, :nrkv], rwn_ref[0:1, :nrkv], rkvw_ref[...])
    r = rkv[:, :RWKV_WIDTH]
    k = rkv[:, RWKV_WIDTH:2 * RWKV_WIDTH]
    v = rkv[:, 2 * RWKV_WIDTH:]

    lora = rw[:, nrkv:nrkv + LANES]
    gate = rw[:, nrkv + LANES:]
    wl = _dot3(_split(jnp.tanh(lora)), _split(wd_ref[...])) + w0_ref[...]
    lw = (-math.exp(-0.5)) * jax.nn.sigmoid(wl)
    a = jax.nn.sigmoid(_dot3(_split(lora), _split(wa_ref[...])) + a0_ref[...])
    g = _dot3(_split(jax.nn.sigmoid(gate)), _split(wg_ref[...]))

    ones = _head_ones(RWKV_WIDTH, RWKV_N)
    kraw = k * kk_w_ref[...]
    ss = _dot_mask_rhs(kraw * kraw, ones)
    kk = kraw * lax.rsqrt(jnp.maximum(ss, 1e-24))
    k2 = jnp.concatenate([k, k], axis=1)
    ka2 = jnp.concatenate([ka_ref[...], ka_ref[...]], axis=1)
    kd = k2 * (1.0 + (a - 1.0) * ka2)
    bb = jnp.concatenate([kk, kk], axis=1) * a
    bonus = _dot_mask_rhs(r * (kd[:, :RWKV_WIDTH] + kd[:, RWKV_WIDTH:]) * rk_ref[...], ones)

    r_ref[...] = r
    v_ref[...] = v
    kk_ref[...] = kk
    lw_ref[...] = lw
    bb_ref[...] = bb
    kd_ref[...] = kd
    bv_ref[...] = bonus * v
    g_ref[...] = g

    def gated(ref):
        return ref[:, 2 * CONV_WIDTH:] * ref[:, :CONV_WIDTH]

    cv_u = gated(cv_ref)
    conv = conv3(cv_u, gated(cvp_ref)[7:8], gated(cvn_ref)[0:1], cw_ref[...])
    co_ref[...] = (cv_ref[:, CONV_WIDTH:2 * CONV_WIDTH] * conv).astype(co_ref.dtype)


def _split(a):
    hi = a.astype(BF16)
    return hi, (a - hi.astype(F32)).astype(BF16)


_NN = (((1,), (0,)), ((), ()))
_NT = (((1,), (1,)), ((), ()))
_TN = (((0,), (0,)), ((), ()))


def _dot3(a, b, dims=_NN):
    (ah, al), (bh, bl) = a, b
    dg = lambda p, q: lax.dot_general(p, q, dims, preferred_element_type=F32)
    return dg(ah, bh) + dg(ah, bl) + dg(al, bh)


def _dot1(a, b, dims=_NN):
    return lax.dot_general(a.astype(BF16), b.astype(BF16), dims, preferred_element_type=F32)


def _chunk_terms_kernel(r_ref, v_ref, kk_ref, lw_ref, bb_ref, kd_ref, g_ref, h_ref, rq_ref, y0_ref):
    c = SCAN_CHUNK
    n = RWKV_N
    row = lax.broadcasted_iota(jnp.int32, (c, c), 0)
    col = lax.broadcasted_iota(jnp.int32, (c, c), 1)
    row2 = lax.broadcasted_iota(jnp.int32, (c, 2 * c), 0)
    col2 = lax.broadcasted_iota(jnp.int32, (c, 2 * c), 1) % c
    eye = lax.broadcasted_iota(jnp.int32, (n, n), 0) == lax.broadcasted_iota(jnp.int32, (n, n), 1)
    n_chunks = r_ref.shape[0] // c

    chains = []
    for ck in range(n_chunks):
        rows = slice(ck * c, (ck + 1) * c)
        r_all = r_ref[rows, :]
        v_all = v_ref[rows, :]
        kk = kk_ref[rows, :]
        for d in range(2):
            sgn = 1 if d == 0 else -1
            dsl = slice(d * RWKV_WIDTH, (d + 1) * RWKV_WIDTH)
            incl = (col - row) * sgn <= 0
            lw = lw_ref[rows, dsl]
            bb = bb_ref[rows, dsl]
            kd = kd_ref[rows, dsl]
            lp = _dot_mask_lhs(incl, lw)
            lt = jnp.sum(lw, axis=0, keepdims=True)
            p_inv = jnp.exp(-lp)
            p_end = jnp.exp(lt - lp)
            a_t = -kk * jnp.exp(lp - lw)
            b_t = bb * p_inv
            k_t = kd * p_inv
            r_t = r_all * jnp.exp(lp)
            k_e = kd * p_end
            b_e = bb * p_end
            p_tot = jnp.exp(lt)
            for h in range(RWKV_HEADS):
                sl = slice(h * n, (h + 1) * n)
                chains.append(dict(
                    strict=(col - row) * sgn < 0, incl2=(col2 - row2) * sgn <= 0,
                    ah=a_t[:, sl], rh=r_t[:, sl], vh=v_all[:, sl], bt=b_t[:, sl], kt=k_t[:, sl],
                    be=b_e[:, sl], ke=k_e[:, sl], ptot=p_tot[:, sl]))

    for ch in chains:
        ch["sc"] = _dot3(_split(jnp.concatenate([ch["ah"], ch["rh"]], axis=0)),
                         _split(jnp.concatenate([ch["bt"], ch["kt"]], axis=0)), _NT)
    for ch in chains:
        sc = ch["sc"]
        ch["nmat"] = jnp.where(ch["strict"], sc[:c, :c], 0.0).astype(BF16)
        ch["mr"] = jnp.where(ch["incl2"], sc[c:, :], 0.0).astype(BF16)
        ch["mkv"] = _dot1(jnp.where(ch["strict"], sc[:c, c:], 0.0), ch["vh"])
    for ch in chains:
        ch["z"] = jnp.concatenate([ch["ah"], ch["mkv"]], axis=1)
    n_factors = c.bit_length() - 1
    for p in range(n_factors):
        for ch in chains:
            ch["z"] = ch["z"] + _dot1(ch["nmat"], ch["z"])
        if p < n_factors - 1:
            for ch in chains:
                ch["nmat"] = _dot1(ch["nmat"], ch["nmat"]).astype(BF16)
    for ch in chains:
        z = ch["z"]
        ch["ws"] = z[:, :n].astype(BF16)
        ch["uv"] = jnp.concatenate([z[:, n:], ch["vh"]], axis=0).astype(BF16)
    for ch in chains:
        mr = ch["mr"]
        ch["rq"] = ch["rh"] + _dot1(mr[:, :c], ch["ws"])
        ch["y0"] = _dot1(mr, ch["uv"])
        ch["g"] = jnp.where(eye, ch["ptot"], 0.0) + _dot1(ch["ws"], ch["be"], _TN)
        ch["h"] = _dot1(ch["uv"], jnp.concatenate([ch["be"], ch["ke"]], axis=0), _TN)
    for ck in range(n_chunks):
        rows = slice(ck * c, (ck + 1) * c)
        for d in range(2):
            first = (ck * 2 + d) * RWKV_HEADS
            part = chains[first:first + RWKV_HEADS]
            for ref, key in ((g_ref, "g"), (h_ref, "h")):
                top = jnp.concatenate([ch[key] for ch in part], axis=1)
                if c > n:
                    top = jnp.concatenate([top, jnp.zeros((c - n, RWKV_WIDTH), F32)], axis=0)
                ref[d, rows, :] = top
            rq_ref[d, rows, :] = jnp.concatenate([ch["rq"] for ch in part], axis=1)
            y0_ref[d, rows, :] = jnp.concatenate([ch["y0"] for ch in part], axis=1)


def _prep_chunk_kernel(*refs, tm, n_lat, ctx_len, seq):
    ins, (bv_ref, g_ref, co_ref), terms, scratch = refs[:16], refs[16:19], refs[19:23], refs[23:]
    _prep_kernel(*ins, *scratch, bv_ref, g_ref, co_ref, tm=tm, n_lat=n_lat, ctx_len=ctx_len, seq=seq)
    _chunk_terms_kernel(*scratch, *terms)


def _prep_chunk_terms(lay, layer, rw, cv, rkv_conv, w0, wd, a0, wa, wg, k_k, k_a, r_k, conv_w, tm):
    n8 = lay.n_tok // 8
    t8 = tm // 8
    row = lambda i: (i, 0)
    prev = lambda i: (jnp.maximum(i * t8 - 1, 0), 0)
    nxt = lambda i: (jnp.minimum((i + 1) * t8, n8 - 1), 0)
    lmap = lambda i: (layer, 0, 0)
    w2 = 2 * RWKV_WIDTH
    kern = functools.partial(_prep_chunk_kernel, tm=tm, n_lat=lay.n_lat, ctx_len=lay.ctx_len, seq=lay.seq)
    f32 = lambda w: jax.ShapeDtypeStruct((lay.n_tok, w), F32)
    term = pl.BlockSpec((2, tm, RWKV_WIDTH), lambda i: (0, i, 0))
    term_shape = jax.ShapeDtypeStruct((2, lay.n_tok, RWKV_WIDTH), F32)
    return pl.pallas_call(
        kern,
        grid=(lay.n_tok // tm,),
        in_specs=[
            pl.BlockSpec((tm, RW_W), row), pl.BlockSpec((8, RW_W), prev), pl.BlockSpec((8, RW_W), nxt),
            pl.BlockSpec((tm, CV_W), row), pl.BlockSpec((8, CV_W), prev), pl.BlockSpec((8, CV_W), nxt),
            pl.BlockSpec((None, 3, 3 * RWKV_WIDTH), lmap),
            pl.BlockSpec((None, 1, w2), lmap),
            pl.BlockSpec((None, LANES, w2), lmap),
            pl.BlockSpec((None, 1, w2), lmap),
            pl.BlockSpec((None, LANES, w2), lmap),
            pl.BlockSpec((None, LANES, RWKV_WIDTH), lmap),
            pl.BlockSpec((None, 1, RWKV_WIDTH), lmap),
            pl.BlockSpec((None, 1, RWKV_WIDTH), lmap),
            pl.BlockSpec((None, 1, RWKV_WIDTH), lmap),
            pl.BlockSpec((None, 3, CONV_WIDTH), lmap),
        ],
        out_specs=[pl.BlockSpec((tm, RWKV_WIDTH), row), pl.BlockSpec((tm, RWKV_WIDTH), row),
                   pl.BlockSpec((tm, CONV_WIDTH), row), term, term, term, term],
        out_shape=[f32(RWKV_WIDTH), f32(RWKV_WIDTH), jax.ShapeDtypeStruct((lay.n_tok, CONV_WIDTH), BF16),
                   term_shape, term_shape, term_shape, term_shape],
        scratch_shapes=[pltpu.VMEM((tm, RWKV_WIDTH), F32)] * 3 + [pltpu.VMEM((tm, w2), F32)] * 3,
        compiler_params=_cparams(("parallel",)),
        name="rwkv_prep_chunk_terms",
    )(rw, rw, rw, cv, cv, cv, rkv_conv, w0, wd, a0, wa, wg, k_k, k_a, r_k, conv_w)


def _scan_kernel(gf_ref, hf_ref, rqf_ref, y0f_ref, gb_ref, hb_ref, rqb_ref, y0b_ref, yf_ref, yb_ref, st_ref):
    @pl.when(pl.program_id(1) == 0)
    def _():
        st_ref[...] = jnp.zeros_like(st_ref)

    dirs = ((gf_ref, hf_ref, rqf_ref, y0f_ref, yf_ref), (gb_ref, hb_ref, rqb_ref, y0b_ref, yb_ref))
    heads = [slice(h * RWKV_N, (h + 1) * RWKV_N) for h in range(RWKV_HEADS)]
    n_sub = gf_ref.shape[0] // SCAN_CHUNK
    state = [[st_ref[d, h] for h in range(RWKV_HEADS)] for d in range(2)]
    for step in range(n_sub):
        rows = [slice(k * SCAN_CHUNK, (k + 1) * SCAN_CHUNK) for k in (step, n_sub - 1 - step)]
        top = [slice(k * SCAN_CHUNK, k * SCAN_CHUNK + RWKV_N) for k in (step, n_sub - 1 - step)]
        s0 = [[_split(state[d][h]) for h in range(RWKV_HEADS)] for d in range(2)]
        state = [[_dot1(s0[d][h][0], dirs[d][0][top[d], sl]) + _dot1(s0[d][h][1], dirs[d][0][top[d], sl])
                  + dirs[d][1][top[d], sl] for h, sl in enumerate(heads)] for d in range(2)]
        for d in range(2):
            rq_ref, y0_ref, y_ref = dirs[d][2:]
            y_ref[rows[d], :] = jnp.concatenate(
                [_dot1(rq_ref[rows[d], sl], s0[d][h][0], _NT) + y0_ref[rows[d], sl]
                 for h, sl in enumerate(heads)], axis=1)
    for d in range(2):
        for h in range(RWKV_HEADS):
            st_ref[d, h] = state[d][h]


def _scan(lay, g, h, rq, y0, chunks_per_step):
    c = SCAN_CHUNK * chunks_per_step
    nc_ctx = lay.ctx_len // c
    nc_lat = lay.seq // c
    lat_blocks = lay.n_lat // c

    def blk(b, d, ci):
        if d == 0:
            return jnp.where(ci < nc_ctx, lat_blocks + b * nc_ctx + ci, b * nc_lat + (ci - nc_ctx))
        return jnp.where(ci < nc_ctx, lat_blocks + b * nc_ctx + (nc_ctx - 1 - ci),
                         b * nc_lat + (nc_lat - 1 - (ci - nc_ctx)))

    def term(d):
        return pl.BlockSpec((None, c, RWKV_WIDTH), lambda b, ci: (d, blk(b, d, ci), 0))

    def yspec(d):
        return pl.BlockSpec((c, RWKV_WIDTH), lambda b, ci: (blk(b, d, ci), 0))

    shp = jax.ShapeDtypeStruct((lay.n_tok, RWKV_WIDTH), F32)
    return pl.pallas_call(
        _scan_kernel,
        grid=(lay.batch, nc_ctx + nc_lat),
        in_specs=[term(0)] * 4 + [term(1)] * 4,
        out_specs=[yspec(0), yspec(1)],
        out_shape=[shp, shp],
        scratch_shapes=[pltpu.VMEM((2, RWKV_HEADS, RWKV_N, RWKV_N), F32)],
        compiler_params=_cparams(("parallel", "arbitrary")),
        name="rwkv_scan",
    )(g, h, rq, y0, g, h, rq, y0)


def _outproj_kernel(att_lat_ref, att_ctx_ref, yf_ref, yb_ref, bv_ref, g_ref, co_ref, x_ref, gate_ref, w_ref,
                    gng_ref, gnb_ref, lng_ref, lnb_ref, o_ref, *, n_lat_tiles):
    att = jnp.where(pl.program_id(0) < n_lat_tiles, att_lat_ref[...], att_ctx_ref[...])
    ones = _head_ones(RWKV_WIDTH, RWKV_N)
    y = yf_ref[...] + yb_ref[...]
    mu = _dot_mask_rhs(y, ones) * (1.0 / RWKV_N)
    yc = y - mu
    var = _dot_mask_rhs(yc * yc, ones) * (1.0 / RWKV_N)
    yn = yc * lax.rsqrt(var + GN_EPS) * gng_ref[...] + gnb_ref[...]
    rwkv = ((yn + bv_ref[...]) * g_ref[...]).astype(BF16)
    mix = (_dot(att, w_ref[:ATT_WIDTH, :])
           + _dot(rwkv, w_ref[ATT_WIDTH:ATT_WIDTH + RWKV_WIDTH, :])
           + _dot(co_ref[...], w_ref[ATT_WIDTH + RWKV_WIDTH:, :]))
    z = DN_ALPHA * x_ref[...] + gate_ref[...] * mix
    o_ref[...] = _layer_norm(z, lng_ref[...], lnb_ref[...])


def _outproj(lay, layer, att_lat, att_ctx, yf, yb, bv, g, co, x, mods, w_out, gn_g, gn_b, ln_g, ln_b, tm, n_rows):
    row = lambda i: (i, 0)
    lmap = lambda i: (layer, 0, 0)
    n_lat_tiles = lay.n_lat // tm
    return pl.pallas_call(
        functools.partial(_outproj_kernel, n_lat_tiles=n_lat_tiles),
        grid=(n_rows // tm,),
        in_specs=[
            pl.BlockSpec((tm, ATT_WIDTH), lambda i: (jnp.minimum(i, n_lat_tiles - 1), 0)),
            pl.BlockSpec((tm, ATT_WIDTH), lambda i: (jnp.maximum(i - n_lat_tiles, 0), 0)),
            pl.BlockSpec((tm, RWKV_WIDTH), row),
            pl.BlockSpec((tm, RWKV_WIDTH), row),
            pl.BlockSpec((tm, RWKV_WIDTH), row),
            pl.BlockSpec((tm, RWKV_WIDTH), row),
            pl.BlockSpec((tm, CONV_WIDTH), row),
            pl.BlockSpec((tm, D_MODEL), row),
            lay.mod_spec(layer, tm, 2),
            pl.BlockSpec((None, D_MODEL, D_MODEL), lmap),
            pl.BlockSpec((None, 1, RWKV_WIDTH), lmap),
            pl.BlockSpec((None, 1, RWKV_WIDTH), lmap),
            pl.BlockSpec((None, 1, D_MODEL), lmap),
            pl.BlockSpec((None, 1, D_MODEL), lmap),
        ],
        out_specs=pl.BlockSpec((tm, D_MODEL), row),
        out_shape=jax.ShapeDtypeStruct((n_rows, D_MODEL), F32),
        compiler_params=_cparams(("parallel",)),
        name="outproj_ln1",
    )(att_lat, att_ctx, yf, yb, bv, g, co, x, mods, w_out, gn_g, gn_b, ln_g, ln_b)


def _ffn_kernel(x_ref, sh_ref, sc_ref, gate_ref, w1_ref, w3_ref, w2_ref, lng_ref, lnb_ref, o_ref, *, n_split):
    x = x_ref[...]
    h = (x * (1.0 + sc_ref[...]) + sh_ref[...]).astype(BF16)
    step = D_FF // n_split
    f = jnp.zeros(x.shape, F32)
    for s in range(n_split):
        cols = slice(s * step, (s + 1) * step)
        a = _dot(h, w1_ref[:, cols])
        act = (a * jax.nn.sigmoid(a) * _dot(h, w3_ref[:, cols])).astype(BF16)
        f = f + _dot(act, w2_ref[cols, :])
    z = DN_ALPHA * x + gate_ref[...] * f
    o_ref[...] = _layer_norm(z, lng_ref[...], lnb_ref[...])


def _ffn(lay, layer, j, x, mods, w1, w3, w2, ln_g, ln_b, tm):
    row = lambda i: (i, 0)
    once = pl.Buffered(1)
    return pl.pallas_call(
        functools.partial(_ffn_kernel, n_split=2),
        grid=(lay.n_tok // tm,),
        in_specs=[
            pl.BlockSpec((tm, D_MODEL), row),
            lay.mod_spec(layer, tm, 3), lay.mod_spec(layer, tm, 4), lay.mod_spec(layer, tm, 5),
            pl.BlockSpec((None, D_MODEL, D_FF), lambda i: (j, 0, 0), pipeline_mode=once),
            pl.BlockSpec((None, D_MODEL, D_FF), lambda i: (j, 0, 0), pipeline_mode=once),
            pl.BlockSpec((None, D_FF, D_MODEL), lambda i: (j, 0, 0), pipeline_mode=once),
            pl.BlockSpec((None, 1, D_MODEL), lambda i: (layer, 0, 0)),
            pl.BlockSpec((None, 1, D_MODEL), lambda i: (layer, 0, 0)),
        ],
        out_specs=pl.BlockSpec((tm, D_MODEL), row),
        out_shape=jax.ShapeDtypeStruct((lay.n_tok, D_MODEL), F32),
        compiler_params=_cparams(("parallel",)),
        name="ffn_ln2",
    )(x, mods, mods, mods, w1, w3, w2, ln_g, ln_b)


def _router_kernel(x_ref, sh_ref, sc_ref, rw_ref, rb_ref, tri_ref, g_ref, rk_ref, rkt_ref, cnt_ref):
    h = x_ref[...] * (1.0 + sc_ref[...]) + sh_ref[...]
    logits = _dot3(_split(h), _split(rw_ref[...])) + rb_ref[...]
    lane = lax.broadcasted_iota(jnp.int32, logits.shape, 1)
    m1 = jnp.max(logits, axis=-1, keepdims=True)
    i1 = jnp.min(jnp.where(logits == m1, lane, LANES), axis=-1, keepdims=True)
    rest = jnp.where(lane == i1, -jnp.inf, logits)
    m2 = jnp.max(rest, axis=-1, keepdims=True)
    i2 = jnp.min(jnp.where(rest == m2, lane, LANES), axis=-1, keepdims=True)
    e2 = jnp.exp(m2 - m1)
    den = 1.0 + e2
    g_ref[...] = jnp.where(lane == i1, 1.0 / den, 0.0) + jnp.where(lane == i2, e2 / den, 0.0)
    sel = jnp.where(lane == i1, 1.0, 0.0) + jnp.where(lane == i2, 1.0, 0.0)
    rank = _dot(tri_ref[...], sel.astype(BF16))
    rk = jnp.where(sel > 0.0, rank, -1.0)
    rk_ref[...] = rk
    rkt_ref[...] = jnp.transpose(rk)[:N_EXPERTS, :]
    cnt = jnp.sum(sel, axis=0, keepdims=True).astype(jnp.int32)
    cnt_ref[...] = jnp.broadcast_to(cnt, cnt_ref.shape)


def _router(lay, layer, j, x, mods, router_w, router_b, tri, tm, n_rows):
    n_tiles = n_rows // tm
    row = lambda i: (i, 0)
    return pl.pallas_call(
        _router_kernel,
        grid=(n_tiles,),
        in_specs=[
            pl.BlockSpec((tm, D_MODEL), row),
            lay.mod_spec(layer, tm, 3), lay.mod_spec(layer, tm, 4),
            pl.BlockSpec((None, D_MODEL, LANES), lambda i: (j, 0, 0)),
            pl.BlockSpec((None, 1, LANES), lambda i: (j, 0, 0)),
            pl.BlockSpec((tm, tm), lambda i: (0, 0)),
        ],
        out_specs=[
            pl.BlockSpec((tm, LANES), row),
            pl.BlockSpec((tm, LANES), row),
            pl.BlockSpec((None, N_EXPERTS, tm), lambda i: (i, 0, 0)),
            pl.BlockSpec((None, 8, LANES), lambda i: (i, 0, 0)),
        ],
        out_shape=[
            jax.ShapeDtypeStruct((n_rows, LANES), F32),
            jax.ShapeDtypeStruct((n_rows, LANES), F32),
            jax.ShapeDtypeStruct((n_tiles, N_EXPERTS, tm), F32),
            jax.ShapeDtypeStruct((n_tiles, 8, LANES), jnp.int32),
        ],
        compiler_params=_cparams(("parallel",)),
        name="moe_router",
    )(x, mods, mods, router_w, router_b, tri)


def _moe_kernel(cnt_ref, x_ref, sh_ref, sc_ref, gate_ref, g_ref, rk_ref, rkt_ref, w1_ref, w3_ref, w2_ref,
                lng_ref, lnb_ref, o_ref, h_ref, acc_ref, *, blk):
    i = pl.program_id(0)
    e = pl.program_id(1)
    tm = x_ref.shape[0]

    @pl.when(e == 0)
    def _():
        h_ref[...] = (x_ref[...] * (1.0 + sc_ref[...]) + sh_ref[...]).astype(BF16)
        acc_ref[...] = jnp.zeros_like(acc_ref)

    lane = lax.broadcasted_iota(jnp.int32, (tm, LANES), 1)
    rk_col = jnp.sum(jnp.where(lane == e, rk_ref[...], 0.0), axis=-1, keepdims=True)
    rk_row = rkt_ref[pl.ds(e, 1), :]
    g_parts = _split(g_ref[...])

    pos_r = lax.broadcasted_iota(jnp.int32, (blk, tm), 0).astype(F32)
    pos_c = lax.broadcasted_iota(jnp.int32, (tm, blk), 1).astype(F32)
    lane_b = lax.broadcasted_iota(jnp.int32, (blk, LANES), 1)

    def block(jb, carry):
        base = (jb * blk).astype(F32)
        take = jnp.where(rk_row - base == pos_r, 1.0, 0.0).astype(BF16)
        put = jnp.where(rk_col - base == pos_c, 1.0, 0.0).astype(BF16)
        hg = _dot(take, h_ref[...]).astype(BF16)
        a = _dot(hg, w1_ref[...])
        act = (a * jax.nn.sigmoid(a) * _dot(hg, w3_ref[...])).astype(BF16)
        f = _dot(act, w2_ref[...])
        gg = _dot(take, g_parts[0]) + _dot(take, g_parts[1])
        ge = jnp.sum(jnp.where(lane_b == e, gg, 0.0), axis=-1, keepdims=True)
        acc_ref[...] += _dot(put, (ge * f).astype(BF16))
        return carry

    n_routed = cnt_ref[i * N_EXPERTS + e]
    lax.fori_loop(0, (n_routed + blk - 1) // blk, block, 0)

    @pl.when(e == N_EXPERTS - 1)
    def _():
        z = DN_ALPHA * x_ref[...] + gate_ref[...] * acc_ref[...]
        o_ref[...] = _layer_norm(z, lng_ref[...], lnb_ref[...])


def _moe(lay, layer, j, x, mods, counts, g, rk, rkt, w1, w3, w2, ln_g, ln_b, tm, blk, n_rows):
    row = lambda i, e, cnt: (i, 0)
    grid_spec = pltpu.PrefetchScalarGridSpec(
        num_scalar_prefetch=1,
        grid=(n_rows // tm, N_EXPERTS),
        in_specs=[
            pl.BlockSpec((tm, D_MODEL), row),
            lay.mod_spec(layer, tm, 3), lay.mod_spec(layer, tm, 4), lay.mod_spec(layer, tm, 5),
            pl.BlockSpec((tm, LANES), row),
            pl.BlockSpec((tm, LANES), row),
            pl.BlockSpec((None, N_EXPERTS, tm), lambda i, e, cnt: (i, 0, 0)),
            pl.BlockSpec((None, None, D_MODEL, D_FF_EXPERT), lambda i, e, cnt: (j, e, 0, 0)),
            pl.BlockSpec((None, None, D_MODEL, D_FF_EXPERT), lambda i, e, cnt: (j, e, 0, 0)),
            pl.BlockSpec((None, None, D_FF_EXPERT, D_MODEL), lambda i, e, cnt: (j, e, 0, 0)),
            pl.BlockSpec((None, 1, D_MODEL), lambda i, e, cnt: (layer, 0, 0)),
            pl.BlockSpec((None, 1, D_MODEL), lambda i, e, cnt: (layer, 0, 0)),
        ],
        out_specs=pl.BlockSpec((tm, D_MODEL), row),
        scratch_shapes=[pltpu.VMEM((tm, D_MODEL), BF16), pltpu.VMEM((tm, D_MODEL), F32)],
    )
    return pl.pallas_call(
        functools.partial(_moe_kernel, blk=blk),
        grid_spec=grid_spec,
        out_shape=jax.ShapeDtypeStruct((n_rows, D_MODEL), F32),
        compiler_params=_cparams(("parallel", "arbitrary")),
        name="moe_ln2",
    )(counts, x, mods, mods, mods, g, rk, rkt, w1, w3, w2, ln_g, ln_b)


def _rope_tables(ctx_len, seq):
    t = np.arange(seq)
    inv = ROPE_THETA ** (-np.arange(0, AXIS_DIM, 2, dtype=np.float64) / AXIS_DIM)
    ang_r = (t // GRID_W)[:, None].astype(np.float64) * inv
    ang_c = (t % GRID_W)[:, None].astype(np.float64) * inv
    cos = np.concatenate([np.cos(ang_r), np.cos(ang_r), np.cos(ang_c), np.cos(ang_c)], axis=1)
    sin = np.concatenate([-np.sin(ang_r), np.sin(ang_r), -np.sin(ang_c), np.sin(ang_c)], axis=1)
    cos = np.concatenate([cos, np.ones((ctx_len, ATT_QK))], axis=0)
    sin = np.concatenate([sin, np.zeros((ctx_len, ATT_QK))], axis=0)
    reps = LANES // ATT_QK
    return (jnp.asarray(np.tile(cos, (1, reps)), F32), jnp.asarray(np.tile(sin, (1, reps)), F32))


def _tiles(batch, ctx_len, seq):
    tm = 4 * SCAN_CHUNK
    tm_wide = 2 * tm
    assert (batch * ctx_len) % tm_wide == 0 and seq % tm_wide == 0
    tq = 512
    tq_ctx = 256
    tk = next(t for t in (1024, 512, 256) if seq % t == 0)
    tm_moe = next(t for t in (1024, 512, 256) if (batch * ctx_len) % t == 0 and seq % t == 0)
    moe_blk = LANES
    scan_chunks_per_step = 4
    assert ctx_len % tm == 0 and seq % tm == 0 and ctx_len % tq_ctx == 0 and seq % tq == 0 and seq % tk == 0
    assert (batch * seq) % tq_ctx == 0
    assert seq % ctx_len == 0 and ctx_len % (SCAN_CHUNK * scan_chunks_per_step) == 0 and seq % GRID_W == 0
    return tm, tm_wide, tq, tq_ctx, tk, tm_moe, moe_blk, scan_chunks_per_step


def kernel(x, c, ctx, c_ctx, w_ada, b_ada, w_in, w_out, ln1_g, ln1_b, ln2_g, ln2_b,
           lam_q1, lam_k1, lam_q2, lam_k2, subln_g, rkv_conv, decay_w0, decay_up, iclr_a0, iclr_up,
           gate_up, k_k, k_a, r_k, gn_g, gn_b, conv_w, ffn_w1, ffn_w3, ffn_w2,
           router_w, router_b, moe_w1, moe_w3, moe_w2):
    batch, seq, d = x.shape
    ctx_len = ctx.shape[1]
    depth = w_in.shape[0]
    assert d == D_MODEL and depth == DEPTH and batch < MOD_ROWS
    tm, tm_wide, tq, tq_ctx, tk, tm_moe, moe_blk, scan_chunks_per_step = _tiles(batch, ctx_len, seq)
    tri = jnp.asarray(np.tri(tm_moe, k=-1), BF16)
    lay = _Layout(batch, ctx_len, seq)

    tokens = jnp.concatenate([x.reshape(batch * seq, d), ctx.reshape(batch * ctx_len, d)], axis=0)

    c_all = jnp.concatenate([c, c_ctx[None, :], jnp.zeros((MOD_ROWS - batch - 1, d), F32)], axis=0)
    mods = _adaln(c_all, w_ada, b_ada).reshape(depth * MOD_ROWS, 1, 6 * d)

    cos, sin = _rope_tables(ctx_len, seq)

    o = np.cumsum([0, 512, 512, 512, 256, 256, 256, 64, 64, 64, 256, 256, 256])
    wq = w_in[:, :, :o[3]].astype(BF16)
    wr = jnp.concatenate([w_in[:, :, o[3]:o[9]], jnp.zeros((depth, d, RW_W - (o[9] - o[3])), F32)],
                         axis=-1).astype(BF16)
    wc = w_in[:, :, o[9]:].astype(BF16)
    w_out_b = w_out.astype(BF16)

    z = jnp.zeros((depth, DECAY_LORA, RWKV_WIDTH), F32)
    wd = jnp.concatenate([
        jnp.concatenate([decay_up[:, 0], z], axis=-1),
        jnp.concatenate([z, decay_up[:, 1]], axis=-1),
        jnp.zeros((depth, 2 * ICLR_LORA, 2 * RWKV_WIDTH), F32)], axis=1)
    wa = jnp.concatenate([
        jnp.zeros((depth, 2 * DECAY_LORA, 2 * RWKV_WIDTH), F32),
        jnp.concatenate([iclr_up[:, 0], z], axis=-1),
        jnp.concatenate([z, iclr_up[:, 1]], axis=-1)], axis=1)
    wg = jnp.concatenate([gate_up, jnp.zeros((depth, LANES - GATE_LORA, RWKV_WIDTH), F32)], axis=1)
    w0 = decay_w0.reshape(depth, 1, 2 * RWKV_WIDTH)
    a0 = iclr_a0.reshape(depth, 1, 2 * RWKV_WIDTH)
    vec = lambda w: w.reshape(depth, 1, -1)

    lam4 = jnp.stack([lam_q1, lam_k1, lam_q2, lam_k2], axis=1)

    ffn_w1_b, ffn_w3_b, ffn_w2_b = ffn_w1.astype(BF16), ffn_w3.astype(BF16), ffn_w2.astype(BF16)
    moe_w1_b, moe_w3_b, moe_w2_b = moe_w1.astype(BF16), moe_w3.astype(BF16), moe_w2.astype(BF16)
    n_moe = router_w.shape[0]
    router_w_p = jnp.concatenate([router_w, jnp.zeros((n_moe, d, LANES - N_EXPERTS), F32)], axis=-1)
    router_b_p = jnp.concatenate([router_b, jnp.full((n_moe, LANES - N_EXPERTS), -1e30, F32)],
                                 axis=-1).reshape(n_moe, 1, LANES)

    xs = tokens
    for l in range(depth):
        lam_init = 0.8 - 0.6 * math.exp(-0.3 * l)
        qkv, rw, cv = _inproj(lay, l, xs, mods, wq, wr, wc, cos, sin, tm)
        last = l == depth - 1
        att_lat = _attention(lay, l, qkv, lam4, vec(subln_g), lam_init, tq, tk, latent=True)
        att_ctx = att_lat if last else _attention(lay, l, qkv, lam4, vec(subln_g), lam_init,
                                                  tq_ctx, tk, latent=False)
        bv, g, co, *terms = _prep_chunk_terms(lay, l, rw, cv, rkv_conv, w0, wd, a0, wa, wg,
                                              vec(k_k), vec(k_a), vec(r_k), conv_w, tm)
        yf, yb = _scan(lay, *terms, scan_chunks_per_step)
        last_moe = l == depth - 1 and l % 2 == 1
        n_rows = lay.n_lat if last_moe else lay.n_tok
        xs = _outproj(lay, l, att_lat, att_ctx, yf, yb, bv, g, co, xs, mods, w_out_b, vec(gn_g), vec(gn_b),
                      vec(ln1_g), vec(ln1_b), tm_wide, n_rows)
        if l % 2 == 0:
            xs = _ffn(lay, l, l // 2, xs, mods, ffn_w1_b, ffn_w3_b, ffn_w2_b, vec(ln2_g), vec(ln2_b), tm_wide)
        else:
            g_tok, rk, rkt, cnt = _router(lay, l, l // 2, xs, mods, router_w_p, router_b_p, tri, tm_moe,
                                          n_rows)
            counts = cnt[:, 0, :N_EXPERTS].reshape(-1)
            xs = _moe(lay, l, l // 2, xs, mods, counts, g_tok, rk, rkt, moe_w1_b, moe_w3_b, moe_w2_b,
                      vec(ln2_g), vec(ln2_b), tm_moe, moe_blk, n_rows)
    return xs[:lay.n_lat].reshape(batch, seq, d)
```

```python
import functools
import math

import numpy as np
import jax
import jax.numpy as jnp
from jax import lax
from jax.experimental import pallas as pl
from jax.experimental.pallas import tpu as pltpu

F32 = jnp.float32
BF16 = jnp.bfloat16

D_MODEL = 1024
DEPTH = 4
GRID_W = 64
ATT_HEADS = 4
ATT_QK = 64
ATT_V = 128
ATT_WIDTH = 512
AXIS_DIM = 32
ROPE_THETA = 10000.0
SUBLN_EPS = 1e-5
RWKV_HEADS = 4
RWKV_N = 64
RWKV_WIDTH = 256
DECAY_LORA = 32
ICLR_LORA = 32
GATE_LORA = 64
GN_EPS = 64e-5
CONV_WIDTH = 256
D_FF = 2816
N_EXPERTS = 8
D_FF_EXPERT = 1408
DN_ALPHA = (2 * DEPTH) ** 0.25
LN_EPS = 1e-5

LANES = 128
MOD_ROWS = 16
QKV_W = 3 * ATT_WIDTH
RW_W = 1024
CV_W = 3 * CONV_WIDTH
SCAN_CHUNK = 64
VMEM_LIMIT = 56 * 1024 * 1024


def _cparams(sem):
    return pltpu.CompilerParams(dimension_semantics=sem, vmem_limit_bytes=VMEM_LIMIT)


def _dot(a, b):
    return jnp.dot(a, b, preferred_element_type=F32)


def _split3(a):
    hi = a.astype(BF16)
    rest = a - hi.astype(F32)
    mid = rest.astype(BF16)
    return hi, mid, (rest - mid.astype(F32)).astype(BF16)


def _dot_mask_lhs(mask, a):
    m = jnp.where(mask, 1.0, 0.0).astype(BF16)
    return sum(jnp.dot(m, part, preferred_element_type=F32) for part in _split3(a))


def _dot_mask_rhs(a, mask):
    m = mask.astype(BF16)
    return sum(jnp.dot(part, m, preferred_element_type=F32) for part in _split3(a))


def _layer_norm(z, g, b):
    mu = jnp.mean(z, axis=-1, keepdims=True)
    zc = z - mu
    var = jnp.mean(zc * zc, axis=-1, keepdims=True)
    return zc * lax.rsqrt(var + LN_EPS) * g + b


def _head_ones(width, head):
    r = lax.broadcasted_iota(jnp.int32, (width, width), 0) // head
    c = lax.broadcasted_iota(jnp.int32, (width, width), 1) // head
    return (r == c).astype(F32)


def _ada_kernel(c_ref, w_ref, b_ref, o_ref):
    c = c_ref[...]
    sc = c * jax.nn.sigmoid(c)
    o_ref[...] = _dot(sc.astype(BF16), w_ref[...].astype(BF16)) + b_ref[...]


def _adaln(c_all, w_ada, b_ada):
    depth, d, n = w_ada.shape
    tn = 1536
    return pl.pallas_call(
        _ada_kernel,
        grid=(depth, n // tn),
        in_specs=[
            pl.BlockSpec((MOD_ROWS, d), lambda l, j: (0, 0)),
            pl.BlockSpec((None, d, tn), lambda l, j: (l, 0, j)),
            pl.BlockSpec((None, 1, tn), lambda l, j: (l, 0, j)),
        ],
        out_specs=pl.BlockSpec((None, MOD_ROWS, tn), lambda l, j: (l, 0, j)),
        out_shape=jax.ShapeDtypeStruct((depth, MOD_ROWS, n), F32),
        compiler_params=_cparams(("parallel", "parallel")),
        name="adaln",
    )(c_all, w_ada, b_ada.reshape(depth, 1, n))


class _Layout:
    def __init__(self, batch, ctx_len, seq):
        self.batch, self.ctx_len, self.seq = batch, ctx_len, seq
        self.n_lat = batch * seq
        self.n_tok = self.n_lat + batch * ctx_len

    def mod_spec(self, layer, tm, which):
        n_lat_tiles = self.n_lat // tm
        rows_per = self.seq // tm
        batch = self.batch

        def imap(i, *_):
            row = jnp.where(i < n_lat_tiles, i // rows_per, batch)
            return (layer * MOD_ROWS + row, 0, which)

        return pl.BlockSpec((None, 1, D_MODEL), imap)


def _inproj_kernel(x_ref, sh_ref, sc_ref, wq_ref, wr_ref, wc_ref, cos_ref, sin_ref,
                   oq_ref, or_ref, oc_ref):
    h = (x_ref[...] * (1.0 + sc_ref[...]) + sh_ref[...]).astype(BF16)
    qkv = _dot(h, wq_ref[...])
    cos = cos_ref[...]
    sin = sin_ref[...]
    lane = lax.broadcasted_iota(jnp.int32, cos.shape, 1)
    first_half = (lane % AXIS_DIM) < (AXIS_DIM // 2)
    for s in range(2 * ATT_WIDTH // LANES):
        seg = qkv[:, s * LANES:(s + 1) * LANES]
        swapped = jnp.where(first_half, pltpu.roll(seg, LANES - AXIS_DIM // 2, 1),
                            pltpu.roll(seg, AXIS_DIM // 2, 1))
        rot = seg * cos + swapped * sin
        if s < ATT_WIDTH // LANES:
            rot = rot * (ATT_QK ** -0.5 * math.log2(math.e))
        oq_ref[:, s * LANES:(s + 1) * LANES] = rot.astype(BF16)
    oq_ref[:, 2 * ATT_WIDTH:] = qkv[:, 2 * ATT_WIDTH:].astype(BF16)
    or_ref[...] = _dot(h, wr_ref[...])
    oc_ref[...] = _dot(h, wc_ref[...])


def _inproj(lay, layer, x, mods, wq, wr, wc, cos, sin, tm):
    n_lat_tiles = lay.n_lat // tm
    ctx_tiles_per = lay.ctx_len // tm
    lat_tiles_per = lay.seq // tm

    def rope_map(i):
        return (jnp.where(i < n_lat_tiles, i % lat_tiles_per,
                          lat_tiles_per + (i - n_lat_tiles) % ctx_tiles_per), 0)

    row = lambda i: (i, 0)
    wmap = lambda i: (layer, 0, 0)
    return pl.pallas_call(
        _inproj_kernel,
        grid=(lay.n_tok // tm,),
        in_specs=[
            pl.BlockSpec((tm, D_MODEL), row),
            lay.mod_spec(layer, tm, 0),
            lay.mod_spec(layer, tm, 1),
            pl.BlockSpec((None, D_MODEL, QKV_W), wmap),
            pl.BlockSpec((None, D_MODEL, RW_W), wmap),
            pl.BlockSpec((None, D_MODEL, CV_W), wmap),
            pl.BlockSpec((tm, LANES), rope_map),
            pl.BlockSpec((tm, LANES), rope_map),
        ],
        out_specs=[
            pl.BlockSpec((tm, QKV_W), row),
            pl.BlockSpec((tm, RW_W), row),
            pl.BlockSpec((tm, CV_W), row),
        ],
        out_shape=[
            jax.ShapeDtypeStruct((lay.n_tok, QKV_W), BF16),
            jax.ShapeDtypeStruct((lay.n_tok, RW_W), F32),
            jax.ShapeDtypeStruct((lay.n_tok, CV_W), F32),
        ],
        compiler_params=_cparams(("parallel",)),
        name="inproj",
    )(x, mods, mods, wq, wr, wc, cos, sin)


def _attn_kernel(*refs, lam_init, tq, tk):
    n_kv = (len(refs) - 4) // 3
    q_ref = refs[0]
    k_refs = refs[1:1 + 2 * n_kv:2]
    v_refs = refs[2:2 + 2 * n_kv:2]
    lam_ref, g_ref, o_ref = refs[1 + 2 * n_kv:4 + 2 * n_kv]
    vx_refs = refs[4 + 2 * n_kv:]

    @pl.when(pl.program_id(2) == 0)
    def _():
        for v_ref, vx_ref in zip(v_refs, vx_refs):
            vx_ref[:, :ATT_V] = v_ref[...]
            vx_ref[:, ATT_V:] = jnp.ones((vx_ref.shape[0], LANES), BF16)

    q = q_ref[...]
    lane = lax.broadcasted_iota(jnp.int32, q.shape, 1)
    zero = jnp.zeros_like(q)
    qq = jnp.concatenate([jnp.where(lane < ATT_QK, q, zero), jnp.where(lane >= ATT_QK, q, zero)], axis=0)
    lam4 = lam_ref[...]
    lam = (jnp.exp(jnp.sum(lam4[0:1] * lam4[1:2], axis=-1, keepdims=True))
           - jnp.exp(jnp.sum(lam4[2:3] * lam4[3:4], axis=-1, keepdims=True)) + lam_init)

    def scores(blk):
        k_ref, _, start, size = blk
        return lax.dot_general(qq, k_ref[start:start + size, :], _NT, preferred_element_type=F32)

    def attend(blocks):
        s = scores(blocks[0])
        m = acc = None
        for j, (_, vx_ref, start, size) in enumerate(blocks):
            s_next = scores(blocks[j + 1]) if j + 1 < len(blocks) else None
            bm = jnp.max(s, axis=-1, keepdims=True)
            m_new = bm if m is None else jnp.maximum(m, bm)
            p = jnp.exp2(s - m_new).astype(BF16)
            pv = _dot(p, vx_ref[start:start + size, :])
            acc = pv if m is None else acc * jnp.exp2(m - m_new) + pv
            m, s = m_new, s_next
        o = acc[:, :ATT_V] / acc[:, ATT_V:]
        dlt = o[:tq] - lam * o[tq:]
        ms = jnp.mean(dlt * dlt, axis=-1, keepdims=True)
        o_ref[...] = (dlt * lax.rsqrt(ms + SUBLN_EPS) * g_ref[...] * (1.0 - lam_init)).astype(o_ref.dtype)

    blocks = []
    for k_ref, vx_ref in zip(k_refs, vx_refs):
        step = min(tk, k_ref.shape[0])
        blocks += [(k_ref, vx_ref, s0, step) for s0 in range(0, k_ref.shape[0], step)]
    attend(blocks)


def _attention(lay, layer, qkv, lam4, subln_g, lam_init, tq, tk, latent):
    seg = lay.seq if latent else lay.ctx_len
    n_q = seg // tq
    q_blk0 = 0 if latent else lay.n_lat // tq
    ctx_blk0 = lay.n_lat // lay.ctx_len
    hq = ATT_WIDTH // LANES

    kv_specs = [pl.BlockSpec((lay.ctx_len, LANES), lambda b, h, qi: (ctx_blk0 + b, hq + h)),
                pl.BlockSpec((lay.ctx_len, LANES), lambda b, h, qi: (ctx_blk0 + b, 2 * hq + h))]
    scratch = [pltpu.VMEM((lay.ctx_len, ATT_V + LANES), BF16)]
    if latent:
        kv_specs += [pl.BlockSpec((lay.seq, LANES), lambda b, h, qi: (b, hq + h)),
                     pl.BlockSpec((lay.seq, LANES), lambda b, h, qi: (b, 2 * hq + h))]
        scratch += [pltpu.VMEM((lay.seq, ATT_V + LANES), BF16)]
    kern = functools.partial(_attn_kernel, lam_init=lam_init, tq=tq, tk=tk)
    return pl.pallas_call(
        kern,
        grid=(lay.batch, ATT_HEADS, n_q),
        in_specs=[pl.BlockSpec((tq, LANES), lambda b, h, qi: (q_blk0 + b * n_q + qi, h))] + kv_specs + [
            pl.BlockSpec((None, 4, ATT_QK), lambda b, h, qi: (layer, 0, 0)),
            pl.BlockSpec((None, 1, ATT_V), lambda b, h, qi: (layer, 0, 0)),
        ],
        out_specs=pl.BlockSpec((tq, LANES), lambda b, h, qi: (b * n_q + qi, h)),
        out_shape=jax.ShapeDtypeStruct((lay.batch * seg, ATT_WIDTH), BF16),
        scratch_shapes=scratch,
        compiler_params=_cparams(("parallel", "parallel", "arbitrary")),
        name="diff_attn" if latent else "diff_attn_ctx",
    )(*([qkv] * (1 + len(kv_specs))), lam4, subln_g)


def _prep_kernel(rw_ref, rwp_ref, rwn_ref, cv_ref, cvp_ref, cvn_ref,
                 rkvw_ref, w0_ref, wd_ref, a0_ref, wa_ref, wg_ref, kk_w_ref, ka_ref, rk_ref, cw_ref,
                 r_ref, v_ref, kk_ref, lw_ref, bb_ref, kd_ref, bv_ref, g_ref, co_ref,
                 *, tm, n_lat, ctx_len, seq):
    i = pl.program_id(0)
    start = i * tm
    in_lat = start < n_lat
    seg_pos = jnp.where(in_lat, start % seq, (start - n_lat) % ctx_len)
    seg_len = jnp.where(in_lat, seq, ctx_len)
    has_prev = seg_pos != 0
    has_next = seg_pos + tm != seg_len

    def shifted(u, prev_row, next_row):
        rows = lax.broadcasted_iota(jnp.int32, u.shape, 0)
        prev_row = jnp.where(has_prev, prev_row, jnp.zeros_like(prev_row))
        next_row = jnp.where(has_next, next_row, jnp.zeros_like(next_row))
        up = jnp.where(rows == 0, prev_row, pltpu.roll(u, 1, 0))
        un = jnp.where(rows == tm - 1, next_row, pltpu.roll(u, tm - 1, 0))
        return up, un

    def conv3(u, prev_row, next_row, w):
        up, un = shifted(u, prev_row, next_row)
        return up * w[0:1] + u * w[1:2] + un * w[2:3]

    rw = rw_ref[...]
    nrkv = 3 * RWKV_WIDTH
    rkv = conv3(rw[:, :nrkv], rwp_ref[7:8, :nrkv], rwn_ref[0:1, :nrkv], rkvw_ref[...])
    r = rkv[:, :RWKV_WIDTH]
    k = rkv[:, RWKV_WIDTH:2 * RWKV_WIDTH]
    v = rkv[:, 2 * RWKV_WIDTH:]

    lora = rw[:, nrkv:nrkv + LANES]
    gate = rw[:, nrkv + LANES:]
    wl = _dot3(_split(jnp.tanh(lora)), _split(wd_ref[...])) + w0_ref[...]
    lw = (-math.exp(-0.5)) * jax.nn.sigmoid(wl)
    a = jax.nn.sigmoid(_dot3(_split(lora), _split(wa_ref[...])) + a0_ref[...])
    g = _dot3(_split(jax.nn.sigmoid(gate)), _split(wg_ref[...]))

    ones = _head_ones(RWKV_WIDTH, RWKV_N)
    kraw = k * kk_w_ref[...]
    ss = _dot_mask_rhs(kraw * kraw, ones)
    kk = kraw * lax.rsqrt(jnp.maximum(ss, 1e-24))
    k2 = jnp.concatenate([k, k], axis=1)
    ka2 = jnp.concatenate([ka_ref[...], ka_ref[...]], axis=1)
    kd = k2 * (1.0 + (a - 1.0) * ka2)
    bb = jnp.concatenate([kk, kk], axis=1) * a
    bonus = _dot_mask_rhs(r * (kd[:, :RWKV_WIDTH] + kd[:, RWKV_WIDTH:]) * rk_ref[...], ones)

    r_ref[...] = r
    v_ref[...] = v
    kk_ref[...] = kk
    lw_ref[...] = lw
    bb_ref[...] = bb
    kd_ref[...] = kd
    bv_ref[...] = bonus * v
    g_ref[...] = g

    def gated(ref):
        return ref[:, 2 * CONV_WIDTH:] * ref[:, :CONV_WIDTH]

    cv_u = gated(cv_ref)
    conv = conv3(cv_u, gated(cvp_ref)[7:8], gated(cvn_ref)[0:1], cw_ref[...])
    co_ref[...] = (cv_ref[:, CONV_WIDTH:2 * CONV_WIDTH] * conv).astype(co_ref.dtype)


def _split(a):
    hi = a.astype(BF16)
    return hi, (a - hi.astype(F32)).astype(BF16)


_NN = (((1,), (0,)), ((), ()))
_NT = (((1,), (1,)), ((), ()))
_TN = (((0,), (0,)), ((), ()))


def _dot3(a, b, dims=_NN):
    (ah, al), (bh, bl) = a, b
    dg = lambda p, q: lax.dot_general(p, q, dims, preferred_element_type=F32)
    return dg(ah, bh) + dg(ah, bl) + dg(al, bh)


def _dot1(a, b, dims=_NN):
    return lax.dot_general(a.astype(BF16), b.astype(BF16), dims, preferred_element_type=F32)


def _chunk_terms_kernel(r_ref, v_ref, kk_ref, lw_ref, bb_ref, kd_ref, g_ref, h_ref, rq_ref, y0_ref):
    c = SCAN_CHUNK
    n = RWKV_N
    row = lax.broadcasted_iota(jnp.int32, (c, c), 0)
    col = lax.broadcasted_iota(jnp.int32, (c, c), 1)
    row2 = lax.broadcasted_iota(jnp.int32, (c, 2 * c), 0)
    col2 = lax.broadcasted_iota(jnp.int32, (c, 2 * c), 1) % c
    eye = lax.broadcasted_iota(jnp.int32, (n, n), 0) == lax.broadcasted_iota(jnp.int32, (n, n), 1)
    n_chunks = r_ref.shape[0] // c

    chains = []
    for ck in range(n_chunks):
        rows = slice(ck * c, (ck + 1) * c)
        r_all = r_ref[rows, :]
        v_all = v_ref[rows, :]
        kk = kk_ref[rows, :]
        for d in range(2):
            sgn = 1 if d == 0 else -1
            dsl = slice(d * RWKV_WIDTH, (d + 1) * RWKV_WIDTH)
            incl = (col - row) * sgn <= 0
            lw = lw_ref[rows, dsl]
            bb = bb_ref[rows, dsl]
            kd = kd_ref[rows, dsl]
            lp = _dot_mask_lhs(incl, lw)
            lt = jnp.sum(lw, axis=0, keepdims=True)
            p_inv = jnp.exp(-lp)
            p_end = jnp.exp(lt - lp)
            a_t = -kk * jnp.exp(lp - lw)
            b_t = bb * p_inv
            k_t = kd * p_inv
            r_t = r_all * jnp.exp(lp)
            k_e = kd * p_end
            b_e = bb * p_end
            p_tot = jnp.exp(lt)
            for h in range(RWKV_HEADS):
                sl = slice(h * n, (h + 1) * n)
                chains.append(dict(
                    strict=(col - row) * sgn < 0, incl2=(col2 - row2) * sgn <= 0,
                    ah=a_t[:, sl], rh=r_t[:, sl], vh=v_all[:, sl], bt=b_t[:, sl], kt=k_t[:, sl],
                    be=b_e[:, sl], ke=k_e[:, sl], ptot=p_tot[:, sl]))

    for ch in chains:
        ch["sc"] = _dot1(jnp.concatenate([ch["ah"], ch["rh"]], axis=0),
                         jnp.concatenate([ch["bt"], ch["kt"]], axis=0), _NT)
    for ch in chains:
        sc = ch["sc"]
        ch["nmat"] = jnp.where(ch["strict"], sc[:c, :c], 0.0).astype(BF16)
        ch["mr"] = jnp.where(ch["incl2"], sc[c:, :], 0.0).astype(BF16)
        ch["mkv"] = _dot1(jnp.where(ch["strict"], sc[:c, c:], 0.0), ch["vh"])
    for ch in chains:
        ch["z"] = jnp.concatenate([ch["ah"], ch["mkv"]], axis=1)
    n_factors = c.bit_length() - 1
    for p in range(n_factors):
        for ch in chains:
            ch["z"] = ch["z"] + _dot1(ch["nmat"], ch["z"])
        if p < n_factors - 1:
            for ch in chains:
                ch["nmat"] = _dot1(ch["nmat"], ch["nmat"]).astype(BF16)
    for ch in chains:
        z = ch["z"]
        ch["ws"] = z[:, :n].astype(BF16)
        ch["uv"] = jnp.concatenate([z[:, n:], ch["vh"]], axis=0).astype(BF16)
    for ch in chains:
        mr = ch["mr"]
        ch["rq"] = ch["rh"] + _dot1(mr[:, :c], ch["ws"])
        ch["y0"] = _dot1(mr, ch["uv"])
        ch["g"] = jnp.where(eye, ch["ptot"], 0.0) + _dot1(ch["ws"], ch["be"], _TN)
        ch["h"] = _dot1(ch["uv"], jnp.concatenate([ch["be"], ch["ke"]], axis=0), _TN)
    for ck in range(n_chunks):
        rows = slice(ck * c, (ck + 1) * c)
        for d in range(2):
            first = (ck * 2 + d) * RWKV_HEADS
            part = chains[first:first + RWKV_HEADS]
            for ref, key in ((g_ref, "g"), (h_ref, "h")):
                top = jnp.concatenate([ch[key] for ch in part], axis=1)
                if c > n:
                    top = jnp.concatenate([top, jnp.zeros((c - n, RWKV_WIDTH), F32)], axis=0)
                ref[d, rows, :] = top
            rq_ref[d, rows, :] = jnp.concatenate([ch["rq"] for ch in part], axis=1)
            y0_ref[d, rows, :] = jnp.concatenate([ch["y0"] for ch in part], axis=1)


def _prep_chunk_kernel(*refs, tm, n_lat, ctx_len, seq):
    ins, (bv_ref, g_ref, co_ref), terms, scratch = refs[:16], refs[16:19], refs[19:23], refs[23:]
    _prep_kernel(*ins, *scratch, bv_ref, g_ref, co_ref, tm=tm, n_lat=n_lat, ctx_len=ctx_len, seq=seq)
    _chunk_terms_kernel(*scratch, *terms)


def _prep_chunk_terms(lay, layer, rw, cv, rkv_conv, w0, wd, a0, wa, wg, k_k, k_a, r_k, conv_w, tm):
    n8 = lay.n_tok // 8
    t8 = tm // 8
    row = lambda i: (i, 0)
    prev = lambda i: (jnp.maximum(i * t8 - 1, 0), 0)
    nxt = lambda i: (jnp.minimum((i + 1) * t8, n8 - 1), 0)
    lmap = lambda i: (layer, 0, 0)
    w2 = 2 * RWKV_WIDTH
    kern = functools.partial(_prep_chunk_kernel, tm=tm, n_lat=lay.n_lat, ctx_len=lay.ctx_len, seq=lay.seq)
    f32 = lambda w: jax.ShapeDtypeStruct((lay.n_tok, w), F32)
    term = pl.BlockSpec((2, tm, RWKV_WIDTH), lambda i: (0, i, 0))
    term_shape = jax.ShapeDtypeStruct((2, lay.n_tok, RWKV_WIDTH), F32)
    return pl.pallas_call(
        kern,
        grid=(lay.n_tok // tm,),
        in_specs=[
            pl.BlockSpec((tm, RW_W), row), pl.BlockSpec((8, RW_W), prev), pl.BlockSpec((8, RW_W), nxt),
            pl.BlockSpec((tm, CV_W), row), pl.BlockSpec((8, CV_W), prev), pl.BlockSpec((8, CV_W), nxt),
            pl.BlockSpec((None, 3, 3 * RWKV_WIDTH), lmap),
            pl.BlockSpec((None, 1, w2), lmap),
            pl.BlockSpec((None, LANES, w2), lmap),
            pl.BlockSpec((None, 1, w2), lmap),
            pl.BlockSpec((None, LANES, w2), lmap),
            pl.BlockSpec((None, LANES, RWKV_WIDTH), lmap),
            pl.BlockSpec((None, 1, RWKV_WIDTH), lmap),
            pl.BlockSpec((None, 1, RWKV_WIDTH), lmap),
            pl.BlockSpec((None, 1, RWKV_WIDTH), lmap),
            pl.BlockSpec((None, 3, CONV_WIDTH), lmap),
        ],
        out_specs=[pl.BlockSpec((tm, RWKV_WIDTH), row), pl.BlockSpec((tm, RWKV_WIDTH), row),
                   pl.BlockSpec((tm, CONV_WIDTH), row), term, term, term, term],
        out_shape=[f32(RWKV_WIDTH), f32(RWKV_WIDTH), jax.ShapeDtypeStruct((lay.n_tok, CONV_WIDTH), BF16),
                   term_shape, term_shape, term_shape, term_shape],
        scratch_shapes=[pltpu.VMEM((tm, RWKV_WIDTH), F32)] * 3 + [pltpu.VMEM((tm, w2), F32)] * 3,
        compiler_params=_cparams(("parallel",)),
        name="rwkv_prep_chunk_terms",
    )(rw, rw, rw, cv, cv, cv, rkv_conv, w0, wd, a0, wa, wg, k_k, k_a, r_k, conv_w)


def _scan_kernel(gf_ref, hf_ref, rqf_ref, y0f_ref, gb_ref, hb_ref, rqb_ref, y0b_ref, yf_ref, yb_ref, st_ref):
    @pl.when(pl.program_id(1) == 0)
    def _():
        st_ref[...] = jnp.zeros_like(st_ref)

    dirs = ((gf_ref, hf_ref, rqf_ref, y0f_ref, yf_ref), (gb_ref, hb_ref, rqb_ref, y0b_ref, yb_ref))
    heads = [slice(h * RWKV_N, (h + 1) * RWKV_N) for h in range(RWKV_HEADS)]
    n_sub = gf_ref.shape[0] // SCAN_CHUNK
    state = [[st_ref[d, h] for h in range(RWKV_HEADS)] for d in range(2)]
    for step in range(n_sub):
        rows = [slice(k * SCAN_CHUNK, (k + 1) * SCAN_CHUNK) for k in (step, n_sub - 1 - step)]
        top = [slice(k * SCAN_CHUNK, k * SCAN_CHUNK + RWKV_N) for k in (step, n_sub - 1 - step)]
        s0 = [[_split(state[d][h]) for h in range(RWKV_HEADS)] for d in range(2)]
        state = [[_dot1(s0[d][h][0], dirs[d][0][top[d], sl]) + _dot1(s0[d][h][1], dirs[d][0][top[d], sl])
                  + dirs[d][1][top[d], sl] for h, sl in enumerate(heads)] for d in range(2)]
        for d in range(2):
            rq_ref, y0_ref, y_ref = dirs[d][2:]
            y_ref[rows[d], :] = jnp.concatenate(
                [_dot1(rq_ref[rows[d], sl], s0[d][h][0], _NT) + y0_ref[rows[d], sl]
                 for h, sl in enumerate(heads)], axis=1)
    for d in range(2):
        for h in range(RWKV_HEADS):
            st_ref[d, h] = state[d][h]


def _scan(lay, g, h, rq, y0, chunks_per_step):
    c = SCAN_CHUNK * chunks_per_step
    nc_ctx = lay.ctx_len // c
    nc_lat = lay.seq // c
    lat_blocks = lay.n_lat // c

    def blk(b, d, ci):
        if d == 0:
            return jnp.where(ci < nc_ctx, lat_blocks + b * nc_ctx + ci, b * nc_lat + (ci - nc_ctx))
        return jnp.where(ci < nc_ctx, lat_blocks + b * nc_ctx + (nc_ctx - 1 - ci),
                         b * nc_lat + (nc_lat - 1 - (ci - nc_ctx)))

    def term(d):
        return pl.BlockSpec((None, c, RWKV_WIDTH), lambda b, ci: (d, blk(b, d, ci), 0))

    def yspec(d):
        return pl.BlockSpec((c, RWKV_WIDTH), lambda b, ci: (blk(b, d, ci), 0))

    shp = jax.ShapeDtypeStruct((lay.n_tok, RWKV_WIDTH), F32)
    return pl.pallas_call(
        _scan_kernel,
        grid=(lay.batch, nc_ctx + nc_lat),
        in_specs=[term(0)] * 4 + [term(1)] * 4,
        out_specs=[yspec(0), yspec(1)],
        out_shape=[shp, shp],
        scratch_shapes=[pltpu.VMEM((2, RWKV_HEADS, RWKV_N, RWKV_N), F32)],
        compiler_params=_cparams(("parallel", "arbitrary")),
        name="rwkv_scan",
    )(g, h, rq, y0, g, h, rq, y0)


def _outproj_kernel(att_lat_ref, att_ctx_ref, yf_ref, yb_ref, bv_ref, g_ref, co_ref, x_ref, gate_ref, w_ref,
                    gng_ref, gnb_ref, lng_ref, lnb_ref, o_ref, *, n_lat_tiles):
    att = jnp.where(pl.program_id(0) < n_lat_tiles, att_lat_ref[...], att_ctx_ref[...])
    ones = _head_ones(RWKV_WIDTH, RWKV_N)
    y = yf_ref[...] + yb_ref[...]
    mu = _dot_mask_rhs(y, ones) * (1.0 / RWKV_N)
    yc = y - mu
    var = _dot_mask_rhs(yc * yc, ones) * (1.0 / RWKV_N)
    yn = yc * lax.rsqrt(var + GN_EPS) * gng_ref[...] + gnb_ref[...]
    rwkv = ((yn + bv_ref[...]) * g_ref[...]).astype(BF16)
    mix = (_dot(att, w_ref[:ATT_WIDTH, :])
           + _dot(rwkv, w_ref[ATT_WIDTH:ATT_WIDTH + RWKV_WIDTH, :])
           + _dot(co_ref[...], w_ref[ATT_WIDTH + RWKV_WIDTH:, :]))
    z = DN_ALPHA * x_ref[...] + gate_ref[...] * mix
    o_ref[...] = _layer_norm(z, lng_ref[...], lnb_ref[...])


def _outproj(lay, layer, att_lat, att_ctx, yf, yb, bv, g, co, x, mods, w_out, gn_g, gn_b, ln_g, ln_b, tm, n_rows):
    row = lambda i: (i, 0)
    lmap = lambda i: (layer, 0, 0)
    n_lat_tiles = lay.n_lat // tm
    return pl.pallas_call(
        functools.partial(_outproj_kernel, n_lat_tiles=n_lat_tiles),
        grid=(n_rows // tm,),
        in_specs=[
            pl.BlockSpec((tm, ATT_WIDTH), lambda i: (jnp.minimum(i, n_lat_tiles - 1), 0)),
            pl.BlockSpec((tm, ATT_WIDTH), lambda i: (jnp.maximum(i - n_lat_tiles, 0), 0)),
            pl.BlockSpec((tm, RWKV_WIDTH), row),
            pl.BlockSpec((tm, RWKV_WIDTH), row),
            pl.BlockSpec((tm, RWKV_WIDTH), row),
            pl.BlockSpec((tm, RWKV_WIDTH), row),
            pl.BlockSpec((tm, CONV_WIDTH), row),
            pl.BlockSpec((tm, D_MODEL), row),
            lay.mod_spec(layer, tm, 2),
            pl.BlockSpec((None, D_MODEL, D_MODEL), lmap),
            pl.BlockSpec((None, 1, RWKV_WIDTH), lmap),
            pl.BlockSpec((None, 1, RWKV_WIDTH), lmap),
            pl.BlockSpec((None, 1, D_MODEL), lmap),
            pl.BlockSpec((None, 1, D_MODEL), lmap),
        ],
        out_specs=pl.BlockSpec((tm, D_MODEL), row),
        out_shape=jax.ShapeDtypeStruct((n_rows, D_MODEL), F32),
        compiler_params=_cparams(("parallel",)),
        name="outproj_ln1",
    )(att_lat, att_ctx, yf, yb, bv, g, co, x, mods, w_out, gn_g, gn_b, ln_g, ln_b)


def _ffn_kernel(x_ref, sh_ref, sc_ref, gate_ref, w1_ref, w3_ref, w2_ref, lng_ref, lnb_ref, o_ref, *, n_split):
    x = x_ref[...]
    h = (x * (1.0 + sc_ref[...]) + sh_ref[...]).astype(BF16)
    step = D_FF // n_split
    f = jnp.zeros(x.shape, F32)
    for s in range(n_split):
        cols = slice(s * step, (s + 1) * step)
        a = _dot(h, w1_ref[:, cols])
        act = (a * jax.nn.sigmoid(a) * _dot(h, w3_ref[:, cols])).astype(BF16)
        f = f + _dot(act, w2_ref[cols, :])
    z = DN_ALPHA * x + gate_ref[...] * f
    o_ref[...] = _layer_norm(z, lng_ref[...], lnb_ref[...])


def _ffn(lay, layer, j, x, mods, w1, w3, w2, ln_g, ln_b, tm):
    row = lambda i: (i, 0)
    once = pl.Buffered(1)
    return pl.pallas_call(
        functools.partial(_ffn_kernel, n_split=2),
        grid=(lay.n_tok // tm,),
        in_specs=[
            pl.BlockSpec((tm, D_MODEL), row),
            lay.mod_spec(layer, tm, 3), lay.mod_spec(layer, tm, 4), lay.mod_spec(layer, tm, 5),
            pl.BlockSpec((None, D_MODEL, D_FF), lambda i: (j, 0, 0), pipeline_mode=once),
            pl.BlockSpec((None, D_MODEL, D_FF), lambda i: (j, 0, 0), pipeline_mode=once),
            pl.BlockSpec((None, D_FF, D_MODEL), lambda i: (j, 0, 0), pipeline_mode=once),
            pl.BlockSpec((None, 1, D_MODEL), lambda i: (layer, 0, 0)),
            pl.BlockSpec((None, 1, D_MODEL), lambda i: (layer, 0, 0)),
        ],
        out_specs=pl.BlockSpec((tm, D_MODEL), row),
        out_shape=jax.ShapeDtypeStruct((lay.n_tok, D_MODEL), F32),
        compiler_params=_cparams(("parallel",)),
        name="ffn_ln2",
    )(x, mods, mods, mods, w1, w3, w2, ln_g, ln_b)


def _router_kernel(x_ref, sh_ref, sc_ref, rw_ref, rb_ref, tri_ref, gt_ref, rk_ref, rkt_ref, cnt_ref):
    h = x_ref[...] * (1.0 + sc_ref[...]) + sh_ref[...]
    logits = _dot3(_split(h), _split(rw_ref[...])) + rb_ref[...]
    lane = lax.broadcasted_iota(jnp.int32, logits.shape, 1)
    m1 = jnp.max(logits, axis=-1, keepdims=True)
    i1 = jnp.min(jnp.where(logits == m1, lane, LANES), axis=-1, keepdims=True)
    rest = jnp.where(lane == i1, -jnp.inf, logits)
    m2 = jnp.max(rest, axis=-1, keepdims=True)
    i2 = jnp.min(jnp.where(rest == m2, lane, LANES), axis=-1, keepdims=True)
    e2 = jnp.exp(m2 - m1)
    den = 1.0 + e2
    gates = jnp.where(lane == i1, 1.0 / den, 0.0) + jnp.where(lane == i2, e2 / den, 0.0)
    gt_ref[...] = jnp.transpose(gates)[:N_EXPERTS, :]
    sel =jnp.where(lane == i1, 1.0, 0.0) + jnp.where(lane == i2, 1.0, 0.0)
    rank = _dot(tri_ref[...], sel.astype(BF16))
    rk = jnp.where(sel > 0.0, rank, -1.0)
    rk_ref[...] = rk
    rkt_ref[...] = jnp.transpose(rk)[:N_EXPERTS, :]
    cnt = jnp.sum(sel, axis=0, keepdims=True).astype(jnp.int32)
    cnt_ref[...] = jnp.broadcast_to(cnt, cnt_ref.shape)


def _router(lay, layer, j, x, mods, router_w, router_b, tri, tm, n_rows):
    n_tiles = n_rows // tm
    row = lambda i: (i, 0)
    return pl.pallas_call(
        _router_kernel,
        grid=(n_tiles,),
        in_specs=[
            pl.BlockSpec((tm, D_MODEL), row),
            lay.mod_spec(layer, tm, 3), lay.mod_spec(layer, tm, 4),
            pl.BlockSpec((None, D_MODEL, LANES), lambda i: (j, 0, 0)),
            pl.BlockSpec((None, 1, LANES), lambda i: (j, 0, 0)),
            pl.BlockSpec((tm, tm), lambda i: (0, 0)),
        ],
        out_specs=[
            pl.BlockSpec((None, N_EXPERTS, tm), lambda i: (i, 0, 0)),
            pl.BlockSpec((tm, LANES), row),
            pl.BlockSpec((None, N_EXPERTS, tm), lambda i: (i, 0, 0)),
            pl.BlockSpec((None, 8, LANES), lambda i: (i, 0, 0)),
        ],
        out_shape=[
            jax.ShapeDtypeStruct((n_tiles, N_EXPERTS, tm), F32),
            jax.ShapeDtypeStruct((n_rows, LANES), F32),
            jax.ShapeDtypeStruct((n_tiles, N_EXPERTS, tm), F32),
            jax.ShapeDtypeStruct((n_tiles, 8, LANES), jnp.int32),
        ],
        compiler_params=_cparams(("parallel",)),
        name="moe_router",
    )(x, mods, mods, router_w, router_b, tri)


def _moe_kernel(cnt_ref, x_ref, sh_ref, sc_ref, gate_ref, gt_ref, rk_ref, rkt_ref, w1_ref, w3_ref, w2_ref,
                lng_ref, lnb_ref, o_ref, h_ref, acc_ref, *, blk):
    i = pl.program_id(0)
    e = pl.program_id(1)
    tm = x_ref.shape[0]

    @pl.when(e == 0)
    def _():
        h_ref[...] = (x_ref[...] * (1.0 + sc_ref[...]) + sh_ref[...]).astype(BF16)
        acc_ref[...] = jnp.zeros_like(acc_ref)

    lane = lax.broadcasted_iota(jnp.int32, (tm, LANES), 1)
    rk_col = jnp.sum(jnp.where(lane == e, rk_ref[...], 0.0), axis=-1, keepdims=True)
    rk_row = rkt_ref[pl.ds(e, 1), :]
    g_row = gt_ref[pl.ds(e, 1), :]

    pos_r = lax.broadcasted_iota(jnp.int32, (blk, tm), 0).astype(F32)
    pos_c = lax.broadcasted_iota(jnp.int32, (tm, blk), 1).astype(F32)

    def block(jb, carry):
        base = (jb * blk).astype(F32)
        hit = rk_row - base == pos_r
        take = jnp.where(hit, 1.0, 0.0).astype(BF16)
        put = jnp.where(rk_col - base == pos_c, 1.0, 0.0).astype(BF16)
        hg = _dot(take, h_ref[...]).astype(BF16)
        a = _dot(hg, w1_ref[...])
        act = (a * jax.nn.sigmoid(a) * _dot(hg, w3_ref[...])).astype(BF16)
        f = _dot(act, w2_ref[...])
        ge = jnp.sum(jnp.where(hit, g_row, 0.0), axis=-1, keepdims=True)
        acc_ref[...] += _dot(put, (ge * f).astype(BF16))
        return carry

    n_routed = cnt_ref[i * N_EXPERTS + e]
    lax.fori_loop(0, (n_routed + blk - 1) // blk, block, 0)

    @pl.when(e == N_EXPERTS - 1)
    def _():
        z = DN_ALPHA * x_ref[...] + gate_ref[...] * acc_ref[...]
        o_ref[...] = _layer_norm(z, lng_ref[...], lnb_ref[...])


def _moe(lay, layer, j, x, mods, counts, g, rk, rkt, w1, w3, w2, ln_g, ln_b, tm, blk, n_rows):
    row = lambda i, e, cnt: (i, 0)
    grid_spec = pltpu.PrefetchScalarGridSpec(
        num_scalar_prefetch=1,
        grid=(n_rows // tm, N_EXPERTS),
        in_specs=[
            pl.BlockSpec((tm, D_MODEL), row),
            lay.mod_spec(layer, tm, 3), lay.mod_spec(layer, tm, 4), lay.mod_spec(layer, tm, 5),
            pl.BlockSpec((None, N_EXPERTS, tm), lambda i, e, cnt: (i, 0, 0)),
            pl.BlockSpec((tm, LANES), row),
            pl.BlockSpec((None, N_EXPERTS, tm), lambda i, e, cnt: (i, 0, 0)),
            pl.BlockSpec((None, None, D_MODEL, D_FF_EXPERT), lambda i, e, cnt: (j, e, 0, 0)),
            pl.BlockSpec((None, None, D_MODEL, D_FF_EXPERT), lambda i, e, cnt: (j, e, 0, 0)),
            pl.BlockSpec((None, None, D_FF_EXPERT, D_MODEL), lambda i, e, cnt: (j, e, 0, 0)),
            pl.BlockSpec((None, 1, D_MODEL), lambda i, e, cnt: (layer, 0, 0)),
            pl.BlockSpec((None, 1, D_MODEL), lambda i, e, cnt: (layer, 0, 0)),
        ],
        out_specs=pl.BlockSpec((tm, D_MODEL), row),
        scratch_shapes=[pltpu.VMEM((tm, D_MODEL), BF16), pltpu.VMEM((tm, D_MODEL), F32)],
    )
    return pl.pallas_call(
        functools.partial(_moe_kernel, blk=blk),
        grid_spec=grid_spec,
        out_shape=jax.ShapeDtypeStruct((n_rows, D_MODEL), F32),
        compiler_params=_cparams(("parallel", "arbitrary")),
        name="moe_ln2",
    )(counts, x, mods, mods, mods, g, rk, rkt, w1, w3, w2, ln_g, ln_b)


def _rope_tables(ctx_len, seq):
    t = np.arange(seq)
    inv = ROPE_THETA ** (-np.arange(0, AXIS_DIM, 2, dtype=np.float64) / AXIS_DIM)
    ang_r = (t // GRID_W)[:, None].astype(np.float64) * inv
    ang_c = (t % GRID_W)[:, None].astype(np.float64) * inv
    cos = np.concatenate([np.cos(ang_r), np.cos(ang_r), np.cos(ang_c), np.cos(ang_c)], axis=1)
    sin = np.concatenate([-np.sin(ang_r), np.sin(ang_r), -np.sin(ang_c), np.sin(ang_c)], axis=1)
    cos = np.concatenate([cos, np.ones((ctx_len, ATT_QK))], axis=0)
    sin = np.concatenate([sin, np.zeros((ctx_len, ATT_QK))], axis=0)
    reps = LANES // ATT_QK
    return (jnp.asarray(np.tile(cos, (1, reps)), F32), jnp.asarray(np.tile(sin, (1, reps)), F32))


def _tiles(batch, ctx_len, seq):
    tm = 4 * SCAN_CHUNK
    tm_wide = 2 * tm
    assert (batch * ctx_len) % tm_wide == 0 and seq % tm_wide == 0
    tq = 512
    tq_ctx = 256
    tk = next(t for t in (1024, 512, 256) if seq % t == 0)
    tm_moe = next(t for t in (1024, 512, 256) if (batch * ctx_len) % t == 0 and seq % t == 0)
    moe_blk = LANES
    scan_chunks_per_step = 4
    assert ctx_len % tm == 0 and seq % tm == 0 and ctx_len % tq_ctx == 0 and seq % tq == 0 and seq % tk == 0
    assert (batch * seq) % tq_ctx == 0
    assert seq % ctx_len == 0 and ctx_len % (SCAN_CHUNK * scan_chunks_per_step) == 0 and seq % GRID_W == 0
    return tm, tm_wide, tq, tq_ctx, tk, tm_moe, moe_blk, scan_chunks_per_step


def kernel(x, c, ctx, c_ctx, w_ada, b_ada, w_in, w_out, ln1_g, ln1_b, ln2_g, ln2_b,
           lam_q1, lam_k1, lam_q2, lam_k2, subln_g, rkv_conv, decay_w0, decay_up, iclr_a0, iclr_up,
           gate_up, k_k, k_a, r_k, gn_g, gn_b, conv_w, ffn_w1, ffn_w3, ffn_w2,
           router_w, router_b, moe_w1, moe_w3, moe_w2):
    batch, seq, d = x.shape
    ctx_len = ctx.shape[1]
    depth = w_in.shape[0]
    assert d == D_MODEL and depth == DEPTH and batch < MOD_ROWS
    tm, tm_wide, tq, tq_ctx, tk, tm_moe, moe_blk, scan_chunks_per_step = _tiles(batch, ctx_len, seq)
    tri = jnp.asarray(np.tri(tm_moe, k=-1), BF16)
    lay = _Layout(batch, ctx_len, seq)

    tokens = jnp.concatenate([x.reshape(batch * seq, d), ctx.reshape(batch * ctx_len, d)], axis=0)

    c_all = jnp.concatenate([c, c_ctx[None, :], jnp.zeros((MOD_ROWS - batch - 1, d), F32)], axis=0)
    mods = _adaln(c_all, w_ada, b_ada).reshape(depth * MOD_ROWS, 1, 6 * d)

    cos, sin = _rope_tables(ctx_len, seq)

    o = np.cumsum([0, 512, 512, 512, 256, 256, 256, 64, 64, 64, 256, 256, 256])
    wq = w_in[:, :, :o[3]].astype(BF16)
    wr = jnp.concatenate([w_in[:, :, o[3]:o[9]], jnp.zeros((depth, d, RW_W - (o[9] - o[3])), F32)],
                         axis=-1).astype(BF16)
    wc = w_in[:, :, o[9]:].astype(BF16)
    w_out_b = w_out.astype(BF16)

    z = jnp.zeros((depth, DECAY_LORA, RWKV_WIDTH), F32)
    wd = jnp.concatenate([
        jnp.concatenate([decay_up[:, 0], z], axis=-1),
        jnp.concatenate([z, decay_up[:, 1]], axis=-1),
        jnp.zeros((depth, 2 * ICLR_LORA, 2 * RWKV_WIDTH), F32)], axis=1)
    wa = jnp.concatenate([
        jnp.zeros((depth, 2 * DECAY_LORA, 2 * RWKV_WIDTH), F32),
        jnp.concatenate([iclr_up[:, 0], z], axis=-1),
        jnp.concatenate([z, iclr_up[:, 1]], axis=-1)], axis=1)
    wg = jnp.concatenate([gate_up, jnp.zeros((depth, LANES - GATE_LORA, RWKV_WIDTH), F32)], axis=1)
    w0 = decay_w0.reshape(depth, 1, 2 * RWKV_WIDTH)
    a0 = iclr_a0.reshape(depth, 1, 2 * RWKV_WIDTH)
    vec = lambda w: w.reshape(depth, 1, -1)

    lam4 = jnp.stack([lam_q1, lam_k1, lam_q2, lam_k2], axis=1)

    ffn_w1_b, ffn_w3_b, ffn_w2_b = ffn_w1.astype(BF16), ffn_w3.astype(BF16), ffn_w2.astype(BF16)
    moe_w1_b, moe_w3_b, moe_w2_b = moe_w1.astype(BF16), moe_w3.astype(BF16), moe_w2.astype(BF16)
    n_moe = router_w.shape[0]
    router_w_p = jnp.concatenate([router_w, jnp.zeros((n_moe, d, LANES - N_EXPERTS), F32)], axis=-1)
    router_b_p = jnp.concatenate([router_b, jnp.full((n_moe, LANES - N_EXPERTS), -1e30, F32)],
                                 axis=-1).reshape(n_moe, 1, LANES)

    xs = tokens
    for l in range(depth):
        lam_init = 0.8 - 0.6 * math.exp(-0.3 * l)
        qkv, rw, cv = _inproj(lay, l, xs, mods, wq, wr, wc, cos, sin, tm)
        last = l == depth - 1
        att_lat = _attention(lay, l, qkv, lam4, vec(subln_g), lam_init, tq, tk, latent=True)
        att_ctx = att_lat if last else _attention(lay, l, qkv, lam4, vec(subln_g), lam_init,
                                                  tq_ctx, tk, latent=False)
        bv, g, co, *terms = _prep_chunk_terms(lay, l, rw, cv, rkv_conv, w0, wd, a0, wa, wg,
                                              vec(k_k), vec(k_a), vec(r_k), conv_w, tm)
        yf, yb = _scan(lay, *terms, scan_chunks_per_step)
        last_moe = l == depth - 1 and l % 2 == 1
        n_rows = lay.n_lat if last_moe else lay.n_tok
        xs = _outproj(lay, l, att_lat, att_ctx, yf, yb, bv, g, co, xs, mods, w_out_b, vec(gn_g), vec(gn_b),
                      vec(ln1_g), vec(ln1_b), tm_wide, n_rows)
        if l % 2 == 0:
            xs = _ffn(lay, l, l // 2, xs, mods, ffn_w1_b, ffn_w3_b, ffn_w2_b, vec(ln2_g), vec(ln2_b), tm_wide)
        else:
            g_tok, rk, rkt, cnt = _router(lay, l, l // 2, xs, mods, router_w_p, router_b_p, tri, tm_moe,
                                          n_rows)
            counts = cnt[:, 0, :N_EXPERTS].reshape(-1)
            xs = _moe(lay, l, l // 2, xs, mods, counts, g_tok, rk, rkt, moe_w1_b, moe_w3_b, moe_w2_b,
                      vec(ln2_g), vec(ln2_b), tm_moe, moe_blk, n_rows)
    return xs[:lay.n_lat].reshape(batch, seq, d)
```

```python
import functools
import math

import numpy as np
import jax
import jax.numpy as jnp
from jax import lax
from jax.experimental import pallas as pl
from jax.experimental.pallas import tpu as pltpu

F32 = jnp.float32
BF16 = jnp.bfloat16

D_MODEL = 1024
DEPTH = 4
GRID_W = 64
ATT_HEADS = 4
ATT_QK = 64
ATT_V = 128
ATT_WIDTH = 512
AXIS_DIM = 32
ROPE_THETA = 10000.0
SUBLN_EPS = 1e-5
RWKV_HEADS = 4
RWKV_N = 64
RWKV_WIDTH = 256
DECAY_LORA = 32
ICLR_LORA = 32
GATE_LORA = 64
GN_EPS = 64e-5
CONV_WIDTH = 256
D_FF = 2816
N_EXPERTS = 8
D_FF_EXPERT = 1408
DN_ALPHA = (2 * DEPTH) ** 0.25
LN_EPS = 1e-5

LANES = 128
MOD_ROWS = 16
QKV_W = 3 * ATT_WIDTH
RW_W = 1024
CV_W = 3 * CONV_WIDTH
SCAN_CHUNK = 64
VMEM_LIMIT = 56 * 1024 * 1024


def _cparams(sem):
    return pltpu.CompilerParams(dimension_semantics=sem, vmem_limit_bytes=VMEM_LIMIT)


def _dot(a, b):
    return jnp.dot(a, b, preferred_element_type=F32)


def _split3(a):
    hi = a.astype(BF16)
    rest = a - hi.astype(F32)
    mid = rest.astype(BF16)
    return hi, mid, (rest - mid.astype(F32)).astype(BF16)


def _dot_mask_lhs(mask, a):
    m = jnp.where(mask, 1.0, 0.0).astype(BF16)
    return sum(jnp.dot(m, part, preferred_element_type=F32) for part in _split3(a))


def _dot_mask_rhs(a, mask):
    m = mask.astype(BF16)
    return sum(jnp.dot(part, m, preferred_element_type=F32) for part in _split3(a))


def _layer_norm(z, g, b):
    mu = jnp.mean(z, axis=-1, keepdims=True)
    zc = z - mu
    var = jnp.mean(zc * zc, axis=-1, keepdims=True)
    return zc * lax.rsqrt(var + LN_EPS) * g + b


def _head_ones(width, head):
    r = lax.broadcasted_iota(jnp.int32, (width, width), 0) // head
    c = lax.broadcasted_iota(jnp.int32, (width, width), 1) // head
    return (r == c).astype(F32)


def _ada_kernel(c_ref, w_ref, b_ref, o_ref):
    c = c_ref[...]
    sc = c * jax.nn.sigmoid(c)
    o_ref[...] = _dot(sc.astype(BF16), w_ref[...].astype(BF16)) + b_ref[...]


def _adaln(c_all, w_ada, b_ada):
    depth, d, n = w_ada.shape
    tn = 1536
    return pl.pallas_call(
        _ada_kernel,
        grid=(depth, n // tn),
        in_specs=[
            pl.BlockSpec((MOD_ROWS, d), lambda l, j: (0, 0)),
            pl.BlockSpec((None, d, tn), lambda l, j: (l, 0, j)),
            pl.BlockSpec((None, 1, tn), lambda l, j: (l, 0, j)),
        ],
        out_specs=pl.BlockSpec((None, MOD_ROWS, tn), lambda l, j: (l, 0, j)),
        out_shape=jax.ShapeDtypeStruct((depth, MOD_ROWS, n), F32),
        compiler_params=_cparams(("parallel", "parallel")),
        name="adaln",
    )(c_all, w_ada, b_ada.reshape(depth, 1, n))


class _Layout:
    def __init__(self, batch, ctx_len, seq):
        self.batch, self.ctx_len, self.seq = batch, ctx_len, seq
        self.n_lat = batch * seq
        self.n_tok = self.n_lat + batch * ctx_len

    def mod_spec(self, layer, tm, which):
        n_lat_tiles = self.n_lat // tm
        rows_per = self.seq // tm
        batch = self.batch

        def imap(i, *_):
            row = jnp.where(i < n_lat_tiles, i // rows_per, batch)
            return (layer * MOD_ROWS + row, 0, which)

        return pl.BlockSpec((None, 1, D_MODEL), imap)


def _inproj_kernel(x_ref, sh_ref, sc_ref, wq_ref, wr_ref, wc_ref, cos_ref, sin_ref,
                   oq_ref, or_ref, oc_ref):
    h = (x_ref[...] * (1.0 + sc_ref[...]) + sh_ref[...]).astype(BF16)
    qkv = _dot(h, wq_ref[...])
    cos = cos_ref[...]
    sin = sin_ref[...]
    lane = lax.broadcasted_iota(jnp.int32, cos.shape, 1)
    first_half = (lane % AXIS_DIM) < (AXIS_DIM // 2)
    for s in range(2 * ATT_WIDTH // LANES):
        seg = qkv[:, s * LANES:(s + 1) * LANES]
        swapped = jnp.where(first_half, pltpu.roll(seg, LANES - AXIS_DIM // 2, 1),
                            pltpu.roll(seg, AXIS_DIM // 2, 1))
        rot = seg * cos + swapped * sin
        if s < ATT_WIDTH // LANES:
            rot = rot * (ATT_QK ** -0.5 * math.log2(math.e))
        oq_ref[:, s * LANES:(s + 1) * LANES] = rot.astype(BF16)
    oq_ref[:, 2 * ATT_WIDTH:] = qkv[:, 2 * ATT_WIDTH:].astype(BF16)
    or_ref[...] = _dot(h, wr_ref[...])
    oc_ref[...] = _dot(h, wc_ref[...])


def _inproj(lay, layer, x, mods, wq, wr, wc, cos, sin, tm):
    n_lat_tiles = lay.n_lat // tm
    ctx_tiles_per = lay.ctx_len // tm
    lat_tiles_per = lay.seq // tm

    def rope_map(i):
        return (jnp.where(i < n_lat_tiles, i % lat_tiles_per,
                          lat_tiles_per + (i - n_lat_tiles) % ctx_tiles_per), 0)

    row = lambda i: (i, 0)
    wmap = lambda i: (layer, 0, 0)
    return pl.pallas_call(
        _inproj_kernel,
        grid=(lay.n_tok // tm,),
        in_specs=[
            pl.BlockSpec((tm, D_MODEL), row),
            lay.mod_spec(layer, tm, 0),
            lay.mod_spec(layer, tm, 1),
            pl.BlockSpec((None, D_MODEL, QKV_W), wmap),
            pl.BlockSpec((None, D_MODEL, RW_W), wmap),
            pl.BlockSpec((None, D_MODEL, CV_W), wmap),
            pl.BlockSpec((tm, LANES), rope_map),
            pl.BlockSpec((tm, LANES), rope_map),
        ],
        out_specs=[
            pl.BlockSpec((tm, QKV_W), row),
            pl.BlockSpec((tm, RW_W), row),
            pl.BlockSpec((tm, CV_W), row),
        ],
        out_shape=[
            jax.ShapeDtypeStruct((lay.n_tok, QKV_W), BF16),
            jax.ShapeDtypeStruct((lay.n_tok, RW_W), F32),
            jax.ShapeDtypeStruct((lay.n_tok, CV_W), F32),
        ],
        compiler_params=_cparams(("parallel",)),
        name="inproj",
    )(x, mods, mods, wq, wr, wc, cos, sin)


def _attn_kernel(*refs, lam_init, tq, tk):
    n_kv = (len(refs) - 4) // 3
    q_ref = refs[0]
    k_refs = refs[1:1 + 2 * n_kv:2]
    v_refs = refs[2:2 + 2 * n_kv:2]
    lam_ref, g_ref, o_ref = refs[1 + 2 * n_kv:4 + 2 * n_kv]
    vx_refs = refs[4 + 2 * n_kv:]

    @pl.when(pl.program_id(2) == 0)
    def _():
        for v_ref, vx_ref in zip(v_refs, vx_refs):
            vx_ref[:, :ATT_V] = v_ref[...]
            vx_ref[:, ATT_V:] = jnp.ones((vx_ref.shape[0], LANES), BF16)

    q = q_ref[...]
    lane = lax.broadcasted_iota(jnp.int32, q.shape, 1)
    zero = jnp.zeros_like(q)
    qq = jnp.concatenate([jnp.where(lane < ATT_QK, q, zero), jnp.where(lane >= ATT_QK, q, zero)], axis=0)
    lam4 = lam_ref[...]
    lam = (jnp.exp(jnp.sum(lam4[0:1] * lam4[1:2], axis=-1, keepdims=True))
           - jnp.exp(jnp.sum(lam4[2:3] * lam4[3:4], axis=-1, keepdims=True)) + lam_init)

    def scores(blk):
        k_ref, _, start, size = blk
        return lax.dot_general(qq, k_ref[start:start + size, :], _NT, preferred_element_type=F32)

    def attend(blocks):
        s = scores(blocks[0])
        m = acc = None
        for j, (_, vx_ref, start, size) in enumerate(blocks):
            s_next = scores(blocks[j + 1]) if j + 1 < len(blocks) else None
            bm = jnp.max(s, axis=-1, keepdims=True)
            m_new = bm if m is None else jnp.maximum(m, bm)
            p = jnp.exp2(s - m_new).astype(BF16)
            pv = _dot(p, vx_ref[start:start + size, :])
            acc = pv if m is None else acc * jnp.exp2(m - m_new) + pv
            m, s = m_new, s_next
        o = acc[:, :ATT_V] / acc[:, ATT_V:]
        dlt = o[:tq] - lam * o[tq:]
        ms = jnp.mean(dlt * dlt, axis=-1, keepdims=True)
        o_ref[...] = (dlt * lax.rsqrt(ms + SUBLN_EPS) * g_ref[...] * (1.0 - lam_init)).astype(o_ref.dtype)

    blocks = []
    for k_ref, vx_ref in zip(k_refs, vx_refs):
        step = min(tk, k_ref.shape[0])
        blocks += [(k_ref, vx_ref, s0, step) for s0 in range(0, k_ref.shape[0], step)]
    attend(blocks)


def _attention(lay, layer, qkv, lam4, subln_g, lam_init, tq, tk, latent):
    seg = lay.seq if latent else lay.ctx_len
    n_q = seg // tq
    q_blk0 = 0 if latent else lay.n_lat // tq
    ctx_blk0 = lay.n_lat // lay.ctx_len
    hq = ATT_WIDTH // LANES

    kv_specs = [pl.BlockSpec((lay.ctx_len, LANES), lambda b, h, qi: (ctx_blk0 + b, hq + h)),
                pl.BlockSpec((lay.ctx_len, LANES), lambda b, h, qi: (ctx_blk0 + b, 2 * hq + h))]
    scratch = [pltpu.VMEM((lay.ctx_len, ATT_V + LANES), BF16)]
    if latent:
        kv_specs += [pl.BlockSpec((lay.seq, LANES), lambda b, h, qi: (b, hq + h)),
                     pl.BlockSpec((lay.seq, LANES), lambda b, h, qi: (b, 2 * hq + h))]
        scratch += [pltpu.VMEM((lay.seq, ATT_V + LANES), BF16)]
    kern = functools.partial(_attn_kernel, lam_init=lam_init, tq=tq, tk=tk)
    return pl.pallas_call(
        kern,
        grid=(lay.batch, ATT_HEADS, n_q),
        in_specs=[pl.BlockSpec((tq, LANES), lambda b, h, qi: (q_blk0 + b * n_q + qi, h))] + kv_specs + [
            pl.BlockSpec((None, 4, ATT_QK), lambda b, h, qi: (layer, 0, 0)),
            pl.BlockSpec((None, 1, ATT_V), lambda b, h, qi: (layer, 0, 0)),
        ],
        out_specs=pl.BlockSpec((tq, LANES), lambda b, h, qi: (b * n_q + qi, h)),
        out_shape=jax.ShapeDtypeStruct((lay.batch * seg, ATT_WIDTH), BF16),
        scratch_shapes=scratch,
        compiler_params=_cparams(("parallel", "parallel", "arbitrary")),
        name="diff_attn" if latent else "diff_attn_ctx",
    )(*([qkv] * (1 + len(kv_specs))), lam4, subln_g)


def _prep_kernel(rw_ref, rwp_ref, rwn_ref, cv_ref, cvp_ref, cvn_ref,
                 rkvw_ref, w0_ref, wd_ref, a0_ref, wa_ref, wg_ref, kk_w_ref, ka_ref, rk_ref, cw_ref,
                 r_ref, v_ref, kk_ref, lw_ref, bb_ref, kd_ref, bv_ref, g_ref, co_ref,
                 *, tm, n_lat, ctx_len, seq):
    i = pl.program_id(0)
    start = i * tm
    in_lat = start < n_lat
    seg_pos = jnp.where(in_lat, start % seq, (start - n_lat) % ctx_len)
    seg_len = jnp.where(in_lat, seq, ctx_len)
    has_prev = seg_pos != 0
    has_next = seg_pos + tm != seg_len

    def shifted(u, prev_row, next_row):
        rows = lax.broadcasted_iota(jnp.int32, u.shape, 0)
        prev_row = jnp.where(has_prev, prev_row, jnp.zeros_like(prev_row))
        next_row = jnp.where(has_next, next_row, jnp.zeros_like(next_row))
        up = jnp.where(rows == 0, prev_row, pltpu.roll(u, 1, 0))
        un = jnp.where(rows == tm - 1, next_row, pltpu.roll(u, tm - 1, 0))
        return up, un

    def conv3(u, prev_row, next_row, w):
        up, un = shifted(u, prev_row, next_row)
        return up * w[0:1] + u * w[1:2] + un * w[2:3]

    rw = rw_ref[...]
    nrkv = 3 * RWKV_WIDTH
    rkv = conv3(rw[:, :nrkv], rwp_ref[7:8, :nrkv], rwn_ref[0:1, :nrkv], rkvw_ref[...])
    r = rkv[:, :RWKV_WIDTH]
    k = rkv[:, RWKV_WIDTH:2 * RWKV_WIDTH]
    v = rkv[:, 2 * RWKV_WIDTH:]

    lora = rw[:, nrkv:nrkv + LANES]
    gate = rw[:, nrkv + LANES:]
    wl = _dot3(_split(jnp.tanh(lora)), _split(wd_ref[...])) + w0_ref[...]
    lw = (-math.exp(-0.5)) * jax.nn.sigmoid(wl)
    a = jax.nn.sigmoid(_dot3(_split(lora), _split(wa_ref[...])) + a0_ref[...])
    g = _dot3(_split(jax.nn.sigmoid(gate)), _split(wg_ref[...]))

    ones = _head_ones(RWKV_WIDTH, RWKV_N)
    kraw = k * kk_w_ref[...]
    ss = _dot_mask_rhs(kraw * kraw, ones)
    kk = kraw * lax.rsqrt(jnp.maximum(ss, 1e-24))
    k2 = jnp.concatenate([k, k], axis=1)
    ka2 = jnp.concatenate([ka_ref[...], ka_ref[...]], axis=1)
    kd = k2 * (1.0 + (a - 1.0) * ka2)
    bb = jnp.concatenate([kk, kk], axis=1) * a
    bonus = _dot_mask_rhs(r * (kd[:, :RWKV_WIDTH] + kd[:, RWKV_WIDTH:]) * rk_ref[...], ones)

    r_ref[...] = r
    v_ref[...] = v
    kk_ref[...] = kk
    lw_ref[...] = lw
    bb_ref[...] = bb
    kd_ref[...] = kd
    bv_ref[...] = bonus * v
    g_ref[...] = g

    def gated(ref):
        return ref[:, 2 * CONV_WIDTH:] * ref[:, :CONV_WIDTH]

    cv_u = gated(cv_ref)
    conv = conv3(cv_u, gated(cvp_ref)[7:8], gated(cvn_ref)[0:1], cw_ref[...])
    co_ref[...] = (cv_ref[:, CONV_WIDTH:2 * CONV_WIDTH] * conv).astype(co_ref.dtype)


def _split(a):
    hi = a.astype(BF16)
    return hi, (a - hi.astype(F32)).astype(BF16)


_NN = (((1,), (0,)), ((), ()))
_NT = (((1,), (1,)), ((), ()))
_TN = (((0,), (0,)), ((), ()))


def _dot3(a, b, dims=_NN):
    (ah, al), (bh, bl) = a, b
    dg = lambda p, q: lax.dot_general(p, q, dims, preferred_element_type=F32)
    return dg(ah, bh) + dg(ah, bl) + dg(al, bh)


def _dot1(a, b, dims=_NN):
    return lax.dot_general(a.astype(BF16), b.astype(BF16), dims, preferred_element_type=F32)


def _chunk_terms_kernel(r_ref, v_ref, kk_ref, lw_ref, bb_ref, kd_ref, g_ref, h_ref, rq_ref, y0_ref):
    c = SCAN_CHUNK
    n = RWKV_N
    row = lax.broadcasted_iota(jnp.int32, (c, c), 0)
    col = lax.broadcasted_iota(jnp.int32, (c, c), 1)
    row2 = lax.broadcasted_iota(jnp.int32, (c, 2 * c), 0)
    col2 = lax.broadcasted_iota(jnp.int32, (c, 2 * c), 1) % c
    eye = lax.broadcasted_iota(jnp.int32, (n, n), 0) == lax.broadcasted_iota(jnp.int32, (n, n), 1)
    n_chunks = r_ref.shape[0] // c

    chains = []
    for ck in range(n_chunks):
        rows = slice(ck * c, (ck + 1) * c)
        r_all = r_ref[rows, :]
        v_all = v_ref[rows, :]
        kk = kk_ref[rows, :]
        for d in range(2):
            sgn = 1 if d == 0 else -1
            dsl = slice(d * RWKV_WIDTH, (d + 1) * RWKV_WIDTH)
            incl = (col - row) * sgn <= 0
            lw = lw_ref[rows, dsl]
            bb = bb_ref[rows, dsl]
            kd = kd_ref[rows, dsl]
            lp = _dot_mask_lhs(incl, lw)
            lt = jnp.sum(lw, axis=0, keepdims=True)
            p_inv = jnp.exp(-lp)
            p_end = jnp.exp(lt - lp)
            a_t = -kk * jnp.exp(lp - lw)
            b_t = bb * p_inv
            k_t = kd * p_inv
            r_t = r_all * jnp.exp(lp)
            k_e = kd * p_end
            b_e = bb * p_end
            p_tot = jnp.exp(lt)
            for h in range(RWKV_HEADS):
                sl = slice(h * n, (h + 1) * n)
                chains.append(dict(
                    strict=(col - row) * sgn < 0, incl2=(col2 - row2) * sgn <= 0,
                    ah=a_t[:, sl], rh=r_t[:, sl], vh=v_all[:, sl], bt=b_t[:, sl], kt=k_t[:, sl],
                    be=b_e[:, sl], ke=k_e[:, sl], ptot=p_tot[:, sl]))

    for ch in chains:
        ch["sc"] = _dot1(jnp.concatenate([ch["ah"], ch["rh"]], axis=0),
                         jnp.concatenate([ch["bt"], ch["kt"]], axis=0), _NT)
    for ch in chains:
        sc = ch["sc"]
        ch["nmat"] = jnp.where(ch["strict"], sc[:c, :c], 0.0).astype(BF16)
        ch["mr"] = jnp.where(ch["incl2"], sc[c:, :], 0.0).astype(BF16)
        ch["mkv"] = _dot1(jnp.where(ch["strict"], sc[:c, c:], 0.0), ch["vh"])
    for ch in chains:
        ch["z"] = jnp.concatenate([ch["ah"], ch["mkv"]], axis=1)
    n_factors = c.bit_length() - 1
    for p in range(n_factors):
        for ch in chains:
            ch["z"] = ch["z"] + _dot1(ch["nmat"], ch["z"])
        if p < n_factors - 1:
            for ch in chains:
                ch["nmat"] = _dot1(ch["nmat"], ch["nmat"]).astype(BF16)
    for ch in chains:
        z = ch["z"]
        ch["ws"] = z[:, :n].astype(BF16)
        ch["uv"] = jnp.concatenate([z[:, n:], ch["vh"]], axis=0).astype(BF16)
    for ch in chains:
        mr = ch["mr"]
        ch["rq"] = ch["rh"] + _dot1(mr[:, :c], ch["ws"])
        ch["y0"] = _dot1(mr, ch["uv"])
        ch["g"] = jnp.where(eye, ch["ptot"], 0.0) + _dot1(ch["ws"], ch["be"], _TN)
        ch["h"] = _dot1(ch["uv"], jnp.concatenate([ch["be"], ch["ke"]], axis=0), _TN)
    for ck in range(n_chunks):
        rows = slice(ck * c, (ck + 1) * c)
        for d in range(2):
            first = (ck * 2 + d) * RWKV_HEADS
            part = chains[first:first + RWKV_HEADS]
            for ref, key in ((g_ref, "g"), (h_ref, "h")):
                top = jnp.concatenate([ch[key] for ch in part], axis=1)
                if c > n:
                    top = jnp.concatenate([top, jnp.zeros((c - n, RWKV_WIDTH), F32)], axis=0)
                ref[d, rows, :] = top
            rq_ref[d, rows, :] = jnp.concatenate([ch["rq"] for ch in part], axis=1)
            y0_ref[d, rows, :] = jnp.concatenate([ch["y0"] for ch in part], axis=1)


def _prep_chunk_kernel(*refs, tm, n_lat, ctx_len, seq):
    ins, (bv_ref, g_ref, co_ref), terms, scratch = refs[:16], refs[16:19], refs[19:23], refs[23:]
    _prep_kernel(*ins, *scratch, bv_ref, g_ref, co_ref, tm=tm, n_lat=n_lat, ctx_len=ctx_len, seq=seq)
    _chunk_terms_kernel(*scratch, *terms)


def _prep_chunk_terms(lay, layer, rw, cv, rkv_conv, w0, wd, a0, wa, wg, k_k, k_a, r_k, conv_w, tm):
    n8 = lay.n_tok // 8
    t8 = tm // 8
    row = lambda i: (i, 0)
    prev = lambda i: (jnp.maximum(i * t8 - 1, 0), 0)
    nxt = lambda i: (jnp.minimum((i + 1) * t8, n8 - 1), 0)
    lmap = lambda i: (layer, 0, 0)
    w2 = 2 * RWKV_WIDTH
    kern = functools.partial(_prep_chunk_kernel, tm=tm, n_lat=lay.n_lat, ctx_len=lay.ctx_len, seq=lay.seq)
    f32 = lambda w: jax.ShapeDtypeStruct((lay.n_tok, w), F32)
    term = pl.BlockSpec((2, tm, RWKV_WIDTH), lambda i: (0, i, 0))
    term_shape = jax.ShapeDtypeStruct((2, lay.n_tok, RWKV_WIDTH), F32)
    return pl.pallas_call(
        kern,
        grid=(lay.n_tok // tm,),
        in_specs=[
            pl.BlockSpec((tm, RW_W), row), pl.BlockSpec((8, RW_W), prev), pl.BlockSpec((8, RW_W), nxt),
            pl.BlockSpec((tm, CV_W), row), pl.BlockSpec((8, CV_W), prev), pl.BlockSpec((8, CV_W), nxt),
            pl.BlockSpec((None, 3, 3 * RWKV_WIDTH), lmap),
            pl.BlockSpec((None, 1, w2), lmap),
            pl.BlockSpec((None, LANES, w2), lmap),
            pl.BlockSpec((None, 1, w2), lmap),
            pl.BlockSpec((None, LANES, w2), lmap),
            pl.BlockSpec((None, LANES, RWKV_WIDTH), lmap),
            pl.BlockSpec((None, 1, RWKV_WIDTH), lmap),
            pl.BlockSpec((None, 1, RWKV_WIDTH), lmap),
            pl.BlockSpec((None, 1, RWKV_WIDTH), lmap),
            pl.BlockSpec((None, 3, CONV_WIDTH), lmap),
        ],
        out_specs=[pl.BlockSpec((tm, RWKV_WIDTH), row), pl.BlockSpec((tm, RWKV_WIDTH), row),
                   pl.BlockSpec((tm, CONV_WIDTH), row), term, term, term, term],
        out_shape=[f32(RWKV_WIDTH), f32(RWKV_WIDTH), jax.ShapeDtypeStruct((lay.n_tok, CONV_WIDTH), BF16),
                   term_shape, term_shape, term_shape, term_shape],
        scratch_shapes=[pltpu.VMEM((tm, RWKV_WIDTH), F32)] * 3 + [pltpu.VMEM((tm, w2), F32)] * 3,
        compiler_params=_cparams(("parallel",)),
        name="rwkv_prep_chunk_terms",
    )(rw, rw, rw, cv, cv, cv, rkv_conv, w0, wd, a0, wa, wg, k_k, k_a, r_k, conv_w)


def _scan_kernel(gf_ref, hf_ref, rqf_ref, y0f_ref, gb_ref, hb_ref, rqb_ref, y0b_ref, yf_ref, yb_ref, st_ref):
    @pl.when(pl.program_id(1) == 0)
    def _():
        st_ref[...] = jnp.zeros_like(st_ref)

    dirs = ((gf_ref, hf_ref, rqf_ref, y0f_ref, yf_ref), (gb_ref, hb_ref, rqb_ref, y0b_ref, yb_ref))
    heads = [slice(h * RWKV_N, (h + 1) * RWKV_N) for h in range(RWKV_HEADS)]
    n_sub = gf_ref.shape[0] // SCAN_CHUNK
    state = [[st_ref[d, h] for h in range(RWKV_HEADS)] for d in range(2)]
    for step in range(n_sub):
        rows = [slice(k * SCAN_CHUNK, (k + 1) * SCAN_CHUNK) for k in (step, n_sub - 1 - step)]
        top = [slice(k * SCAN_CHUNK, k * SCAN_CHUNK + RWKV_N) for k in (step, n_sub - 1 - step)]
        s0 = [[_split(state[d][h]) for h in range(RWKV_HEADS)] for d in range(2)]
        state = [[_dot1(s0[d][h][0], dirs[d][0][top[d], sl]) + _dot1(s0[d][h][1], dirs[d][0][top[d], sl])
                  + dirs[d][1][top[d], sl] for h, sl in enumerate(heads)] for d in range(2)]
        for d in range(2):
            rq_ref, y0_ref, y_ref = dirs[d][2:]
            y_ref[rows[d], :] = jnp.concatenate(
                [_dot1(rq_ref[rows[d], sl], s0[d][h][0], _NT) + y0_ref[rows[d], sl]
                 for h, sl in enumerate(heads)], axis=1)
    for d in range(2):
        for h in range(RWKV_HEADS):
            st_ref[d, h] = state[d][h]


def _scan(lay, g, h, rq, y0, chunks_per_step):
    c = SCAN_CHUNK * chunks_per_step
    nc_ctx = lay.ctx_len // c
    nc_lat = lay.seq // c
    lat_blocks = lay.n_lat // c

    def blk(b, d, ci):
        if d == 0:
            return jnp.where(ci < nc_ctx, lat_blocks + b * nc_ctx + ci, b * nc_lat + (ci - nc_ctx))
        return jnp.where(ci < nc_ctx, lat_blocks + b * nc_ctx + (nc_ctx - 1 - ci),
                         b * nc_lat + (nc_lat - 1 - (ci - nc_ctx)))

    def term(d):
        return pl.BlockSpec((None, c, RWKV_WIDTH), lambda b, ci: (d, blk(b, d, ci), 0))

    def yspec(d):
        return pl.BlockSpec((c, RWKV_WIDTH), lambda b, ci: (blk(b, d, ci), 0))

    shp = jax.ShapeDtypeStruct((lay.n_tok, RWKV_WIDTH), F32)
    return pl.pallas_call(
        _scan_kernel,
        grid=(lay.batch, nc_ctx + nc_lat),
        in_specs=[term(0)] * 4 + [term(1)] * 4,
        out_specs=[yspec(0), yspec(1)],
        out_shape=[shp, shp],
        scratch_shapes=[pltpu.VMEM((2, RWKV_HEADS, RWKV_N, RWKV_N), F32)],
        compiler_params=_cparams(("parallel", "arbitrary")),
        name="rwkv_scan",
    )(g, h, rq, y0, g, h, rq, y0)


def _outproj_kernel(att_lat_ref, att_ctx_ref, yf_ref, yb_ref, bv_ref, g_ref, co_ref, x_ref, gate_ref, w_ref,
                    gng_ref, gnb_ref, lng_ref, lnb_ref, o_ref, *, n_lat_tiles):
    att = jnp.where(pl.program_id(0) < n_lat_tiles, att_lat_ref[...], att_ctx_ref[...])
    ones = _head_ones(RWKV_WIDTH, RWKV_N)
    y = yf_ref[...] + yb_ref[...]
    mu = _dot_mask_rhs(y, ones) * (1.0 / RWKV_N)
    yc = y - mu
    var = _dot_mask_rhs(yc * yc, ones) * (1.0 / RWKV_N)
    yn = yc * lax.rsqrt(var + GN_EPS) * gng_ref[...] + gnb_ref[...]
    rwkv = ((yn + bv_ref[...]) * g_ref[...]).astype(BF16)
    mix = (_dot(att, w_ref[:ATT_WIDTH, :])
           + _dot(rwkv, w_ref[ATT_WIDTH:ATT_WIDTH + RWKV_WIDTH, :])
           + _dot(co_ref[...], w_ref[ATT_WIDTH + RWKV_WIDTH:, :]))
    z = DN_ALPHA * x_ref[...] + gate_ref[...] * mix
    o_ref[...] = _layer_norm(z, lng_ref[...], lnb_ref[...])


def _outproj(lay, layer, att_lat, att_ctx, yf, yb, bv, g, co, x, mods, w_out, gn_g, gn_b, ln_g, ln_b, tm, n_rows):
    row = lambda i: (i, 0)
    lmap = lambda i: (layer, 0, 0)
    n_lat_tiles = lay.n_lat // tm
    return pl.pallas_call(
        functools.partial(_outproj_kernel, n_lat_tiles=n_lat_tiles),
        grid=(n_rows // tm,),
        in_specs=[
            pl.BlockSpec((tm, ATT_WIDTH), lambda i: (jnp.minimum(i, n_lat_tiles - 1), 0)),
            pl.BlockSpec((tm, ATT_WIDTH), lambda i: (jnp.maximum(i - n_lat_tiles, 0), 0)),
            pl.BlockSpec((tm, RWKV_WIDTH), row),
            pl.BlockSpec((tm, RWKV_WIDTH), row),
            pl.BlockSpec((tm, RWKV_WIDTH), row),
            pl.BlockSpec((tm, RWKV_WIDTH), row),
            pl.BlockSpec((tm, CONV_WIDTH), row),
            pl.BlockSpec((tm, D_MODEL), row),
            lay.mod_spec(layer, tm, 2),
            pl.BlockSpec((None, D_MODEL, D_MODEL), lmap),
            pl.BlockSpec((None, 1, RWKV_WIDTH), lmap),
            pl.BlockSpec((None, 1, RWKV_WIDTH), lmap),
            pl.BlockSpec((None, 1, D_MODEL), lmap),
            pl.BlockSpec((None, 1, D_MODEL), lmap),
        ],
        out_specs=pl.BlockSpec((tm, D_MODEL), row),
        out_shape=jax.ShapeDtypeStruct((n_rows, D_MODEL), F32),
        compiler_params=_cparams(("parallel",)),
        name="outproj_ln1",
    )(att_lat, att_ctx, yf, yb, bv, g, co, x, mods, w_out, gn_g, gn_b, ln_g, ln_b)


def _ffn_kernel(x_ref, sh_ref, sc_ref, gate_ref, w1_ref, w3_ref, w2_ref, lng_ref, lnb_ref, o_ref, *, n_split):
    x = x_ref[...]
    h = (x * (1.0 + sc_ref[...]) + sh_ref[...]).astype(BF16)
    step = D_FF // n_split
    f = jnp.zeros(x.shape, F32)
    for s in range(n_split):
        cols = slice(s * step, (s + 1) * step)
        a = _dot(h, w1_ref[:, cols])
        act = (a * jax.nn.sigmoid(a) * _dot(h, w3_ref[:, cols])).astype(BF16)
        f = f + _dot(act, w2_ref[cols, :])
    z = DN_ALPHA * x + gate_ref[...] * f
    o_ref[...] = _layer_norm(z, lng_ref[...], lnb_ref[...])


def _ffn(lay, layer, j, x, mods, w1, w3, w2, ln_g, ln_b, tm):
    row = lambda i: (i, 0)
    once = pl.Buffered(1)
    return pl.pallas_call(
        functools.partial(_ffn_kernel, n_split=2),
        grid=(lay.n_tok // tm,),
        in_specs=[
            pl.BlockSpec((tm, D_MODEL), row),
            lay.mod_spec(layer, tm, 3), lay.mod_spec(layer, tm, 4), lay.mod_spec(layer, tm, 5),
            pl.BlockSpec((None, D_MODEL, D_FF), lambda i: (j, 0, 0), pipeline_mode=once),
            pl.BlockSpec((None, D_MODEL, D_FF), lambda i: (j, 0, 0), pipeline_mode=once),
            pl.BlockSpec((None, D_FF, D_MODEL), lambda i: (j, 0, 0), pipeline_mode=once),
            pl.BlockSpec((None, 1, D_MODEL), lambda i: (layer, 0, 0)),
            pl.BlockSpec((None, 1, D_MODEL), lambda i: (layer, 0, 0)),
        ],
        out_specs=pl.BlockSpec((tm, D_MODEL), row),
        out_shape=jax.ShapeDtypeStruct((lay.n_tok, D_MODEL), F32),
        compiler_params=_cparams(("parallel",)),
        name="ffn_ln2",
    )(x, mods, mods, mods, w1, w3, w2, ln_g, ln_b)


def _router_kernel(x_ref, sh_ref, sc_ref, rw_ref, rb_ref, tri_ref, gt_ref, rk_ref, rkt_ref, cnt_ref):
    h = x_ref[...] * (1.0 + sc_ref[...]) + sh_ref[...]
    logits = _dot3(_split(h), _split(rw_ref[...])) + rb_ref[...]
    lane = lax.broadcasted_iota(jnp.int32, logits.shape, 1)
    m1 = jnp.max(logits, axis=-1, keepdims=True)
    i1 = jnp.min(jnp.where(logits == m1, lane, LANES), axis=-1, keepdims=True)
    rest = jnp.where(lane == i1, -jnp.inf, logits)
    m2 = jnp.max(rest, axis=-1, keepdims=True)
    i2 = jnp.min(jnp.where(rest == m2, lane, LANES), axis=-1, keepdims=True)
    e2 = jnp.exp(m2 - m1)
    den = 1.0 + e2
    gates = jnp.where(lane == i1, 1.0 / den, 0.0) + jnp.where(lane == i2, e2 / den, 0.0)
    gt_ref[...] = jnp.transpose(gates)[:N_EXPERTS, :]
    sel =jnp.where(lane == i1, 1.0, 0.0) + jnp.where(lane == i2, 1.0, 0.0)
    rank = _dot(tri_ref[...], sel.astype(BF16))
    rk = jnp.where(sel > 0.0, rank, -1.0)
    rk_ref[...] = rk
    rkt_ref[...] = jnp.transpose(rk)[:N_EXPERTS, :]
    cnt = jnp.sum(sel, axis=0, keepdims=True).astype(jnp.int32)
    cnt_ref[...] = jnp.broadcast_to(cnt, cnt_ref.shape)


def _router(lay, layer, j, x, mods, router_w, router_b, tri, tm, n_rows):
    n_tiles = n_rows // tm
    row = lambda i: (i, 0)
    return pl.pallas_call(
        _router_kernel,
        grid=(n_tiles,),
        in_specs=[
            pl.BlockSpec((tm, D_MODEL), row),
            lay.mod_spec(layer, tm, 3), lay.mod_spec(layer, tm, 4),
            pl.BlockSpec((None, D_MODEL, LANES), lambda i: (j, 0, 0)),
            pl.BlockSpec((None, 1, LANES), lambda i: (j, 0, 0)),
            pl.BlockSpec((tm, tm), lambda i: (0, 0)),
        ],
        out_specs=[
            pl.BlockSpec((None, N_EXPERTS, tm), lambda i: (i, 0, 0)),
            pl.BlockSpec((tm, LANES), row),
            pl.BlockSpec((None, N_EXPERTS, tm), lambda i: (i, 0, 0)),
            pl.BlockSpec((None, 8, LANES), lambda i: (i, 0, 0)),
        ],
        out_shape=[
            jax.ShapeDtypeStruct((n_tiles, N_EXPERTS, tm), F32),
            jax.ShapeDtypeStruct((n_rows, LANES), F32),
            jax.ShapeDtypeStruct((n_tiles, N_EXPERTS, tm), F32),
            jax.ShapeDtypeStruct((n_tiles, 8, LANES), jnp.int32),
        ],
        compiler_params=_cparams(("parallel",)),
        name="moe_router",
    )(x, mods, mods, router_w, router_b, tri)


def _moe_kernel(cnt_ref, x_ref, sh_ref, sc_ref, gate_ref, gt_ref, rk_ref, rkt_ref, w1_ref, w3_ref, w2_ref,
                lng_ref, lnb_ref, o_ref, h_ref, acc_ref, *, blk):
    i = pl.program_id(0)
    e = pl.program_id(1)
    tm = x_ref.shape[0]

    @pl.when(e == 0)
    def _():
        h_ref[...] = (x_ref[...] * (1.0 + sc_ref[...]) + sh_ref[...]).astype(BF16)
        acc_ref[...] = jnp.zeros_like(acc_ref)

    lane = lax.broadcasted_iota(jnp.int32, (tm, LANES), 1)
    rk_col = jnp.sum(jnp.where(lane == e, rk_ref[...], 0.0), axis=-1, keepdims=True)
    rk_row = rkt_ref[pl.ds(e, 1), :]
    g_row = gt_ref[pl.ds(e, 1), :]

    pos_r = lax.broadcasted_iota(jnp.int32, (blk, tm), 0).astype(F32)
    pos_c = lax.broadcasted_iota(jnp.int32, (tm, 2 * blk), 1).astype(F32)
    n_routed = cnt_ref[i * N_EXPERTS + e]
    n_blocks = (n_routed + blk - 1) // blk

    def expert(base):
        hit = rk_row - base == pos_r
        hg = _dot(jnp.where(hit, 1.0, 0.0).astype(BF16), h_ref[...]).astype(BF16)
        a = _dot(hg, w1_ref[...])
        act = (a * jax.nn.sigmoid(a) * _dot(hg, w3_ref[...])).astype(BF16)
        ge = jnp.sum(jnp.where(hit, g_row, 0.0), axis=-1, keepdims=True)
        return (ge * _dot(act, w2_ref[...])).astype(BF16)

    def pair(jp, carry):
        base = (jp * 2 * blk).astype(F32)
        first = expert(base)
        second = lax.cond(2 * jp + 1 < n_blocks, lambda: expert(base + blk),
                          lambda: jnp.zeros((blk, D_MODEL), BF16))
        put = jnp.where(rk_col - base == pos_c, 1.0, 0.0).astype(BF16)
        acc_ref[...] += _dot(put, jnp.concatenate([first, second], axis=0))
        return carry

    lax.fori_loop(0, (n_blocks + 1) // 2, pair, 0)

    @pl.when(e == N_EXPERTS - 1)
    def _():
        z = DN_ALPHA * x_ref[...] + gate_ref[...] * acc_ref[...]
        o_ref[...] = _layer_norm(z, lng_ref[...], lnb_ref[...])


def _moe(lay, layer, j, x, mods, counts, g, rk, rkt, w1, w3, w2, ln_g, ln_b, tm, blk, n_rows):
    row = lambda i, e, cnt: (i, 0)
    grid_spec = pltpu.PrefetchScalarGridSpec(
        num_scalar_prefetch=1,
        grid=(n_rows // tm, N_EXPERTS),
        in_specs=[
            pl.BlockSpec((tm, D_MODEL), row),
            lay.mod_spec(layer, tm, 3), lay.mod_spec(layer, tm, 4), lay.mod_spec(layer, tm, 5),
            pl.BlockSpec((None, N_EXPERTS, tm), lambda i, e, cnt: (i, 0, 0)),
            pl.BlockSpec((tm, LANES), row),
            pl.BlockSpec((None, N_EXPERTS, tm), lambda i, e, cnt: (i, 0, 0)),
            pl.BlockSpec((None, None, D_MODEL, D_FF_EXPERT), lambda i, e, cnt: (j, e, 0, 0)),
            pl.BlockSpec((None, None, D_MODEL, D_FF_EXPERT), lambda i, e, cnt: (j, e, 0, 0)),
            pl.BlockSpec((None, None, D_FF_EXPERT, D_MODEL), lambda i, e, cnt: (j, e, 0, 0)),
            pl.BlockSpec((None, 1, D_MODEL), lambda i, e, cnt: (layer, 0, 0)),
            pl.BlockSpec((None, 1, D_MODEL), lambda i, e, cnt: (layer, 0, 0)),
        ],
        out_specs=pl.BlockSpec((tm, D_MODEL), row),
        scratch_shapes=[pltpu.VMEM((tm, D_MODEL), BF16), pltpu.VMEM((tm, D_MODEL), F32)],
    )
    return pl.pallas_call(
        functools.partial(_moe_kernel, blk=blk),
        grid_spec=grid_spec,
        out_shape=jax.ShapeDtypeStruct((n_rows, D_MODEL), F32),
        compiler_params=_cparams(("parallel", "arbitrary")),
        name="moe_ln2",
    )(counts, x, mods, mods, mods, g, rk, rkt, w1, w3, w2, ln_g, ln_b)


def _rope_tables(ctx_len, seq):
    t = np.arange(seq)
    inv = ROPE_THETA ** (-np.arange(0, AXIS_DIM, 2, dtype=np.float64) / AXIS_DIM)
    ang_r = (t // GRID_W)[:, None].astype(np.float64) * inv
    ang_c = (t % GRID_W)[:, None].astype(np.float64) * inv
    cos = np.concatenate([np.cos(ang_r), np.cos(ang_r), np.cos(ang_c), np.cos(ang_c)], axis=1)
    sin = np.concatenate([-np.sin(ang_r), np.sin(ang_r), -np.sin(ang_c), np.sin(ang_c)], axis=1)
    cos = np.concatenate([cos, np.ones((ctx_len, ATT_QK))], axis=0)
    sin = np.concatenate([sin, np.zeros((ctx_len, ATT_QK))], axis=0)
    reps = LANES // ATT_QK
    return (jnp.asarray(np.tile(cos, (1, reps)), F32), jnp.asarray(np.tile(sin, (1, reps)), F32))


def _tiles(batch, ctx_len, seq):
    tm = 4 * SCAN_CHUNK
    tm_wide = 2 * tm
    assert (batch * ctx_len) % tm_wide == 0 and seq % tm_wide == 0
    tq = 512
    tq_ctx = 256
    tk = next(t for t in (512, 256) if seq % t == 0)
    tm_moe = next(t for t in (1024, 512, 256) if (batch * ctx_len) % t == 0 and seq % t == 0)
    moe_blk = LANES
    scan_chunks_per_step = 4
    assert ctx_len % tm == 0 and seq % tm == 0 and ctx_len % tq_ctx == 0 and seq % tq == 0 and seq % tk == 0
    assert (batch * seq) % tq_ctx == 0
    assert seq % ctx_len == 0 and ctx_len % (SCAN_CHUNK * scan_chunks_per_step) == 0 and seq % GRID_W == 0
    return tm, tm_wide, tq, tq_ctx, tk, tm_moe, moe_blk, scan_chunks_per_step


def kernel(x, c, ctx, c_ctx, w_ada, b_ada, w_in, w_out, ln1_g, ln1_b, ln2_g, ln2_b,
           lam_q1, lam_k1, lam_q2, lam_k2, subln_g, rkv_conv, decay_w0, decay_up, iclr_a0, iclr_up,
           gate_up, k_k, k_a, r_k, gn_g, gn_b, conv_w, ffn_w1, ffn_w3, ffn_w2,
           router_w, router_b, moe_w1, moe_w3, moe_w2):
    batch, seq, d = x.shape
    ctx_len = ctx.shape[1]
    depth = w_in.shape[0]
    assert d == D_MODEL and depth == DEPTH and batch < MOD_ROWS
    tm, tm_wide, tq, tq_ctx, tk, tm_moe, moe_blk, scan_chunks_per_step = _tiles(batch, ctx_len, seq)
    tri = jnp.asarray(np.tri(tm_moe, k=-1), BF16)
    lay = _Layout(batch, ctx_len, seq)

    tokens = jnp.concatenate([x.reshape(batch * seq, d), ctx.reshape(batch * ctx_len, d)], axis=0)

    c_all = jnp.concatenate([c, c_ctx[None, :], jnp.zeros((MOD_ROWS - batch - 1, d), F32)], axis=0)
    mods = _adaln(c_all, w_ada, b_ada).reshape(depth * MOD_ROWS, 1, 6 * d)

    cos, sin = _rope_tables(ctx_len, seq)

    o = np.cumsum([0, 512, 512, 512, 256, 256, 256, 64, 64, 64, 256, 256, 256])
    wq = w_in[:, :, :o[3]].astype(BF16)
    wr = jnp.concatenate([w_in[:, :, o[3]:o[9]], jnp.zeros((depth, d, RW_W - (o[9] - o[3])), F32)],
                         axis=-1).astype(BF16)
    wc = w_in[:, :, o[9]:].astype(BF16)
    w_out_b = w_out.astype(BF16)

    z = jnp.zeros((depth, DECAY_LORA, RWKV_WIDTH), F32)
    wd = jnp.concatenate([
        jnp.concatenate([decay_up[:, 0], z], axis=-1),
        jnp.concatenate([z, decay_up[:, 1]], axis=-1),
        jnp.zeros((depth, 2 * ICLR_LORA, 2 * RWKV_WIDTH), F32)], axis=1)
    wa = jnp.concatenate([
        jnp.zeros((depth, 2 * DECAY_LORA, 2 * RWKV_WIDTH), F32),
        jnp.concatenate([iclr_up[:, 0], z], axis=-1),
        jnp.concatenate([z, iclr_up[:, 1]], axis=-1)], axis=1)
    wg = jnp.concatenate([gate_up, jnp.zeros((depth, LANES - GATE_LORA, RWKV_WIDTH), F32)], axis=1)
    w0 = decay_w0.reshape(depth, 1, 2 * RWKV_WIDTH)
    a0 = iclr_a0.reshape(depth, 1, 2 * RWKV_WIDTH)
    vec = lambda w: w.reshape(depth, 1, -1)

    lam4 = jnp.stack([lam_q1, lam_k1, lam_q2, lam_k2], axis=1)

    ffn_w1_b, ffn_w3_b, ffn_w2_b = ffn_w1.astype(BF16), ffn_w3.astype(BF16), ffn_w2.astype(BF16)
    moe_w1_b, moe_w3_b, moe_w2_b = moe_w1.astype(BF16), moe_w3.astype(BF16), moe_w2.astype(BF16)
    n_moe = router_w.shape[0]
    router_w_p = jnp.concatenate([router_w, jnp.zeros((n_moe, d, LANES - N_EXPERTS), F32)], axis=-1)
    router_b_p = jnp.concatenate([router_b, jnp.full((n_moe, LANES - N_EXPERTS), -1e30, F32)],
                                 axis=-1).reshape(n_moe, 1, LANES)

    xs = tokens
    for l in range(depth):
        lam_init = 0.8 - 0.6 * math.exp(-0.3 * l)
        qkv, rw, cv = _inproj(lay, l, xs, mods, wq, wr, wc, cos, sin, tm)
        last = l == depth - 1
        att_lat = _attention(lay, l, qkv, lam4, vec(subln_g), lam_init, tq, tk, latent=True)
        att_ctx = att_lat if last else _attention(lay, l, qkv, lam4, vec(subln_g), lam_init,
                                                  tq_ctx, tk, latent=False)
        bv, g, co, *terms = _prep_chunk_terms(lay, l, rw, cv, rkv_conv, w0, wd, a0, wa, wg,
                                              vec(k_k), vec(k_a), vec(r_k), conv_w, tm)
        yf, yb = _scan(lay, *terms, scan_chunks_per_step)
        last_moe = l == depth - 1 and l % 2 == 1
        n_rows = lay.n_lat if last_moe else lay.n_tok
        xs = _outproj(lay, l, att_lat, att_ctx, yf, yb, bv, g, co, xs, mods, w_out_b, vec(gn_g), vec(gn_b),
                      vec(ln1_g), vec(ln1_b), tm_wide, n_rows)
        if l % 2 == 0:
            xs = _ffn(lay, l, l // 2, xs, mods, ffn_w1_b, ffn_w3_b, ffn_w2_b, vec(ln2_g), vec(ln2_b), tm_wide)
        else:
            g_tok, rk, rkt, cnt = _router(lay, l, l // 2, xs, mods, router_w_p, router_b_p, tri, tm_moe,
                                          n_rows)
            counts = cnt[:, 0, :N_EXPERTS].reshape(-1)
            xs = _moe(lay, l, l // 2, xs, mods, counts, g_tok, rk, rkt, moe_w1_b, moe_w3_b, moe_w2_b,
                      vec(ln2_g), vec(ln2_b), tm_moe, moe_blk, n_rows)
    return xs[:lay.n_lat].reshape(batch, seq, d)
```

```python
import functools
import math

import numpy as np
import jax
import jax.numpy as jnp
from jax import lax
from jax.experimental import pallas as pl
from jax.experimental.pallas import tpu as pltpu

F32 = jnp.float32
BF16 = jnp.bfloat16

D_MODEL = 1024
DEPTH = 4
GRID_W = 64
ATT_HEADS = 4
ATT_QK = 64
ATT_V = 128
ATT_WIDTH = 512
AXIS_DIM = 32
ROPE_THETA = 10000.0
SUBLN_EPS = 1e-5
RWKV_HEADS = 4
RWKV_N = 64
RWKV_WIDTH = 256
DECAY_LORA = 32
ICLR_LORA = 32
GATE_LORA = 64
GN_EPS = 64e-5
CONV_WIDTH = 256
D_FF = 2816
N_EXPERTS = 8
D_FF_EXPERT = 1408
DN_ALPHA = (2 * DEPTH) ** 0.25
LN_EPS = 1e-5

LANES = 128
MOD_ROWS = 16
QKV_W = 3 * ATT_WIDTH
RW_W = 1024
CV_W = 3 * CONV_WIDTH
SCAN_CHUNK = 64
VMEM_LIMIT = 56 * 1024 * 1024


def _cparams(sem):
    return pltpu.CompilerParams(dimension_semantics=sem, vmem_limit_bytes=VMEM_LIMIT)


def _dot(a, b):
    return jnp.dot(a, b, preferred_element_type=F32)


def _split3(a):
    hi = a.astype(BF16)
    rest = a - hi.astype(F32)
    mid = rest.astype(BF16)
    return hi, mid, (rest - mid.astype(F32)).astype(BF16)


def _dot_mask_lhs(mask, a):
    m = jnp.where(mask, 1.0, 0.0).astype(BF16)
    return sum(jnp.dot(m, part, preferred_element_type=F32) for part in _split3(a))


def _dot_mask_rhs(a, mask):
    m = mask.astype(BF16)
    return sum(jnp.dot(part, m, preferred_element_type=F32) for part in _split3(a))


def _layer_norm(z, g, b):
    mu = jnp.mean(z, axis=-1, keepdims=True)
    zc = z - mu
    var = jnp.mean(zc * zc, axis=-1, keepdims=True)
    return zc * lax.rsqrt(var + LN_EPS) * g + b


def _head_ones(width, head):
    r = lax.broadcasted_iota(jnp.int32, (width, width), 0) // head
    c = lax.broadcasted_iota(jnp.int32, (width, width), 1) // head
    return (r == c).astype(F32)


def _ada_kernel(c_ref, w_ref, b_ref, o_ref):
    c = c_ref[...]
    sc = c * jax.nn.sigmoid(c)
    o_ref[...] = _dot(sc.astype(BF16), w_ref[...].astype(BF16)) + b_ref[...]


def _adaln(c_all, w_ada, b_ada):
    depth, d, n = w_ada.shape
    tn = 1536
    return pl.pallas_call(
        _ada_kernel,
        grid=(depth, n // tn),
        in_specs=[
            pl.BlockSpec((MOD_ROWS, d), lambda l, j: (0, 0)),
            pl.BlockSpec((None, d, tn), lambda l, j: (l, 0, j)),
            pl.BlockSpec((None, 1, tn), lambda l, j: (l, 0, j)),
        ],
        out_specs=pl.BlockSpec((None, MOD_ROWS, tn), lambda l, j: (l, 0, j)),
        out_shape=jax.ShapeDtypeStruct((depth, MOD_ROWS, n), F32),
        compiler_params=_cparams(("parallel", "parallel")),
        name="adaln",
    )(c_all, w_ada, b_ada.reshape(depth, 1, n))


class _Layout:
    def __init__(self, batch, ctx_len, seq):
        self.batch, self.ctx_len, self.seq = batch, ctx_len, seq
        self.n_lat = batch * seq
        self.n_tok = self.n_lat + batch * ctx_len

    def mod_spec(self, layer, tm, which):
        n_lat_tiles = self.n_lat // tm
        rows_per = self.seq // tm
        batch = self.batch

        def imap(i, *_):
            row = jnp.where(i < n_lat_tiles, i // rows_per, batch)
            return (layer * MOD_ROWS + row, 0, which)

        return pl.BlockSpec((None, 1, D_MODEL), imap)


def _inproj_kernel(x_ref, sh_ref, sc_ref, wq_ref, wr_ref, wc_ref, cos_ref, sin_ref,
                   oq_ref, or_ref, oc_ref):
    h = (x_ref[...] * (1.0 + sc_ref[...]) + sh_ref[...]).astype(BF16)
    qkv = _dot(h, wq_ref[...])
    cos = cos_ref[...]
    sin = sin_ref[...]
    lane = lax.broadcasted_iota(jnp.int32, cos.shape, 1)
    first_half = (lane % AXIS_DIM) < (AXIS_DIM // 2)
    for s in range(2 * ATT_WIDTH // LANES):
        seg = qkv[:, s * LANES:(s + 1) * LANES]
        swapped = jnp.where(first_half, pltpu.roll(seg, LANES - AXIS_DIM // 2, 1),
                            pltpu.roll(seg, AXIS_DIM // 2, 1))
        rot = seg * cos + swapped * sin
        if s < ATT_WIDTH // LANES:
            rot = rot * (ATT_QK ** -0.5 * math.log2(math.e))
        oq_ref[:, s * LANES:(s + 1) * LANES] = rot.astype(BF16)
    oq_ref[:, 2 * ATT_WIDTH:] = qkv[:, 2 * ATT_WIDTH:].astype(BF16)
    or_ref[...] = _dot(h, wr_ref[...])
    oc_ref[...] = _dot(h, wc_ref[...])


def _inproj(lay, layer, x, mods, wq, wr, wc, cos, sin, tm):
    n_lat_tiles = lay.n_lat // tm
    lat_tiles_per = lay.seq // tm

    def rope_map(i):
        return (jnp.where(i < n_lat_tiles, i % lat_tiles_per, lat_tiles_per), 0)

    row = lambda i: (i, 0)
    wmap = lambda i: (layer, 0, 0)
    return pl.pallas_call(
        _inproj_kernel,
        grid=(lay.n_tok // tm,),
        in_specs=[
            pl.BlockSpec((tm, D_MODEL), row),
            lay.mod_spec(layer, tm, 0),
            lay.mod_spec(layer, tm, 1),
            pl.BlockSpec((None, D_MODEL, QKV_W), wmap),
            pl.BlockSpec((None, D_MODEL, RW_W), wmap),
            pl.BlockSpec((None, D_MODEL, CV_W), wmap),
            pl.BlockSpec((tm, LANES), rope_map),
            pl.BlockSpec((tm, LANES), rope_map),
        ],
        out_specs=[
            pl.BlockSpec((tm, QKV_W), row),
            pl.BlockSpec((tm, RW_W), row),
            pl.BlockSpec((tm, CV_W), row),
        ],
        out_shape=[
            jax.ShapeDtypeStruct((lay.n_tok, QKV_W), BF16),
            jax.ShapeDtypeStruct((lay.n_tok, RW_W), F32),
            jax.ShapeDtypeStruct((lay.n_tok, CV_W), F32),
        ],
        compiler_params=_cparams(("parallel",)),
        name="inproj",
    )(x, mods, mods, wq, wr, wc, cos, sin)


def _attn_kernel(*refs, lam_init, tq, tk):
    n_kv = (len(refs) - 4) // 3
    q_ref = refs[0]
    k_refs = refs[1:1 + 2 * n_kv:2]
    v_refs = refs[2:2 + 2 * n_kv:2]
    lam_ref, g_ref, o_ref = refs[1 + 2 * n_kv:4 + 2 * n_kv]
    vx_refs = refs[4 + 2 * n_kv:]

    @pl.when(pl.program_id(2) == 0)
    def _():
        for v_ref, vx_ref in zip(v_refs, vx_refs):
            vx_ref[:, :ATT_V] = v_ref[...]
            vx_ref[:, ATT_V:] = jnp.ones((vx_ref.shape[0], LANES), BF16)

    q = q_ref[...]
    lane = lax.broadcasted_iota(jnp.int32, q.shape, 1)
    zero = jnp.zeros_like(q)
    qq = jnp.concatenate([jnp.where(lane < ATT_QK, q, zero), jnp.where(lane >= ATT_QK, q, zero)], axis=0)
    lam4 = lam_ref[...]
    lam = (jnp.exp(jnp.sum(lam4[0:1] * lam4[1:2], axis=-1, keepdims=True))
           - jnp.exp(jnp.sum(lam4[2:3] * lam4[3:4], axis=-1, keepdims=True)) + lam_init)

    def scores(blk):
        k_ref, _, start, size = blk
        return lax.dot_general(qq, k_ref[start:start + size, :], _NT, preferred_element_type=F32)

    def attend(blocks):
        s = scores(blocks[0])
        m = acc = None
        for j, (_, vx_ref, start, size) in enumerate(blocks):
            s_next = scores(blocks[j + 1]) if j + 1 < len(blocks) else None
            bm = jnp.max(s, axis=-1, keepdims=True)
            m_new = bm if m is None else jnp.maximum(m, bm)
            p = jnp.exp2(s - m_new).astype(BF16)
            pv = _dot(p, vx_ref[start:start + size, :])
            acc = pv if m is None else acc * jnp.exp2(m - m_new) + pv
            m, s = m_new, s_next
        o = acc[:, :ATT_V] / acc[:, ATT_V:]
        dlt = o[:tq] - lam * o[tq:]
        ms = jnp.mean(dlt * dlt, axis=-1, keepdims=True)
        o_ref[...] = (dlt * lax.rsqrt(ms + SUBLN_EPS) * g_ref[...] * (1.0 - lam_init)).astype(o_ref.dtype)

    blocks = []
    for k_ref, vx_ref in zip(k_refs, vx_refs):
        step = min(tk, k_ref.shape[0])
        blocks += [(k_ref, vx_ref, s0, step) for s0 in range(0, k_ref.shape[0], step)]
    attend(blocks)


def _attention(lay, layer, qkv, lam4, subln_g, lam_init, tq, tk, latent):
    seg = lay.seq if latent else lay.ctx_len
    n_q = seg // tq
    q_blk0 = 0 if latent else lay.n_lat // tq
    ctx_blk0 = lay.n_lat // lay.ctx_len
    hq = ATT_WIDTH // LANES

    kv_specs = [pl.BlockSpec((lay.ctx_len, LANES), lambda b, h, qi: (ctx_blk0 + b, hq + h)),
                pl.BlockSpec((lay.ctx_len, LANES), lambda b, h, qi: (ctx_blk0 + b, 2 * hq + h))]
    scratch = [pltpu.VMEM((lay.ctx_len, ATT_V + LANES), BF16)]
    if latent:
        kv_specs += [pl.BlockSpec((lay.seq, LANES), lambda b, h, qi: (b, hq + h)),
                     pl.BlockSpec((lay.seq, LANES), lambda b, h, qi: (b, 2 * hq + h))]
        scratch += [pltpu.VMEM((lay.seq, ATT_V + LANES), BF16)]
    kern = functools.partial(_attn_kernel, lam_init=lam_init, tq=tq, tk=tk)
    return pl.pallas_call(
        kern,
        grid=(lay.batch, ATT_HEADS, n_q),
        in_specs=[pl.BlockSpec((tq, LANES), lambda b, h, qi: (q_blk0 + b * n_q + qi, h))] + kv_specs + [
            pl.BlockSpec((None, 4, ATT_QK), lambda b, h, qi: (layer, 0, 0)),
            pl.BlockSpec((None, 1, ATT_V), lambda b, h, qi: (layer, 0, 0)),
        ],
        out_specs=pl.BlockSpec((tq, LANES), lambda b, h, qi: (b * n_q + qi, h)),
        out_shape=jax.ShapeDtypeStruct((lay.batch * seg, ATT_WIDTH), BF16),
        scratch_shapes=scratch,
        compiler_params=_cparams(("parallel", "parallel", "arbitrary")),
        name="diff_attn" if latent else "diff_attn_ctx",
    )(*([qkv] * (1 + len(kv_specs))), lam4, subln_g)


def _prep_kernel(rw_ref, rwp_ref, rwn_ref, cv_ref, cvp_ref, cvn_ref,
                 rkvw_ref, w0_ref, wd_ref, a0_ref, wa_ref, wg_ref, kk_w_ref, ka_ref, rk_ref, cw_ref,
                 r_ref, v_ref, kk_ref, lw_ref, bb_ref, kd_ref, bv_ref, g_ref, co_ref,
                 *, tm, n_lat, ctx_len, seq):
    i = pl.program_id(0)
    start = i * tm
    in_lat = start < n_lat
    seg_pos = jnp.where(in_lat, start % seq, (start - n_lat) % ctx_len)
    seg_len = jnp.where(in_lat, seq, ctx_len)
    has_prev = seg_pos != 0
    has_next = seg_pos + tm != seg_len

    def shifted(u, prev_row, next_row):
        rows = lax.broadcasted_iota(jnp.int32, u.shape, 0)
        prev_row = jnp.where(has_prev, prev_row, jnp.zeros_like(prev_row))
        next_row = jnp.where(has_next, next_row, jnp.zeros_like(next_row))
        up = jnp.where(rows == 0, prev_row, pltpu.roll(u, 1, 0))
        un = jnp.where(rows == tm - 1, next_row, pltpu.roll(u, tm - 1, 0))
        return up, un

    def conv3(u, prev_row, next_row, w):
        up, un = shifted(u, prev_row, next_row)
        return up * w[0:1] + u * w[1:2] + un * w[2:3]

    rw = rw_ref[...]
    nrkv = 3 * RWKV_WIDTH
    rkv = conv3(rw[:, :nrkv], rwp_ref[7:8, :nrkv], rwn_ref[0:1, :nrkv], rkvw_ref[...])
    r = rkv[:, :RWKV_WIDTH]
    k = rkv[:, RWKV_WIDTH:2 * RWKV_WIDTH]
    v = rkv[:, 2 * RWKV_WIDTH:]

    lora = rw[:, nrkv:nrkv + LANES]
    gate = rw[:, nrkv + LANES:]
    wl = _dot3(_split(jnp.tanh(lora)), _split(wd_ref[...])) + w0_ref[...]
    lw = (-math.exp(-0.5)) * jax.nn.sigmoid(wl)
    a = jax.nn.sigmoid(_dot3(_split(lora), _split(wa_ref[...])) + a0_ref[...])
    g = _dot3(_split(jax.nn.sigmoid(gate)), _split(wg_ref[...]))

    ones = _head_ones(RWKV_WIDTH, RWKV_N)
    kraw = k * kk_w_ref[...]
    ss = _dot_mask_rhs(kraw * kraw, ones)
    kk = kraw * lax.rsqrt(jnp.maximum(ss, 1e-24))
    k2 = jnp.concatenate([k, k], axis=1)
    ka2 = jnp.concatenate([ka_ref[...], ka_ref[...]], axis=1)
    kd = k2 * (1.0 + (a - 1.0) * ka2)
    bb = jnp.concatenate([kk, kk], axis=1) * a
    bonus = _dot_mask_rhs(r * (kd[:, :RWKV_WIDTH] + kd[:, RWKV_WIDTH:]) * rk_ref[...], ones)

    r_ref[...] = r
    v_ref[...] = v
    kk_ref[...] = kk
    lw_ref[...] = lw
    bb_ref[...] = bb
    kd_ref[...] = kd
    bv_ref[...] = bonus * v
    g_ref[...] = g

    def gated(ref):
        return ref[:, 2 * CONV_WIDTH:] * ref[:, :CONV_WIDTH]

    cv_u = gated(cv_ref)
    conv = conv3(cv_u, gated(cvp_ref)[7:8], gated(cvn_ref)[0:1], cw_ref[...])
    co_ref[...] = (cv_ref[:, CONV_WIDTH:2 * CONV_WIDTH] * conv).astype(co_ref.dtype)


def _split(a):
    hi = a.astype(BF16)
    return hi, (a - hi.astype(F32)).astype(BF16)


_NN = (((1,), (0,)), ((), ()))
_NT = (((1,), (1,)), ((), ()))
_TN = (((0,), (0,)), ((), ()))


def _dot3(a, b, dims=_NN):
    (ah, al), (bh, bl) = a, b
    dg = lambda p, q: lax.dot_general(p, q, dims, preferred_element_type=F32)
    return dg(ah, bh) + dg(ah, bl) + dg(al, bh)


def _dot1(a, b, dims=_NN):
    return lax.dot_general(a.astype(BF16), b.astype(BF16), dims, preferred_element_type=F32)


def _chunk_terms_kernel(r_ref, v_ref, kk_ref, lw_ref, bb_ref, kd_ref, g_ref, h_ref, rq_ref, y0_ref):
    c = SCAN_CHUNK
    n = RWKV_N
    row = lax.broadcasted_iota(jnp.int32, (c, c), 0)
    col = lax.broadcasted_iota(jnp.int32, (c, c), 1)
    row2 = lax.broadcasted_iota(jnp.int32, (c, 2 * c), 0)
    col2 = lax.broadcasted_iota(jnp.int32, (c, 2 * c), 1) % c
    eye = lax.broadcasted_iota(jnp.int32, (n, n), 0) == lax.broadcasted_iota(jnp.int32, (n, n), 1)
    n_chunks = r_ref.shape[0] // c

    chains = []
    for ck in range(n_chunks):
        rows = slice(ck * c, (ck + 1) * c)
        r_all = r_ref[rows, :]
        v_all = v_ref[rows, :]
        kk = kk_ref[rows, :]
        for d in range(2):
            sgn = 1 if d == 0 else -1
            dsl = slice(d * RWKV_WIDTH, (d + 1) * RWKV_WIDTH)
            incl = (col - row) * sgn <= 0
            lw = lw_ref[rows, dsl]
            bb = bb_ref[rows, dsl]
            kd = kd_ref[rows, dsl]
            lp = _dot_mask_lhs(incl, lw)
            lt = jnp.sum(lw, axis=0, keepdims=True)
            p_inv = jnp.exp(-lp)
            p_end = jnp.exp(lt - lp)
            a_t = -kk * jnp.exp(lp - lw)
            b_t = bb * p_inv
            k_t = kd * p_inv
            r_t = r_all * jnp.exp(lp)
            k_e = kd * p_end
            b_e = bb * p_end
            p_tot = jnp.exp(lt)
            for h in range(RWKV_HEADS):
                sl = slice(h * n, (h + 1) * n)
                chains.append(dict(
                    strict=(col - row) * sgn < 0, incl2=(col2 - row2) * sgn <= 0,
                    ah=a_t[:, sl], rh=r_t[:, sl], vh=v_all[:, sl], bt=b_t[:, sl], kt=k_t[:, sl],
                    be=b_e[:, sl], ke=k_e[:, sl], ptot=p_tot[:, sl]))

    for ch in chains:
        ch["sc"] = _dot1(jnp.concatenate([ch["ah"], ch["rh"]], axis=0),
                         jnp.concatenate([ch["bt"], ch["kt"]], axis=0), _NT)
    for ch in chains:
        sc = ch["sc"]
        ch["nmat"] = jnp.where(ch["strict"], sc[:c, :c], 0.0).astype(BF16)
        ch["mr"] = jnp.where(ch["incl2"], sc[c:, :], 0.0).astype(BF16)
        ch["mkv"] = _dot1(jnp.where(ch["strict"], sc[:c, c:], 0.0), ch["vh"])
    for ch in chains:
        ch["z"] = jnp.concatenate([ch["ah"], ch["mkv"]], axis=1)
    n_factors = c.bit_length() - 1
    for p in range(n_factors):
        for ch in chains:
            ch["z"] = ch["z"] + _dot1(ch["nmat"], ch["z"])
        if p < n_factors - 1:
            for ch in chains:
                ch["nmat"] = _dot1(ch["nmat"], ch["nmat"]).astype(BF16)
    for ch in chains:
        z = ch["z"]
        ch["ws"] = z[:, :n].astype(BF16)
        ch["uv"] = jnp.concatenate([z[:, n:], ch["vh"]], axis=0).astype(BF16)
    for ch in chains:
        mr = ch["mr"]
        ch["rq"] = ch["rh"] + _dot1(mr[:, :c], ch["ws"])
        ch["y0"] = _dot1(mr, ch["uv"])
        ch["g"] = jnp.where(eye, ch["ptot"], 0.0) + _dot1(ch["ws"], ch["be"], _TN)
        ch["h"] = _dot1(ch["uv"], jnp.concatenate([ch["be"], ch["ke"]], axis=0), _TN)
    for ck in range(n_chunks):
        rows = slice(ck * c, (ck + 1) * c)
        for d in range(2):
            first = (ck * 2 + d) * RWKV_HEADS
            part = chains[first:first + RWKV_HEADS]
            for ref, key in ((g_ref, "g"), (h_ref, "h")):
                top = jnp.concatenate([ch[key] for ch in part], axis=1)
                if c > n:
                    top = jnp.concatenate([top, jnp.zeros((c - n, RWKV_WIDTH), F32)], axis=0)
                ref[d, rows, :] = top
            rq_ref[d, rows, :] = jnp.concatenate([ch["rq"] for ch in part], axis=1)
            y0_ref[d, rows, :] = jnp.concatenate([ch["y0"] for ch in part], axis=1)


def _prep_chunk_kernel(*refs, tm, n_lat, ctx_len, seq):
    ins, (bv_ref, g_ref, co_ref), terms, scratch = refs[:16], refs[16:19], refs[19:23], refs[23:]
    _prep_kernel(*ins, *scratch, bv_ref, g_ref, co_ref, tm=tm, n_lat=n_lat, ctx_len=ctx_len, seq=seq)
    _chunk_terms_kernel(*scratch, *terms)


def _prep_chunk_terms(lay, layer, rw, cv, rkv_conv, w0, wd, a0, wa, wg, k_k, k_a, r_k, conv_w, tm):
    n8 = lay.n_tok // 8
    t8 = tm // 8
    row = lambda i: (i, 0)
    prev = lambda i: (jnp.maximum(i * t8 - 1, 0), 0)
    nxt = lambda i: (jnp.minimum((i + 1) * t8, n8 - 1), 0)
    lmap = lambda i: (layer, 0, 0)
    w2 = 2 * RWKV_WIDTH
    kern = functools.partial(_prep_chunk_kernel, tm=tm, n_lat=lay.n_lat, ctx_len=lay.ctx_len, seq=lay.seq)
    f32 = lambda w: jax.ShapeDtypeStruct((lay.n_tok, w), F32)
    term = pl.BlockSpec((2, tm, RWKV_WIDTH), lambda i: (0, i, 0))
    term_shape = jax.ShapeDtypeStruct((2, lay.n_tok, RWKV_WIDTH), F32)
    return pl.pallas_call(
        kern,
        grid=(lay.n_tok // tm,),
        in_specs=[
            pl.BlockSpec((tm, RW_W), row), pl.BlockSpec((8, RW_W), prev), pl.BlockSpec((8, RW_W), nxt),
            pl.BlockSpec((tm, CV_W), row), pl.BlockSpec((8, CV_W), prev), pl.BlockSpec((8, CV_W), nxt),
            pl.BlockSpec((None, 3, 3 * RWKV_WIDTH), lmap),
            pl.BlockSpec((None, 1, w2), lmap),
            pl.BlockSpec((None, LANES, w2), lmap),
            pl.BlockSpec((None, 1, w2), lmap),
            pl.BlockSpec((None, LANES, w2), lmap),
            pl.BlockSpec((None, LANES, RWKV_WIDTH), lmap),
            pl.BlockSpec((None, 1, RWKV_WIDTH), lmap),
            pl.BlockSpec((None, 1, RWKV_WIDTH), lmap),
            pl.BlockSpec((None, 1, RWKV_WIDTH), lmap),
            pl.BlockSpec((None, 3, CONV_WIDTH), lmap),
        ],
        out_specs=[pl.BlockSpec((tm, RWKV_WIDTH), row), pl.BlockSpec((tm, RWKV_WIDTH), row),
                   pl.BlockSpec((tm, CONV_WIDTH), row), term, term, term, term],
        out_shape=[f32(RWKV_WIDTH), f32(RWKV_WIDTH), jax.ShapeDtypeStruct((lay.n_tok, CONV_WIDTH), BF16),
                   term_shape, term_shape, term_shape, term_shape],
        scratch_shapes=[pltpu.VMEM((tm, RWKV_WIDTH), F32)] * 3 + [pltpu.VMEM((tm, w2), F32)] * 3,
        compiler_params=_cparams(("parallel",)),
        name="rwkv_prep_chunk_terms",
    )(rw, rw, rw, cv, cv, cv, rkv_conv, w0, wd, a0, wa, wg, k_k, k_a, r_k, conv_w)


def _scan_kernel(gf_ref, hf_ref, rqf_ref, y0f_ref, gb_ref, hb_ref, rqb_ref, y0b_ref, yf_ref, yb_ref, st_ref):
    @pl.when(pl.program_id(1) == 0)
    def _():
        st_ref[...] = jnp.zeros_like(st_ref)

    dirs = ((gf_ref, hf_ref, rqf_ref, y0f_ref, yf_ref), (gb_ref, hb_ref, rqb_ref, y0b_ref, yb_ref))
    heads = [slice(h * RWKV_N, (h + 1) * RWKV_N) for h in range(RWKV_HEADS)]
    n_sub = gf_ref.shape[0] // SCAN_CHUNK
    state = [[st_ref[d, h] for h in range(RWKV_HEADS)] for d in range(2)]
    for step in range(n_sub):
        rows = [slice(k * SCAN_CHUNK, (k + 1) * SCAN_CHUNK) for k in (step, n_sub - 1 - step)]
        top = [slice(k * SCAN_CHUNK, k * SCAN_CHUNK + RWKV_N) for k in (step, n_sub - 1 - step)]
        s0 = [[_split(state[d][h]) for h in range(RWKV_HEADS)] for d in range(2)]
        state = [[_dot1(s0[d][h][0], dirs[d][0][top[d], sl]) + _dot1(s0[d][h][1], dirs[d][0][top[d], sl])
                  + dirs[d][1][top[d], sl] for h, sl in enumerate(heads)] for d in range(2)]
        for d in range(2):
            rq_ref, y0_ref, y_ref = dirs[d][2:]
            y_ref[rows[d], :] = jnp.concatenate(
                [_dot1(rq_ref[rows[d], sl], s0[d][h][0], _NT) + y0_ref[rows[d], sl]
                 for h, sl in enumerate(heads)], axis=1)
    for d in range(2):
        for h in range(RWKV_HEADS):
            st_ref[d, h] = state[d][h]


def _scan(lay, g, h, rq, y0, chunks_per_step):
    c = SCAN_CHUNK * chunks_per_step
    nc_ctx = lay.ctx_len // c
    nc_lat = lay.seq // c
    lat_blocks = lay.n_lat // c

    def blk(b, d, ci):
        if d == 0:
            return jnp.where(ci < nc_ctx, lat_blocks + b * nc_ctx + ci, b * nc_lat + (ci - nc_ctx))
        return jnp.where(ci < nc_ctx, lat_blocks + b * nc_ctx + (nc_ctx - 1 - ci),
                         b * nc_lat + (nc_lat - 1 - (ci - nc_ctx)))

    def term(d):
        return pl.BlockSpec((None, c, RWKV_WIDTH), lambda b, ci: (d, blk(b, d, ci), 0))

    def yspec(d):
        return pl.BlockSpec((c, RWKV_WIDTH), lambda b, ci: (blk(b, d, ci), 0))

    shp = jax.ShapeDtypeStruct((lay.n_tok, RWKV_WIDTH), F32)
    return pl.pallas_call(
        _scan_kernel,
        grid=(lay.batch, nc_ctx + nc_lat),
        in_specs=[term(0)] * 4 + [term(1)] * 4,
        out_specs=[yspec(0), yspec(1)],
        out_shape=[shp, shp],
        scratch_shapes=[pltpu.VMEM((2, RWKV_HEADS, RWKV_N, RWKV_N), F32)],
        compiler_params=_cparams(("parallel", "arbitrary")),
        name="rwkv_scan",
    )(g, h, rq, y0, g, h, rq, y0)


def _outproj_kernel(att_lat_ref, att_ctx_ref, yf_ref, yb_ref, bv_ref, g_ref, co_ref, x_ref, gate_ref, w_ref,
                    gng_ref, gnb_ref, lng_ref, lnb_ref, o_ref, *, n_lat_tiles):
    att = jnp.where(pl.program_id(0) < n_lat_tiles, att_lat_ref[...], att_ctx_ref[...])
    ones = _head_ones(RWKV_WIDTH, RWKV_N)
    y = yf_ref[...] + yb_ref[...]
    mu = _dot_mask_rhs(y, ones) * (1.0 / RWKV_N)
    yc = y - mu
    var = _dot_mask_rhs(yc * yc, ones) * (1.0 / RWKV_N)
    yn = yc * lax.rsqrt(var + GN_EPS) * gng_ref[...] + gnb_ref[...]
    rwkv = ((yn + bv_ref[...]) * g_ref[...]).astype(BF16)
    mix = (_dot(att, w_ref[:ATT_WIDTH, :])
           + _dot(rwkv, w_ref[ATT_WIDTH:ATT_WIDTH + RWKV_WIDTH, :])
           + _dot(co_ref[...], w_ref[ATT_WIDTH + RWKV_WIDTH:, :]))
    z = DN_ALPHA * x_ref[...] + gate_ref[...] * mix
    o_ref[...] = _layer_norm(z, lng_ref[...], lnb_ref[...])


def _outproj(lay, layer, att_lat, att_ctx, yf, yb, bv, g, co, x, mods, w_out, gn_g, gn_b, ln_g, ln_b, tm, n_rows):
    row = lambda i: (i, 0)
    lmap = lambda i: (layer, 0, 0)
    n_lat_tiles = lay.n_lat // tm
    return pl.pallas_call(
        functools.partial(_outproj_kernel, n_lat_tiles=n_lat_tiles),
        grid=(n_rows // tm,),
        in_specs=[
            pl.BlockSpec((tm, ATT_WIDTH), lambda i: (jnp.minimum(i, n_lat_tiles - 1), 0)),
            pl.BlockSpec((tm, ATT_WIDTH), lambda i: (jnp.maximum(i - n_lat_tiles, 0), 0)),
            pl.BlockSpec((tm, RWKV_WIDTH), row),
            pl.BlockSpec((tm, RWKV_WIDTH), row),
            pl.BlockSpec((tm, RWKV_WIDTH), row),
            pl.BlockSpec((tm, RWKV_WIDTH), row),
            pl.BlockSpec((tm, CONV_WIDTH), row),
            pl.BlockSpec((tm, D_MODEL), row),
            lay.mod_spec(layer, tm, 2),
            pl.BlockSpec((None, D_MODEL, D_MODEL), lmap),
            pl.BlockSpec((None, 1, RWKV_WIDTH), lmap),
            pl.BlockSpec((None, 1, RWKV_WIDTH), lmap),
            pl.BlockSpec((None, 1, D_MODEL), lmap),
            pl.BlockSpec((None, 1, D_MODEL), lmap),
        ],
        out_specs=pl.BlockSpec((tm, D_MODEL), row),
        out_shape=jax.ShapeDtypeStruct((n_rows, D_MODEL), F32),
        compiler_params=_cparams(("parallel",)),
        name="outproj_ln1",
    )(att_lat, att_ctx, yf, yb, bv, g, co, x, mods, w_out, gn_g, gn_b, ln_g, ln_b)


def _ffn_kernel(x_ref, sh_ref, sc_ref, gate_ref, w1_ref, w3_ref, w2_ref, lng_ref, lnb_ref, o_ref, *, n_split):
    x = x_ref[...]
    h = (x * (1.0 + sc_ref[...]) + sh_ref[...]).astype(BF16)
    step = D_FF // n_split
    f = jnp.zeros(x.shape, F32)
    for s in range(n_split):
        cols = slice(s * step, (s + 1) * step)
        a = _dot(h, w1_ref[:, cols])
        act = (a * jax.nn.sigmoid(a) * _dot(h, w3_ref[:, cols])).astype(BF16)
        f = f + _dot(act, w2_ref[cols, :])
    z = DN_ALPHA * x + gate_ref[...] * f
    o_ref[...] = _layer_norm(z, lng_ref[...], lnb_ref[...])


def _ffn(lay, layer, j, x, mods, w1, w3, w2, ln_g, ln_b, tm):
    row = lambda i: (i, 0)
    once = pl.Buffered(1)
    return pl.pallas_call(
        functools.partial(_ffn_kernel, n_split=2),
        grid=(lay.n_tok // tm,),
        in_specs=[
            pl.BlockSpec((tm, D_MODEL), row),
            lay.mod_spec(layer, tm, 3), lay.mod_spec(layer, tm, 4), lay.mod_spec(layer, tm, 5),
            pl.BlockSpec((None, D_MODEL, D_FF), lambda i: (j, 0, 0), pipeline_mode=once),
            pl.BlockSpec((None, D_MODEL, D_FF), lambda i: (j, 0, 0), pipeline_mode=once),
            pl.BlockSpec((None, D_FF, D_MODEL), lambda i: (j, 0, 0), pipeline_mode=once),
            pl.BlockSpec((None, 1, D_MODEL), lambda i: (layer, 0, 0)),
            pl.BlockSpec((None, 1, D_MODEL), lambda i: (layer, 0, 0)),
        ],
        out_specs=pl.BlockSpec((tm, D_MODEL), row),
        out_shape=jax.ShapeDtypeStruct((lay.n_tok, D_MODEL), F32),
        compiler_params=_cparams(("parallel",)),
        name="ffn_ln2",
    )(x, mods, mods, mods, w1, w3, w2, ln_g, ln_b)


def _router_kernel(x_ref, sh_ref, sc_ref, rw_ref, rb_ref, tri_ref, gt_ref, rk_ref, rkt_ref, cnt_ref):
    h = x_ref[...] * (1.0 + sc_ref[...]) + sh_ref[...]
    logits = _dot3(_split(h), _split(rw_ref[...])) + rb_ref[...]
    lane = lax.broadcasted_iota(jnp.int32, logits.shape, 1)
    m1 = jnp.max(logits, axis=-1, keepdims=True)
    i1 = jnp.min(jnp.where(logits == m1, lane, LANES), axis=-1, keepdims=True)
    rest = jnp.where(lane == i1, -jnp.inf, logits)
    m2 = jnp.max(rest, axis=-1, keepdims=True)
    i2 = jnp.min(jnp.where(rest == m2, lane, LANES), axis=-1, keepdims=True)
    e2 = jnp.exp(m2 - m1)
    den = 1.0 + e2
    gates = jnp.where(lane == i1, 1.0 / den, 0.0) + jnp.where(lane == i2, e2 / den, 0.0)
    gt_ref[...] = jnp.transpose(gates)[:N_EXPERTS, :]
    sel =jnp.where(lane == i1, 1.0, 0.0) + jnp.where(lane == i2, 1.0, 0.0)
    rank = _dot(tri_ref[...], sel.astype(BF16))
    rk = jnp.where(sel > 0.0, rank, -1.0)
    rk_ref[...] = rk
    rkt_ref[...] = jnp.transpose(rk)[:N_EXPERTS, :]
    cnt = jnp.sum(sel, axis=0, keepdims=True).astype(jnp.int32)
    cnt_ref[...] = jnp.broadcast_to(cnt, cnt_ref.shape)


def _router(lay, layer, j, x, mods, router_w, router_b, tri, tm, n_rows):
    n_tiles = n_rows // tm
    row = lambda i: (i, 0)
    return pl.pallas_call(
        _router_kernel,
        grid=(n_tiles,),
        in_specs=[
            pl.BlockSpec((tm, D_MODEL), row),
            lay.mod_spec(layer, tm, 3), lay.mod_spec(layer, tm, 4),
            pl.BlockSpec((None, D_MODEL, LANES), lambda i: (j, 0, 0)),
            pl.BlockSpec((None, 1, LANES), lambda i: (j, 0, 0)),
            pl.BlockSpec((tm, tm), lambda i: (0, 0)),
        ],
        out_specs=[
            pl.BlockSpec((None, N_EXPERTS, tm), lambda i: (i, 0, 0)),
            pl.BlockSpec((tm, LANES), row),
            pl.BlockSpec((None, N_EXPERTS, tm), lambda i: (i, 0, 0)),
            pl.BlockSpec((None, 8, LANES), lambda i: (i, 0, 0)),
        ],
        out_shape=[
            jax.ShapeDtypeStruct((n_tiles, N_EXPERTS, tm), F32),
            jax.ShapeDtypeStruct((n_rows, LANES), F32),
            jax.ShapeDtypeStruct((n_tiles, N_EXPERTS, tm), F32),
            jax.ShapeDtypeStruct((n_tiles, 8, LANES), jnp.int32),
        ],
        compiler_params=_cparams(("parallel",)),
        name="moe_router",
    )(x, mods, mods, router_w, router_b, tri)


def _moe_kernel(cnt_ref, x_ref, sh_ref, sc_ref, gate_ref, gt_ref, rk_ref, rkt_ref, w1_ref, w3_ref, w2_ref,
                lng_ref, lnb_ref, o_ref, h_ref, acc_ref, *, blk):
    i = pl.program_id(0)
    e = pl.program_id(1)
    tm = x_ref.shape[0]

    @pl.when(e == 0)
    def _():
        h_ref[...] = (x_ref[...] * (1.0 + sc_ref[...]) + sh_ref[...]).astype(BF16)
        acc_ref[...] = jnp.zeros_like(acc_ref)

    lane = lax.broadcasted_iota(jnp.int32, (tm, LANES), 1)
    rk_col = jnp.sum(jnp.where(lane == e, rk_ref[...], 0.0), axis=-1, keepdims=True)
    rk_row = rkt_ref[pl.ds(e, 1), :]
    g_row = gt_ref[pl.ds(e, 1), :]

    pos_r = lax.broadcasted_iota(jnp.int32, (blk, tm), 0).astype(F32)
    pos_c = lax.broadcasted_iota(jnp.int32, (tm, 2 * blk), 1).astype(F32)
    n_routed = cnt_ref[i * N_EXPERTS + e]
    n_blocks = (n_routed + blk - 1) // blk

    def expert(base):
        hit = rk_row - base == pos_r
        hg = _dot(jnp.where(hit, 1.0, 0.0).astype(BF16), h_ref[...]).astype(BF16)
        a = _dot(hg, w1_ref[...])
        act = (a * jax.nn.sigmoid(a) * _dot(hg, w3_ref[...])).astype(BF16)
        ge = jnp.sum(jnp.where(hit, g_row, 0.0), axis=-1, keepdims=True)
        return (ge * _dot(act, w2_ref[...])).astype(BF16)

    def pair(jp, carry):
        base = (jp * 2 * blk).astype(F32)
        first = expert(base)
        second = lax.cond(2 * jp + 1 < n_blocks, lambda: expert(base + blk),
                          lambda: jnp.zeros((blk, D_MODEL), BF16))
        put = jnp.where(rk_col - base == pos_c, 1.0, 0.0).astype(BF16)
        acc_ref[...] += _dot(put, jnp.concatenate([first, second], axis=0))
        return carry

    lax.fori_loop(0, (n_blocks + 1) // 2, pair, 0)

    @pl.when(e == N_EXPERTS - 1)
    def _():
        z = DN_ALPHA * x_ref[...] + gate_ref[...] * acc_ref[...]
        o_ref[...] = _layer_norm(z, lng_ref[...], lnb_ref[...])


def _moe(lay, layer, j, x, mods, counts, g, rk, rkt, w1, w3, w2, ln_g, ln_b, tm, blk, n_rows):
    row = lambda i, e, cnt: (i, 0)
    grid_spec = pltpu.PrefetchScalarGridSpec(
        num_scalar_prefetch=1,
        grid=(n_rows // tm, N_EXPERTS),
        in_specs=[
            pl.BlockSpec((tm, D_MODEL), row),
            lay.mod_spec(layer, tm, 3), lay.mod_spec(layer, tm, 4), lay.mod_spec(layer, tm, 5),
            pl.BlockSpec((None, N_EXPERTS, tm), lambda i, e, cnt: (i, 0, 0)),
            pl.BlockSpec((tm, LANES), row),
            pl.BlockSpec((None, N_EXPERTS, tm), lambda i, e, cnt: (i, 0, 0)),
            pl.BlockSpec((None, None, D_MODEL, D_FF_EXPERT), lambda i, e, cnt: (j, e, 0, 0)),
            pl.BlockSpec((None, None, D_MODEL, D_FF_EXPERT), lambda i, e, cnt: (j, e, 0, 0)),
            pl.BlockSpec((None, None, D_FF_EXPERT, D_MODEL), lambda i, e, cnt: (j, e, 0, 0)),
            pl.BlockSpec((None, 1, D_MODEL), lambda i, e, cnt: (layer, 0, 0)),
            pl.BlockSpec((None, 1, D_MODEL), lambda i, e, cnt: (layer, 0, 0)),
        ],
        out_specs=pl.BlockSpec((tm, D_MODEL), row),
        scratch_shapes=[pltpu.VMEM((tm, D_MODEL), BF16), pltpu.VMEM((tm, D_MODEL), F32)],
    )
    return pl.pallas_call(
        functools.partial(_moe_kernel, blk=blk),
        grid_spec=grid_spec,
        out_shape=jax.ShapeDtypeStruct((n_rows, D_MODEL), F32),
        compiler_params=_cparams(("parallel", "arbitrary")),
        name="moe_ln2",
    )(counts, x, mods, mods, mods, g, rk, rkt, w1, w3, w2, ln_g, ln_b)


def _rope_tables(ctx_len, seq):
    t = np.arange(seq)
    inv = ROPE_THETA ** (-np.arange(0, AXIS_DIM, 2, dtype=np.float64) / AXIS_DIM)
    ang_r = (t // GRID_W)[:, None].astype(np.float64) * inv
    ang_c = (t % GRID_W)[:, None].astype(np.float64) * inv
    cos = np.concatenate([np.cos(ang_r), np.cos(ang_r), np.cos(ang_c), np.cos(ang_c)], axis=1)
    sin = np.concatenate([-np.sin(ang_r), np.sin(ang_r), -np.sin(ang_c), np.sin(ang_c)], axis=1)
    cos = np.concatenate([cos, np.ones((ctx_len, ATT_QK))], axis=0)
    sin = np.concatenate([sin, np.zeros((ctx_len, ATT_QK))], axis=0)
    reps = LANES // ATT_QK
    return (jnp.asarray(np.tile(cos, (1, reps)), F32), jnp.asarray(np.tile(sin, (1, reps)), F32))


def _tiles(batch, ctx_len, seq):
    tm = 4 * SCAN_CHUNK
    tm_wide = 2 * tm
    assert (batch * ctx_len) % tm_wide == 0 and seq % tm_wide == 0
    tq = 512
    tq_ctx = 256
    tk = next(t for t in (512, 256) if seq % t == 0)
    tm_moe = next(t for t in (1024, 512, 256) if (batch * ctx_len) % t == 0 and seq % t == 0)
    moe_blk = LANES
    scan_chunks_per_step = 4
    assert ctx_len % tm == 0 and seq % tm == 0 and ctx_len % tq_ctx == 0 and seq % tq == 0 and seq % tk == 0
    assert (batch * seq) % tq_ctx == 0
    assert seq % ctx_len == 0 and ctx_len % (SCAN_CHUNK * scan_chunks_per_step) == 0 and seq % GRID_W == 0
    return tm, tm_wide, tq, tq_ctx, tk, tm_moe, moe_blk, scan_chunks_per_step


def kernel(x, c, ctx, c_ctx, w_ada, b_ada, w_in, w_out, ln1_g, ln1_b, ln2_g, ln2_b,
           lam_q1, lam_k1, lam_q2, lam_k2, subln_g, rkv_conv, decay_w0, decay_up, iclr_a0, iclr_up,
           gate_up, k_k, k_a, r_k, gn_g, gn_b, conv_w, ffn_w1, ffn_w3, ffn_w2,
           router_w, router_b, moe_w1, moe_w3, moe_w2):
    batch, seq, d = x.shape
    ctx_len = ctx.shape[1]
    depth = w_in.shape[0]
    assert d == D_MODEL and depth == DEPTH and batch < MOD_ROWS
    tm, tm_wide, tq, tq_ctx, tk, tm_moe, moe_blk, scan_chunks_per_step = _tiles(batch, ctx_len, seq)
    tri = jnp.asarray(np.tri(tm_moe, k=-1), BF16)
    lay = _Layout(batch, ctx_len, seq)

    tokens = jnp.concatenate([x.reshape(batch * seq, d), ctx.reshape(batch * ctx_len, d)], axis=0)

    c_all = jnp.concatenate([c, c_ctx[None, :], jnp.zeros((MOD_ROWS - batch - 1, d), F32)], axis=0)
    mods = _adaln(c_all, w_ada, b_ada).reshape(depth * MOD_ROWS, 1, 6 * d)

    cos, sin = _rope_tables(tm_wide, seq)

    o = np.cumsum([0, 512, 512, 512, 256, 256, 256, 64, 64, 64, 256, 256, 256])
    wq = w_in[:, :, :o[3]].astype(BF16)
    wr = jnp.concatenate([w_in[:, :, o[3]:o[9]], jnp.zeros((depth, d, RW_W - (o[9] - o[3])), F32)],
                         axis=-1).astype(BF16)
    wc = w_in[:, :, o[9]:].astype(BF16)
    w_out_b = w_out.astype(BF16)

    z = jnp.zeros((depth, DECAY_LORA, RWKV_WIDTH), F32)
    wd = jnp.concatenate([
        jnp.concatenate([decay_up[:, 0], z], axis=-1),
        jnp.concatenate([z, decay_up[:, 1]], axis=-1),
        jnp.zeros((depth, 2 * ICLR_LORA, 2 * RWKV_WIDTH), F32)], axis=1)
    wa = jnp.concatenate([
        jnp.zeros((depth, 2 * DECAY_LORA, 2 * RWKV_WIDTH), F32),
        jnp.concatenate([iclr_up[:, 0], z], axis=-1),
        jnp.concatenate([z, iclr_up[:, 1]], axis=-1)], axis=1)
    wg = jnp.concatenate([gate_up, jnp.zeros((depth, LANES - GATE_LORA, RWKV_WIDTH), F32)], axis=1)
    w0 = decay_w0.reshape(depth, 1, 2 * RWKV_WIDTH)
    a0 = iclr_a0.reshape(depth, 1, 2 * RWKV_WIDTH)
    vec = lambda w: w.reshape(depth, 1, -1)

    lam4 = jnp.stack([lam_q1, lam_k1, lam_q2, lam_k2], axis=1)

    ffn_w1_b, ffn_w3_b, ffn_w2_b = ffn_w1.astype(BF16), ffn_w3.astype(BF16), ffn_w2.astype(BF16)
    moe_w1_b, moe_w3_b, moe_w2_b = moe_w1.astype(BF16), moe_w3.astype(BF16), moe_w2.astype(BF16)
    n_moe = router_w.shape[0]
    router_w_p = jnp.concatenate([router_w, jnp.zeros((n_moe, d, LANES - N_EXPERTS), F32)], axis=-1)
    router_b_p = jnp.concatenate([router_b, jnp.full((n_moe, LANES - N_EXPERTS), -1e30, F32)],
                                 axis=-1).reshape(n_moe, 1, LANES)

    xs = tokens
    for l in range(depth):
        lam_init = 0.8 - 0.6 * math.exp(-0.3 * l)
        qkv, rw, cv = _inproj(lay, l, xs, mods, wq, wr, wc, cos, sin, tm_wide)
        last = l == depth - 1
        att_lat = _attention(lay, l, qkv, lam4, vec(subln_g), lam_init, tq, tk, latent=True)
        att_ctx = att_lat if last else _attention(lay, l, qkv, lam4, vec(subln_g), lam_init,
                                                  tq_ctx, tk, latent=False)
        bv, g, co, *terms = _prep_chunk_terms(lay, l, rw, cv, rkv_conv, w0, wd, a0, wa, wg,
                                              vec(k_k), vec(k_a), vec(r_k), conv_w, tm)
        yf, yb = _scan(lay, *terms, scan_chunks_per_step)
        last_moe = l == depth - 1 and l % 2 == 1
        n_rows = lay.n_lat if last_moe else lay.n_tok
        xs = _outproj(lay, l, att_lat, att_ctx, yf, yb, bv, g, co, xs, mods, w_out_b, vec(gn_g), vec(gn_b),
                      vec(ln1_g), vec(ln1_b), tm_wide, n_rows)
        if l % 2 == 0:
            xs = _ffn(lay, l, l // 2, xs, mods, ffn_w1_b, ffn_w3_b, ffn_w2_b, vec(ln2_g), vec(ln2_b), tm_wide)
        else:
            g_tok, rk, rkt, cnt = _router(lay, l, l // 2, xs, mods, router_w_p, router_b_p, tri, tm_moe,
                                          n_rows)
            counts = cnt[:, 0, :N_EXPERTS].reshape(-1)
            xs = _moe(lay, l, l // 2, xs, mods, counts, g_tok, rk, rkt, moe_w1_b, moe_w3_b, moe_w2_b,
                      vec(ln2_g), vec(ln2_b), tm_moe, moe_blk, n_rows)
    return xs[:lay.n_lat].reshape(batch, seq, d)
```
